```python
import math
import jax
import jax.numpy as jnp
from jax import lax
import numpy as np

D_MODEL = 4096
BATCH = 4
SEQ = 4096
DEPTH = 4

HEAD_DIM = 128
N_EVEN = (DEPTH + 1) // 2
N_ODD = DEPTH // 2
A_HEADS = (D_MODEL // 2) // HEAD_DIM
A_DK = HEAD_DIM
A_DV = HEAD_DIM
B_HEADS = (D_MODEL // 2) // HEAD_DIM
B_KV_HEADS = B_HEADS // 4
WINDOW = 128
C_HEADS = 4
C_DK = (D_MODEL // 4) // C_HEADS
C_DV = (D_MODEL // 2) // C_HEADS
GLA_RANK = 16
GLA_TAU = 16.0
D_HEADS = (D_MODEL // 2) // HEAD_DIM
Q_BLOCK = 128
CHUNK = 64
N_BUCKETS = 32
MAX_DISTANCE = 128
N_EXPERTS = 64
TOP_K = 8
N_GROUPS = 8
TOPK_GROUPS = 4
D_EXPERT = 192
D_SHARED = 192
ROUTED_SCALE = 2.5
ALPHA = (2.0 * DEPTH) ** 0.25
BETA = (8.0 * DEPTH) ** -0.25
LN_EPS = 1e-5
RMS_EPS = 1e-6
NEG_BIG = -1e30

AB_WIDTHS = (A_HEADS * A_DK, A_HEADS * A_DK, A_HEADS * A_DV, A_HEADS * A_DV,
             B_HEADS * HEAD_DIM, B_KV_HEADS * HEAD_DIM, B_KV_HEADS * HEAD_DIM)
CD_WIDTHS = (C_HEADS * C_DK, C_HEADS * C_DK, C_HEADS * C_DV, C_HEADS * C_DV, GLA_RANK,
             D_HEADS * HEAD_DIM, D_HEADS * HEAD_DIM, D_HEADS * HEAD_DIM, D_HEADS)
AB_TOTAL = sum(AB_WIDTHS)
CD_TOTAL = sum(CD_WIDTHS)

kernel_name = "hybrid_hgrn2_swa_gla_fox_moe"


def split_cols(t, widths):
    idx = np.cumsum(widths)[:-1].tolist()
    return jnp.split(t, idx, axis=-1)


def layer_norm(x, g, b):
    xf = x.astype(jnp.float32)
    mu = jnp.mean(xf, -1, keepdims=True)
    xc = xf - mu
    var = jnp.mean(xc * xc, -1, keepdims=True)
    return (xc * lax.rsqrt(var + LN_EPS) * g.astype(jnp.float32) + b.astype(jnp.float32)).astype(x.dtype)


def rms_norm(x, g):
    xf = x.astype(jnp.float32)
    return xf * lax.rsqrt(jnp.mean(xf * xf, -1, keepdims=True) + RMS_EPS) * g.astype(jnp.float32)


def chunk_gated_linear_attention(q, k, v, log_f):
    Bn, S, H, dk = q.shape
    dv = v.shape[-1]
    n = S // CHUNK

    def to_chunks(t):
        return t.astype(jnp.float32).reshape(Bn, n, CHUNK, H, t.shape[-1]).transpose(1, 0, 3, 2, 4)

    qc, kc, vc, gc = to_chunks(q), to_chunks(k), to_chunks(v), to_chunks(log_f)
    causal = jnp.tril(jnp.ones((CHUNK, CHUNK), dtype=bool))

    def step(state, inp):
        qb, kb, vb, gb = inp
        b = jnp.cumsum(gb, axis=2)
        o_inter = jnp.einsum('bhtk,bhkv->bhtv', qb * jnp.exp(b), state)
        diff = b[:, :, :, None, :] - b[:, :, None, :, :]
        decay = jnp.exp(jnp.where(causal[:, :, None], diff, NEG_BIG))
        scores = jnp.einsum('bhtk,bhsk,bhtsk->bhts', qb, kb, decay)
        o_intra = jnp.einsum('bhts,bhsv->bhtv', scores, vb)
        b_last = b[:, :, -1]
        k_dec = kb * jnp.exp(b_last[:, :, None, :] - b)
        new_state = state * jnp.exp(b_last)[..., None] + jnp.einsum('bhsk,bhsv->bhkv', k_dec, vb)
        return new_state, o_inter + o_intra

    state0 = jnp.zeros((Bn, H, dk, dv), jnp.float32)
    _, out = lax.scan(step, state0, (qc, kc, vc, gc))
    return out.transpose(1, 0, 3, 2, 4).reshape(Bn, S, H, dv)


def t5_causal_bucket(dist):
    max_exact = N_BUCKETS // 2
    d = jnp.maximum(dist, 0)
    large = max_exact + (jnp.log(jnp.maximum(d, 1).astype(jnp.float32) / max_exact)
                         / math.log(MAX_DISTANCE / max_exact) * (N_BUCKETS - max_exact)).astype(jnp.int32)
    large = jnp.minimum(large, N_BUCKETS - 1)
    return jnp.where(d < max_exact, d, large)


def sliding_window_sink_attention(q, k, v, sinks, rel_bias):
    Bn, S, H, hd = q.shape
    G = H // B_KV_HEADS
    nb = S // WINDOW
    scale = hd ** -0.5
    qf = q.reshape(Bn, nb, WINDOW, B_KV_HEADS, G, hd)

    def banded(t):
        prev = jnp.pad(t, ((0, 0), (WINDOW, 0), (0, 0), (0, 0)))[:, :S]
        return jnp.concatenate([prev.reshape(Bn, nb, WINDOW, B_KV_HEADS, hd),
                                t.reshape(Bn, nb, WINDOW, B_KV_HEADS, hd)], axis=2)

    kb, vb = banded(k), banded(v)
    dist = WINDOW + jnp.arange(WINDOW)[:, None] - jnp.arange(2 * WINDOW)[None, :]
    in_window = (dist >= 0) & (dist < WINDOW)
    bias = rel_bias.astype(jnp.float32)[t5_causal_bucket(dist)]
    bias = bias.transpose(2, 0, 1).reshape(B_KV_HEADS, G, WINDOW, 2 * WINDOW)
    key_pos = jnp.arange(nb)[:, None] * WINDOW - WINDOW + jnp.arange(2 * WINDOW)[None, :]
    mask = in_window[None] & (key_pos >= 0)[:, None, :]
    s = jnp.einsum('bnqhgd,bnkhd->bnhgqk', qf, kb).astype(jnp.float32) * scale + bias
    s = jnp.where(mask[None, :, None, None], s, NEG_BIG)
    sink = sinks.astype(jnp.float32).reshape(B_KV_HEADS, G)[None, None, :, :, None, None]
    m = jnp.maximum(jnp.max(s, -1, keepdims=True), sink)
    p = jnp.exp(s - m)
    p = p / (jnp.sum(p, -1, keepdims=True) + jnp.exp(sink - m))
    o = jnp.einsum('bnhgqk,bnkhd->bnqhgd', p.astype(v.dtype), vb)
    return o.reshape(Bn, S, H * hd)


def forgetting_attention(q, k, v, f_logit):
    Bn, S, H, hd = q.shape
    scale = hd ** -0.5
    c = jnp.cumsum(jax.nn.log_sigmoid(f_logit.astype(jnp.float32)), axis=1).transpose(0, 2, 1)
    nb = S // Q_BLOCK
    kpos = jnp.arange(S)

    def block(i):
        start = i * Q_BLOCK
        qb = lax.dynamic_slice_in_dim(q, start, Q_BLOCK, axis=1)
        cb = lax.dynamic_slice_in_dim(c, start, Q_BLOCK, axis=2)
        s = jnp.einsum('bqhd,bkhd->bhqk', qb, k).astype(jnp.float32) * scale
        s = s + cb[..., None] - c[:, :, None, :]
        qpos = start + jnp.arange(Q_BLOCK)
        s = jnp.where(kpos[None, :] <= qpos[:, None], s, NEG_BIG)
        p = jax.nn.softmax(s, axis=-1)
        return jnp.einsum('bhqk,bkhd->bqhd', p.astype(v.dtype), v)

    o = lax.map(block, jnp.arange(nb))
    return o.transpose(1, 0, 2, 3, 4).reshape(Bn, S, H * hd)


def mixer_ab(x, w_in, b_in, lb, hgrn_norm, sinks, rel_bias, w_out):
    Bn, S, _ = x.shape
    proj = jnp.einsum('bsd,de->bse', x, w_in) + b_in
    qa, fa, ia, ga, qb, kb, vb = split_cols(proj, AB_WIDTHS)
    f = lb + (1.0 - lb) * jax.nn.sigmoid(fa.astype(jnp.float32))
    log_f = jnp.log(f)
    k_in = 1.0 - f
    hA = lambda t: t.reshape(Bn, S, A_HEADS, -1)
    oa = chunk_gated_linear_attention(hA(jax.nn.silu(qa)), hA(k_in), hA(ia), hA(log_f))
    oa = rms_norm(oa, hgrn_norm) * jax.nn.sigmoid(hA(ga).astype(jnp.float32))
    oa = oa.reshape(Bn, S, A_HEADS * A_DV).astype(x.dtype)
    ob = sliding_window_sink_attention(qb.reshape(Bn, S, B_HEADS, HEAD_DIM),
                                       kb.reshape(Bn, S, B_KV_HEADS, HEAD_DIM),
                                       vb.reshape(Bn, S, B_KV_HEADS, HEAD_DIM), sinks, rel_bias)
    o = jnp.concatenate([oa, ob.astype(x.dtype)], axis=-1)
    return jnp.einsum('bse,ed->bsd', o, w_out)


def mixer_cd(x, w_in, b_in, gla_w2, gla_b, gla_norm, w_out):
    Bn, S, _ = x.shape
    proj = jnp.einsum('bsd,de->bse', x, w_in) + b_in
    qc, kc, vc, gc, ac, qd, kd, vd, fd = split_cols(proj, CD_WIDTHS)
    gate_logit = (jnp.einsum('bsr,re->bse', ac, gla_w2) + gla_b).astype(jnp.float32)
    log_alpha = jax.nn.log_sigmoid(gate_logit) / GLA_TAU
    hC = lambda t: t.reshape(Bn, S, C_HEADS, -1)
    oc = chunk_gated_linear_attention(hC(qc.astype(jnp.float32) * (C_DK ** -0.5)), hC(kc), hC(vc), hC(log_alpha))
    oc = rms_norm(oc, gla_norm) * jax.nn.silu(hC(gc).astype(jnp.float32))
    oc = oc.reshape(Bn, S, C_HEADS * C_DV).astype(x.dtype)
    hD = lambda t: t.reshape(Bn, S, D_HEADS, HEAD_DIM)
    od = forgetting_attention(hD(qd), hD(kd), hD(vd), fd)
    o = jnp.concatenate([oc, od.astype(x.dtype)], axis=-1)
    return jnp.einsum('bse,ed->bsd', o, w_out)


def moe_ffn(x, w_router, router_bias, w_g, w_u, w_d, w_sg, w_su, w_sd):
    Bn, S, D = x.shape
    scores = jax.nn.sigmoid(jnp.einsum('bsd,de->bse', x, w_router).astype(jnp.float32))
    biased = scores + router_bias.astype(jnp.float32)
    grp = biased.reshape(Bn, S, N_GROUPS, N_EXPERTS // N_GROUPS)
    grp_score = jnp.sum(lax.top_k(grp, 2)[0], -1)
    _, grp_idx = lax.top_k(grp_score, TOPK_GROUPS)
    grp_mask = jnp.sum(jax.nn.one_hot(grp_idx, N_GROUPS, dtype=jnp.float32), -2)
    expert_ok = jnp.repeat(grp_mask, N_EXPERTS // N_GROUPS, axis=-1) > 0
    _, idx = lax.top_k(jnp.where(expert_ok, biased, NEG_BIG), TOP_K)
    w = jnp.take_along_axis(scores, idx, -1)
    w = w / jnp.sum(w, -1, keepdims=True) * ROUTED_SCALE
    combine = jnp.einsum('bsk,bske->bse', w, jax.nn.one_hot(idx, N_EXPERTS, dtype=jnp.float32))

    def routed(args):
        xb, cb = args
        h = jax.nn.silu(jnp.einsum('sd,edf->sef', xb, w_g)) * jnp.einsum('sd,edf->sef', xb, w_u)
        h = h * cb.astype(h.dtype)[..., None]
        return jnp.einsum('sef,efd->sd', h, w_d)

    y_routed = lax.map(routed, (x, combine))
    y_shared = jnp.einsum('bsf,fd->bsd', jax.nn.silu(x @ w_sg) * (x @ w_su), w_sd)
    return y_routed + y_shared


def setup_inputs(seed: int = 0) -> dict:
    key = jax.random.key(seed)
    ks = jax.random.split(key, 24)
    f32 = jnp.float32
    D = D_MODEL

    def nrm(k, shape, scale):
        return jax.random.normal(k, shape, f32) * scale

    return {
        "x": nrm(ks[0], (BATCH, SEQ, D), 1.0),
        "w_in_ab": nrm(ks[1], (N_EVEN, D, AB_TOTAL), D ** -0.5),
        "b_in_ab": nrm(ks[2], (N_EVEN, AB_TOTAL), 0.02),
        "w_in_cd": nrm(ks[3], (N_ODD, D, CD_TOTAL), D ** -0.5),
        "b_in_cd": nrm(ks[4], (N_ODD, CD_TOTAL), 0.02),
        "hgrn_lb_logits": nrm(ks[5], (DEPTH, A_HEADS * A_DK), 0.1),
        "hgrn_norm": 1.0 + nrm(ks[6], (N_EVEN, A_DV), 0.02),
        "sinks": nrm(ks[7], (N_EVEN, B_HEADS), 1.0),
        "rel_bias": nrm(ks[8], (N_BUCKETS, B_HEADS), 0.5),
        "gla_w2": nrm(ks[9], (N_ODD, GLA_RANK, C_HEADS * C_DK), GLA_RANK ** -0.5),
        "gla_b": nrm(ks[10], (N_ODD, C_HEADS * C_DK), 0.1),
        "gla_norm": 1.0 + nrm(ks[11], (N_ODD, C_DV), 0.02),
        "w_out": nrm(ks[12], (DEPTH, D, D), D ** -0.5 * BETA),
        "ln_g": 1.0 + nrm(ks[13], (DEPTH, 2, D), 0.02),
        "ln_b": nrm(ks[14], (DEPTH, 2, D), 0.02),
        "w_router": nrm(ks[15], (DEPTH, D, N_EXPERTS), D ** -0.5),
        "router_bias": nrm(ks[16], (DEPTH, N_EXPERTS), 0.01),
        "w_exp_gate": nrm(ks[17], (DEPTH, N_EXPERTS, D, D_EXPERT), D ** -0.5),
        "w_exp_up": nrm(ks[18], (DEPTH, N_EXPERTS, D, D_EXPERT), D ** -0.5),
        "w_exp_down": nrm(ks[19], (DEPTH, N_EXPERTS, D_EXPERT, D), D_EXPERT ** -0.5 * BETA),
        "w_sh_gate": nrm(ks[20], (DEPTH, D, D_SHARED), D ** -0.5),
        "w_sh_up": nrm(ks[21], (DEPTH, D, D_SHARED), D ** -0.5),
        "w_sh_down": nrm(ks[22], (DEPTH, D_SHARED, D), D_SHARED ** -0.5 * BETA),
    }


def reference(x, w_in_ab, b_in_ab, w_in_cd, b_in_cd, hgrn_lb_logits, hgrn_norm, sinks, rel_bias,
              gla_w2, gla_b, gla_norm, w_out, ln_g, ln_b, w_router, router_bias,
              w_exp_gate, w_exp_up, w_exp_down, w_sh_gate, w_sh_up, w_sh_down):
    lb_soft = jax.nn.softmax(hgrn_lb_logits.astype(jnp.float32), axis=0)
    lower_bounds = jnp.cumsum(lb_soft, axis=0) - lb_soft[0]
    for l in range(DEPTH):
        j = l // 2
        if l % 2 == 0:
            mix = mixer_ab(x, w_in_ab[j], b_in_ab[j], lower_bounds[l], hgrn_norm[j], sinks[j], rel_bias, w_out[l])
        else:
            mix = mixer_cd(x, w_in_cd[j], b_in_cd[j], gla_w2[j], gla_b[j], gla_norm[j], w_out[l])
        x = layer_norm(ALPHA * x + mix.astype(x.dtype), ln_g[l, 0], ln_b[l, 0])
        ffn = moe_ffn(x, w_router[l], router_bias[l], w_exp_gate[l], w_exp_up[l], w_exp_down[l],
                      w_sh_gate[l], w_sh_up[l], w_sh_down[l])
        x = layer_norm(ALPHA * x + ffn.astype(x.dtype), ln_g[l, 1], ln_b[l, 1])
    return x
```

```python
import functools
import math

import numpy as np
import jax
import jax.numpy as jnp
from jax import lax
from jax.experimental import pallas as pl
from jax.experimental.pallas import tpu as pltpu

F32 = jnp.float32
BF16 = jnp.bfloat16
HIGHEST = lax.Precision.HIGHEST

HEAD_DIM = 128
WINDOW = 128
C_HEADS = 4
GLA_TAU = 16.0
CHUNK = 64
SUB = 16
N_BUCKETS = 32
MAX_DISTANCE = 128
TOP_K = 8
N_GROUPS = 8
TOPK_GROUPS = 4
ROUTED_SCALE = 2.5
LN_EPS = 1e-5
RMS_EPS = 1e-6
NEG_BIG = -1e30

VMEM_LIMIT_BYTES = 52 * 1024 * 1024


def _params(*sem):
    return pltpu.CompilerParams(dimension_semantics=sem, vmem_limit_bytes=VMEM_LIMIT_BYTES)


def _tile(n, pref, align):
    t = min(pref, n)
    t -= t % align
    while t >= align:
        if n % t == 0:
            return t
        t -= align
    return n


def _sigmoid(x):
    return 1.0 / (1.0 + jnp.exp(-x))


def _log_sigmoid(x):
    return jnp.minimum(x, 0.0) - jnp.log(1.0 + jnp.exp(-jnp.abs(x)))


def _dot_nt(a, b, **kw):
    return lax.dot_general(a, b, (((1,), (1,)), ((), ())), preferred_element_type=F32, **kw)


def _dot_tn(a, b, **kw):
    return lax.dot_general(a, b, (((0,), (0,)), ((), ())), preferred_element_type=F32, **kw)


def _mm_kernel(x_ref, w_ref, b_ref, o_ref):
    acc = jnp.dot(x_ref[...], w_ref[...], preferred_element_type=F32)
    o_ref[...] = (acc + b_ref[...]).astype(o_ref.dtype)


def matmul_bias(x, w, b, out_dtype, tm_pref=512, tn_pref=1024):
    m, k = x.shape
    n = w.shape[1]
    tm = _tile(m, tm_pref, 8)
    tn = _tile(n, tn_pref, 128)
    return pl.pallas_call(
        _mm_kernel,
        grid=(n // tn, m // tm),
        in_specs=[pl.BlockSpec((tm, k), lambda j, i: (i, 0)),
                  pl.BlockSpec((k, tn), lambda j, i: (0, j)),
                  pl.BlockSpec((1, tn), lambda j, i: (0, j))],
        out_specs=pl.BlockSpec((tm, tn), lambda j, i: (i, j)),
        out_shape=jax.ShapeDtypeStruct((m, n), out_dtype),
        compiler_params=_params("parallel", "parallel"),
        name="matmul_bias",
    )(x, w, b.reshape(1, n).astype(F32))


def _mm2_kernel(a1_ref, a2_ref, w1_ref, w2_ref, o_ref):
    acc = jnp.dot(a1_ref[...], w1_ref[...], preferred_element_type=F32)
    acc += jnp.dot(a2_ref[...], w2_ref[...], preferred_element_type=F32)
    o_ref[...] = acc.astype(o_ref.dtype)


def matmul_pair(a1, a2, w1, w2, out_dtype, tm_pref=512, tn_pref=1024):
    m, k1 = a1.shape
    k2 = a2.shape[1]
    n = w1.shape[1]
    tm = _tile(m, tm_pref, 8)
    tn = _tile(n, tn_pref, 128)
    return pl.pallas_call(
        _mm2_kernel,
        grid=(n // tn, m // tm),
        in_specs=[pl.BlockSpec((tm, k1), lambda j, i: (i, 0)),
                  pl.BlockSpec((tm, k2), lambda j, i: (i, 0)),
                  pl.BlockSpec((k1, tn), lambda j, i: (0, j)),
                  pl.BlockSpec((k2, tn), lambda j, i: (0, j))],
        out_specs=pl.BlockSpec((tm, tn), lambda j, i: (i, j)),
        out_shape=jax.ShapeDtypeStruct((m, n), out_dtype),
        compiler_params=_params("parallel", "parallel"),
        name="matmul_pair",
    )(a1, a2, w1, w2)


def _add_ln_kernel(x_ref, y_ref, g_ref, b_ref, o_ref, ob_ref, *, alpha):
    z = alpha * x_ref[...] + y_ref[...]
    mu = jnp.mean(z, axis=-1, keepdims=True)
    zc = z - mu
    var = jnp.mean(zc * zc, axis=-1, keepdims=True)
    out = zc * lax.rsqrt(var + LN_EPS) * g_ref[...] + b_ref[...]
    o_ref[...] = out
    ob_ref[...] = out.astype(BF16)


def add_layer_norm(x, y, g, b, alpha, tm_pref=256):
    m, d = x.shape
    tm = _tile(m, tm_pref, 16)
    row = pl.BlockSpec((tm, d), lambda i: (i, 0))
    vec = pl.BlockSpec((1, d), lambda i: (0, 0))
    return pl.pallas_call(
        functools.partial(_add_ln_kernel, alpha=alpha),
        grid=(m // tm,),
        in_specs=[row, row, vec, vec],
        out_specs=[row, row],
        out_shape=[jax.ShapeDtypeStruct((m, d), F32), jax.ShapeDtypeStruct((m, d), BF16)],
        compiler_params=_params("parallel"),
        name="add_layer_norm",
    )(x, y, g.reshape(1, d), b.reshape(1, d))


def _lower_bounds_kernel(z_ref, o_ref):
    z = z_ref[...]
    depth = z.shape[0]
    e = jnp.exp(z - jnp.max(z, axis=0, keepdims=True))
    p = e / jnp.sum(e, axis=0, keepdims=True)
    run = jnp.zeros_like(p[0:1])
    for l in range(depth):
        run = run + p[l:l + 1]
        o_ref[l:l + 1, :] = run - p[0:1]


def lower_bounds(logits):
    return pl.pallas_call(
        _lower_bounds_kernel,
        out_shape=jax.ShapeDtypeStruct(logits.shape, F32),
        name="hgrn_lower_bounds",
    )(logits.astype(F32))


def _gla_chunk(q, k, v, g, st_ref):
    dk = q.shape[1]
    nsub = CHUNK // SUB
    row = lax.broadcasted_iota(jnp.int32, (CHUNK, CHUNK), 0)
    col = lax.broadcasted_iota(jnp.int32, (CHUNK, CHUNK), 1)
    tri = (col <= row).astype(F32)
    b = jnp.dot(tri, g, precision=HIGHEST, preferred_element_type=F32)
    b_last = b[CHUNK - 1:CHUNK, :]
    st = st_ref[...]

    qe = q * jnp.exp(b)
    o = _dot_nt(qe.astype(BF16), st.astype(BF16))

    levels = [jnp.zeros((1, dk), F32)] + [b[SUB * i - 1:SUB * i, :] for i in range(1, nsub)]
    lvl_rows = jnp.concatenate([jnp.broadcast_to(r, (SUB, dk)) for r in levels], axis=0)
    qt = q * jnp.exp(b - lvl_rows)
    row_blk = lax.broadcasted_iota(jnp.int32, (CHUNK, dk), 0) // SUB
    qhat = jnp.concatenate([jnp.where(row_blk == i, qt, 0.0) for i in range(1, nsub)], axis=1)
    khat = jnp.concatenate([k * jnp.exp(jnp.minimum(levels[i] - b, 0.0)) for i in range(1, nsub)], axis=1)
    p = _dot_nt(qhat.astype(BF16), khat.astype(BF16))
    p = jnp.where(col < (row // SUB) * SUB, p, 0.0)
    o += jnp.dot(p.astype(BF16), v.astype(BF16), preferred_element_type=F32)

    b4 = b.reshape(nsub, SUB, dk)
    q4 = q.reshape(nsub, SUB, dk)
    k4 = k.reshape(nsub, SUB, dk)
    ti = lax.broadcasted_iota(jnp.int32, (1, SUB, SUB, 1), 1)
    si = lax.broadcasted_iota(jnp.int32, (1, SUB, SUB, 1), 2)
    diff = b4[:, :, None, :] - b4[:, None, :, :]
    dec = jnp.exp(jnp.where(si <= ti, diff, NEG_BIG))
    dg = jnp.sum(q4[:, :, None, :] * k4[:, None, :, :] * dec, axis=-1)
    v4 = v.reshape(nsub, SUB, v.shape[1])
    od = jnp.einsum("its,isv->itv", dg.astype(BF16), v4.astype(BF16), preferred_element_type=F32)
    o += od.reshape(CHUNK, v.shape[1])

    kd = k * jnp.exp(b_last - b)
    st_ref[...] = st * jnp.exp(b_last) + _dot_tn(v.astype(BF16), kd.astype(BF16))
    return o


def _hgrn_kernel(q_ref, f_ref, i_ref, gate_ref, lb_ref, nw_ref, o_ref, st_ref, *, chunks):
    @pl.when(pl.program_id(2) == 0)
    def _():
        st_ref[...] = jnp.zeros_like(st_ref)

    lb = lb_ref[...]
    nw = nw_ref[...]

    def body(c, carry):
        rows = pl.ds(pl.multiple_of(c * CHUNK, CHUNK), CHUNK)
        qa = q_ref[0, rows, :]
        f = lb + (1.0 - lb) * _sigmoid(f_ref[0, rows, :])
        o = _gla_chunk(qa * _sigmoid(qa), 1.0 - f, i_ref[0, rows, :], jnp.log(f), st_ref)
        o = o * lax.rsqrt(jnp.mean(o * o, axis=-1, keepdims=True) + RMS_EPS) * nw
        o_ref[0, rows, :] = (o * _sigmoid(gate_ref[0, rows, :])).astype(o_ref.dtype)
        return carry

    lax.fori_loop(0, chunks, body, 0)


def hgrn2_mixer(proj, lb, norm_w, heads, t_pref=512):
    bsz, s, _ = proj.shape
    d = HEAD_DIM
    t = _tile(s, t_pref, CHUNK)

    def col(group):
        return pl.BlockSpec((1, t, d), lambda b, h, i, group=group: (b, i, group * heads + h))

    return pl.pallas_call(
        functools.partial(_hgrn_kernel, chunks=t // CHUNK),
        grid=(bsz, heads, s // t),
        in_specs=[col(0), col(1), col(2), col(3),
                  pl.BlockSpec((1, d), lambda b, h, i: (0, h)),
                  pl.BlockSpec((1, d), lambda b, h, i: (0, 0))],
        out_specs=pl.BlockSpec((1, t, d), lambda b, h, i: (b, i, h)),
        out_shape=jax.ShapeDtypeStruct((bsz, s, heads * d), BF16),
        scratch_shapes=[pltpu.VMEM((d, d), F32)],
        compiler_params=_params("parallel", "parallel", "arbitrary"),
        name="hgrn2_mixer",
    )(proj, proj, proj, proj, lb.reshape(1, heads * d), norm_w.reshape(1, d))


def _gla_kernel(q_ref, k_ref, v_ref, gate_ref, a_ref, w2_ref, gb_ref, nw_ref, o_ref, st_ref, *, chunks, scale):
    @pl.when(pl.program_id(2) == 0)
    def _():
        st_ref[...] = jnp.zeros_like(st_ref)

    w2 = w2_ref[...]
    gb = gb_ref[...]
    nw = nw_ref[...]

    def body(c, carry):
        rows = pl.ds(pl.multiple_of(c * CHUNK, CHUNK), CHUNK)
        logit = jnp.dot(a_ref[0, rows, :], w2, precision=HIGHEST, preferred_element_type=F32) + gb
        g = _log_sigmoid(logit) * (1.0 / GLA_TAU)
        o = _gla_chunk(q_ref[0, rows, :] * scale, k_ref[0, rows, :], v_ref[0, rows, :], g, st_ref)
        o = o * lax.rsqrt(jnp.mean(o * o, axis=-1, keepdims=True) + RMS_EPS) * nw
        gate = gate_ref[0, rows, :]
        o_ref[0, rows, :] = (o * gate * _sigmoid(gate)).astype(o_ref.dtype)
        return carry

    lax.fori_loop(0, chunks, body, 0)


def gla_mixer(proj, a, w2, gb, norm_w, heads, dk, dv, t_pref=512):
    bsz, s, _ = proj.shape
    rank = a.shape[-1]
    t = _tile(s, t_pref, CHUNK)
    kv = dv // dk
    assert dv % dk == 0
    return pl.pallas_call(
        functools.partial(_gla_kernel, chunks=t // CHUNK, scale=dk ** -0.5),
        grid=(bsz, heads, s // t),
        in_specs=[pl.BlockSpec((1, t, dk), lambda b, h, i: (b, i, h)),
                  pl.BlockSpec((1, t, dk), lambda b, h, i: (b, i, heads + h)),
                  pl.BlockSpec((1, t, dv), lambda b, h, i: (b, i, 2 * heads // kv + h)),
                  pl.BlockSpec((1, t, dv), lambda b, h, i: (b, i, 2 * heads // kv + heads + h)),
                  pl.BlockSpec((1, t, rank), lambda b, h, i: (b, i, 0)),
                  pl.BlockSpec((rank, dk), lambda b, h, i: (0, h)),
                  pl.BlockSpec((1, dk), lambda b, h, i: (0, h)),
                  pl.BlockSpec((1, dv), lambda b, h, i: (0, 0))],
        out_specs=pl.BlockSpec((1, t, dv), lambda b, h, i: (b, i, h)),
        out_shape=jax.ShapeDtypeStruct((bsz, s, heads * dv), BF16),
        scratch_shapes=[pltpu.VMEM((dv, dk), F32)],
        compiler_params=_params("parallel", "parallel", "arbitrary"),
        name="gla_mixer",
    )(proj, proj, proj, proj, a, w2, gb.reshape(1, heads * dk), norm_w.reshape(1, dv))


def _t5_bucket_table():
    dist = WINDOW + np.arange(WINDOW)[:, None] - np.arange(2 * WINDOW)[None, :]
    max_exact = N_BUCKETS // 2
    d = np.maximum(dist, 0)
    large = max_exact + (np.log(np.maximum(d, 1).astype(np.float32) / max_exact)
                         / math.log(MAX_DISTANCE / max_exact) * (N_BUCKETS - max_exact)).astype(np.int32)
    large = np.minimum(large, N_BUCKETS - 1)
    return np.where(d < max_exact, d, large).astype(np.int32)


def _band_bias_kernel(rel_ref, bucket_ref, o_ref):
    h = pl.program_id(0)
    bucket = bucket_ref[...]
    acc = jnp.zeros(bucket.shape, F32)
    for n in range(N_BUCKETS):
        acc = jnp.where(bucket == n, rel_ref[n, h], acc)
    o_ref[0] = acc


def band_bias(rel_bias):
    heads = rel_bias.shape[1]
    bucket = jnp.asarray(_t5_bucket_table())
    return pl.pallas_call(
        _band_bias_kernel,
        grid=(heads,),
        in_specs=[pl.BlockSpec(memory_space=pltpu.SMEM),
                  pl.BlockSpec((WINDOW, 2 * WINDOW), lambda h: (0, 0))],
        out_specs=pl.BlockSpec((1, WINDOW, 2 * WINDOW), lambda h: (h, 0, 0)),
        out_shape=jax.ShapeDtypeStruct((heads, WINDOW, 2 * WINDOW), F32),
        name="t5_band_bias",
    )(rel_bias.astype(F32), bucket)


def _swa_kernel(sink_ref, q_ref, kp_ref, kc_ref, vp_ref, vc_ref, bias_ref, o_ref, *, group):
    n = pl.program_id(1)
    hkv = pl.program_id(2)
    w = WINDOW
    d = HEAD_DIM
    scale = d ** -0.5
    kb = jnp.concatenate([kp_ref[0], kc_ref[0]], axis=0).astype(BF16)
    vb = jnp.concatenate([vp_ref[0], vc_ref[0]], axis=0).astype(BF16)
    qi = lax.broadcasted_iota(jnp.int32, (w, 2 * w), 0)
    ki = lax.broadcasted_iota(jnp.int32, (w, 2 * w), 1)
    dist = w + qi - ki
    mask = (dist >= 0) & (dist < w) & ((n * w - w + ki) >= 0)
    for gi in range(group):
        q = q_ref[0, :, gi * d:(gi + 1) * d].astype(BF16)
        s = _dot_nt(q, kb) * scale + bias_ref[gi]
        s = jnp.where(mask, s, NEG_BIG)
        sink = sink_ref[hkv * group + gi]
        m = jnp.maximum(jnp.max(s, axis=-1, keepdims=True), sink)
        p = jnp.exp(s - m)
        p = p / (jnp.sum(p, axis=-1, keepdims=True) + jnp.exp(sink - m))
        o = jnp.dot(p.astype(BF16), vb, preferred_element_type=F32)
        o_ref[0, :, gi * d:(gi + 1) * d] = o.astype(o_ref.dtype)


def swa_mixer(proj, col0, heads, kv_heads, sinks, bias):
    bsz, s, _ = proj.shape
    w = WINDOW
    d = HEAD_DIM
    group = heads // kv_heads
    nb = s // w
    q0 = col0 // (group * d)
    k0 = (col0 + heads * d) // d
    v0 = k0 + kv_heads
    assert col0 % (group * d) == 0

    def prev(c0):
        return pl.BlockSpec((1, w, d), lambda b, n, h: (b, jnp.maximum(n - 1, 0), c0 + h))

    def cur(c0):
        return pl.BlockSpec((1, w, d), lambda b, n, h: (b, n, c0 + h))

    return pl.pallas_call(
        functools.partial(_swa_kernel, group=group),
        grid=(bsz, nb, kv_heads),
        in_specs=[pl.BlockSpec(memory_space=pltpu.SMEM),
                  pl.BlockSpec((1, w, group * d), lambda b, n, h: (b, n, q0 + h)),
                  prev(k0), cur(k0), prev(v0), cur(v0),
                  pl.BlockSpec((group, w, 2 * w), lambda b, n, h: (h, 0, 0))],
        out_specs=pl.BlockSpec((1, w, group * d), lambda b, n, h: (b, n, h)),
        out_shape=jax.ShapeDtypeStruct((bsz, s, heads * d), BF16),
        compiler_params=_params("parallel", "parallel", "parallel"),
        name="swa_mixer",
    )(sinks.astype(F32), proj, proj, proj, proj, proj, bias)


def _fox_cumsum_kernel(f_ref, o_ref, *, per_head):
    ls = _log_sigmoid(f_ref[0])
    r = ls.shape[0]
    li = lax.broadcasted_iota(jnp.int32, (128, 128), 0)
    lj = lax.broadcasted_iota(jnp.int32, (128, 128), 1)
    within = jnp.dot(ls, (li <= lj).astype(F32), precision=HIGHEST, preferred_element_type=F32)
    total = jnp.dot(ls, jnp.ones((128, 128), F32), precision=HIGHEST, preferred_element_type=F32)
    ri = lax.broadcasted_iota(jnp.int32, (r, r), 0)
    rj = lax.broadcasted_iota(jnp.int32, (r, r), 1)
    before = ((rj < ri) & (rj // per_head == ri // per_head)).astype(F32)
    o_ref[0] = within + jnp.dot(before, total, precision=HIGHEST, preferred_element_type=F32)


def fox_log_decay(f_logit):
    bsz, s, h = f_logit.shape
    per_head = s // 128
    f = jnp.transpose(f_logit, (0, 2, 1)).reshape(bsz, h * per_head, 128)
    c = pl.pallas_call(
        functools.partial(_fox_cumsum_kernel, per_head=per_head),
        grid=(bsz,),
        in_specs=[pl.BlockSpec((1, h * per_head, 128), lambda b: (b, 0, 0))],
        out_specs=pl.BlockSpec((1, h * per_head, 128), lambda b: (b, 0, 0)),
        out_shape=jax.ShapeDtypeStruct((bsz, h * per_head, 128), F32),
        compiler_params=_params("parallel"),
        name="fox_log_decay",
    )(f)
    return c.reshape(bsz, h, s)


def _fox_kernel(q_ref, k_ref, v_ref, cq_ref, ck_ref, o_ref, m_ref, l_ref, acc_ref, *, tq, tk):
    i = pl.program_id(2)
    j = pl.program_id(3)
    scale = HEAD_DIM ** -0.5

    @pl.when(j == 0)
    def _():
        m_ref[...] = jnp.full_like(m_ref, NEG_BIG)
        l_ref[...] = jnp.zeros_like(l_ref)
        acc_ref[...] = jnp.zeros_like(acc_ref)

    def step(masked):
        s = _dot_nt(q_ref[0].astype(BF16), k_ref[0].astype(BF16)) * scale
        s = s + cq_ref[0, 0] - ck_ref[0, 0]
        if masked:
            qpos = i * tq + lax.broadcasted_iota(jnp.int32, (tq, tk), 0)
            kpos = j * tk + lax.broadcasted_iota(jnp.int32, (tq, tk), 1)
            s = jnp.where(kpos <= qpos, s, NEG_BIG)
        m_prev = m_ref[...]
        m_new = jnp.maximum(m_prev, jnp.max(s, axis=-1, keepdims=True))
        a = jnp.exp(m_prev - m_new)
        p = jnp.exp(s - m_new)
        l_ref[...] = a * l_ref[...] + jnp.sum(p, axis=-1, keepdims=True)
        acc_ref[...] = a * acc_ref[...] + jnp.dot(p.astype(BF16), v_ref[0].astype(BF16),
                                                  preferred_element_type=F32)
        m_ref[...] = m_new

    @pl.when(j < i)
    def _():
        step(False)

    @pl.when(j == i)
    def _():
        step(True)
        o_ref[0] = (acc_ref[...] / l_ref[...]).astype(o_ref.dtype)


def fox_mixer(proj, c, heads, t_pref=512):
    bsz, s, _ = proj.shape
    d = HEAD_DIM
    t = _tile(s, t_pref, 128)
    nb = s // t
    c_col = c.reshape(bsz, heads, s, 1)
    c_row = c.reshape(bsz, heads, 1, s)
    return pl.pallas_call(
        functools.partial(_fox_kernel, tq=t, tk=t),
        grid=(bsz, heads, nb, nb),
        in_specs=[pl.BlockSpec((1, t, d), lambda b, h, i, j: (b, i, h)),
                  pl.BlockSpec((1, t, d), lambda b, h, i, j: (b, jnp.minimum(j, i), heads + h)),
                  pl.BlockSpec((1, t, d), lambda b, h, i, j: (b, jnp.minimum(j, i), 2 * heads + h)),
                  pl.BlockSpec((1, 1, t, 1), lambda b, h, i, j: (b, h, i, 0)),
                  pl.BlockSpec((1, 1, 1, t), lambda b, h, i, j: (b, h, 0, jnp.minimum(j, i)))],
        out_specs=pl.BlockSpec((1, t, d), lambda b, h, i, j: (b, i, h)),
        out_shape=jax.ShapeDtypeStruct((bsz, s, heads * d), BF16),
        scratch_shapes=[pltpu.VMEM((t, 1), F32), pltpu.VMEM((t, 1), F32), pltpu.VMEM((t, d), F32)],
        compiler_params=_params("parallel", "parallel", "parallel", "arbitrary"),
        name="fox_mixer",
    )(proj, proj, proj, c_col, c_row)


def _router_kernel(x_ref, wt_ref, rb_ref, o_ref):
    e = wt_ref.shape[0]
    tm = x_ref.shape[0]
    per = e // N_GROUPS
    logits = _dot_nt(wt_ref[...], x_ref[...], precision=HIGHEST)
    scores = _sigmoid(logits)
    biased = scores + rb_ref[...]
    g3 = biased.reshape(N_GROUPS, per, tm)
    mi = lax.broadcasted_iota(jnp.int32, (N_GROUPS, per, tm), 1)
    m1 = jnp.max(g3, axis=1, keepdims=True)
    i1 = jnp.min(jnp.where(g3 == m1, mi, per), axis=1, keepdims=True)
    m2 = jnp.max(jnp.where(mi == i1, -jnp.inf, g3), axis=1, keepdims=True)
    cur = (m1 + m2).reshape(N_GROUPS, tm)
    gi = lax.broadcasted_iota(jnp.int32, (N_GROUPS, tm), 0)
    sel = jnp.zeros((N_GROUPS, tm), jnp.bool_)
    for _ in range(TOPK_GROUPS):
        mx = jnp.max(cur, axis=0, keepdims=True)
        pick = gi == jnp.min(jnp.where(cur == mx, gi, N_GROUPS), axis=0, keepdims=True)
        sel = sel | pick
        cur = jnp.where(pick, -jnp.inf, cur)
    ok = jnp.broadcast_to(sel.reshape(N_GROUPS, 1, tm), (N_GROUPS, per, tm)).reshape(e, tm)
    cur = jnp.where(ok, biased, NEG_BIG)
    ei = lax.broadcasted_iota(jnp.int32, (e, tm), 0)
    chosen = jnp.zeros((e, tm), jnp.bool_)
    for _ in range(TOP_K):
        mx = jnp.max(cur, axis=0, keepdims=True)
        pick = ei == jnp.min(jnp.where(cur == mx, ei, e), axis=0, keepdims=True)
        chosen = chosen | pick
        cur = jnp.where(pick, -jnp.inf, cur)
    w = jnp.where(chosen, scores, 0.0)
    o_ref[...] = w / jnp.sum(w, axis=0, keepdims=True) * ROUTED_SCALE


def router(x, w_router, router_bias, tm_pref=512):
    m, d = x.shape
    e = w_router.shape[1]
    tm = _tile(m, tm_pref, 128)
    return pl.pallas_call(
        _router_kernel,
        grid=(m // tm,),
        in_specs=[pl.BlockSpec((tm, d), lambda i: (i, 0)),
                  pl.BlockSpec((e, d), lambda i: (0, 0)),
                  pl.BlockSpec((e, 1), lambda i: (0, 0))],
        out_specs=pl.BlockSpec((e, tm), lambda i: (0, i)),
        out_shape=jax.ShapeDtypeStruct((e, m), F32),
        compiler_params=_params("parallel"),
        name="moe_router",
    )(x, jnp.transpose(w_router).astype(F32), router_bias.reshape(e, 1).astype(F32))


def _moe_kernel(x_ref, comb_ref, wg_ref, wu_ref, wd_ref, sg_ref, su_ref, sd_ref, o_ref):
    e = pl.program_id(1)
    x = x_ref[...]

    @pl.when(e == 0)
    def _():
        hs = jnp.dot(x, sg_ref[...], preferred_element_type=F32)
        hs = hs * _sigmoid(hs) * jnp.dot(x, su_ref[...], preferred_element_type=F32)
        o_ref[...] = jnp.dot(hs.astype(BF16), sd_ref[...], preferred_element_type=F32)

    comb = comb_ref[...]
    lane = lax.broadcasted_iota(jnp.int32, comb.shape, 1)
    c = jnp.sum(jnp.where(lane == e, comb, 0.0), axis=1, keepdims=True)
    h = jnp.dot(x, wg_ref[0], preferred_element_type=F32)
    h = h * _sigmoid(h) * jnp.dot(x, wu_ref[0], preferred_element_type=F32) * c
    o_ref[...] += jnp.dot(h.astype(BF16), wd_ref[0], preferred_element_type=F32)


def moe_ffn(xb, comb, wg, wu, wd, sg, su, sd, tm_pref=512):
    m, d = xb.shape
    e, _, f = wg.shape
    fs = sg.shape[1]
    tm = _tile(m, tm_pref, 16)
    return pl.pallas_call(
        _moe_kernel,
        grid=(m // tm, e),
        in_specs=[pl.BlockSpec((tm, d), lambda i, j: (i, 0)),
                  pl.BlockSpec((tm, e), lambda i, j: (i, 0)),
                  pl.BlockSpec((1, d, f), lambda i, j: (j, 0, 0)),
                  pl.BlockSpec((1, d, f), lambda i, j: (j, 0, 0)),
                  pl.BlockSpec((1, f, d), lambda i, j: (j, 0, 0)),
                  pl.BlockSpec((d, fs), lambda i, j: (0, 0)),
                  pl.BlockSpec((d, fs), lambda i, j: (0, 0)),
                  pl.BlockSpec((fs, d), lambda i, j: (0, 0))],
        out_specs=pl.BlockSpec((tm, d), lambda i, j: (i, 0)),
        out_shape=jax.ShapeDtypeStruct((m, d), F32),
        compiler_params=_params("parallel", "arbitrary"),
        name="moe_ffn",
    )(xb, comb, wg, wu, wd, sg, su, sd)


def kernel(x, w_in_ab, b_in_ab, w_in_cd, b_in_cd, hgrn_lb_logits, hgrn_norm, sinks, rel_bias, gla_w2, gla_b,
           gla_norm, w_out, ln_g, ln_b, w_router, router_bias, w_exp_gate, w_exp_up, w_exp_down, w_sh_gate,
           w_sh_up, w_sh_down):
    bsz, s, d = x.shape
    m = bsz * s
    depth = w_out.shape[0]
    alpha = (2.0 * depth) ** 0.25
    half = d // 2
    a_heads = half // HEAD_DIM
    b_heads = half // HEAD_DIM
    b_kv_heads = b_heads // 4
    c_dk = (d // 4) // C_HEADS
    c_dv = half // C_HEADS
    d_heads = half // HEAD_DIM
    rank = gla_w2.shape[1]
    c_cols = 2 * C_HEADS * c_dk + 2 * C_HEADS * c_dv
    d_cols = 3 * d_heads * HEAD_DIM

    lbs = lower_bounds(hgrn_lb_logits)
    bias = band_bias(rel_bias)

    xf = x.reshape(m, d).astype(F32)
    xb = xf.astype(BF16)
    for l in range(depth):
        j = l // 2
        w_o = w_out[l].astype(BF16)
        if l % 2 == 0:
            proj = matmul_bias(xb, w_in_ab[j].astype(BF16), b_in_ab[j], F32).reshape(bsz, s, -1)
            o1 = hgrn2_mixer(proj, lbs[l], hgrn_norm[j], a_heads)
            o2 = swa_mixer(proj, 4 * a_heads * HEAD_DIM, b_heads, b_kv_heads, sinks[j], bias)
        else:
            w_in = w_in_cd[j]
            b_in = b_in_cd[j]
            d0 = c_cols + rank
            f0 = d0 + d_cols
            proj_c = matmul_bias(xb, w_in[:, :c_cols].astype(BF16), b_in[:c_cols], F32).reshape(bsz, s, -1)
            proj_d = matmul_bias(xb, w_in[:, d0:f0].astype(BF16), b_in[d0:f0], F32).reshape(bsz, s, -1)
            w_small = jnp.concatenate([w_in[:, c_cols:d0], w_in[:, f0:]], axis=1).astype(BF16)
            b_small = jnp.concatenate([b_in[c_cols:d0], b_in[f0:]])
            small = matmul_bias(xb, w_small, b_small, F32).reshape(bsz, s, -1)
            o1 = gla_mixer(proj_c, small[..., :rank], gla_w2[j].astype(F32), gla_b[j].astype(F32), gla_norm[j],
                           C_HEADS, c_dk, c_dv)
            o2 = fox_mixer(proj_d, fox_log_decay(small[..., rank:]), d_heads)
        mix = matmul_pair(o1.reshape(m, half), o2.reshape(m, half), w_o[:half], w_o[half:], F32)
        xf, xb = add_layer_norm(xf, mix, ln_g[l, 0], ln_b[l, 0], alpha)
        comb_t = router(xf, w_router[l], router_bias[l])
        ffn = moe_ffn(xb, jnp.transpose(comb_t), w_exp_gate[l].astype(BF16), w_exp_up[l].astype(BF16),
                      w_exp_down[l].astype(BF16), w_sh_gate[l].astype(BF16), w_sh_up[l].astype(BF16),
                      w_sh_down[l].astype(BF16))
        xf, xb = add_layer_norm(xf, ffn, ln_g[l, 1], ln_b[l, 1], alpha)
    return xf.reshape(bsz, s, d).astype(x.dtype)
```

```python
import functools
import math

import numpy as np
import jax
import jax.numpy as jnp
from jax import lax
from jax.experimental import pallas as pl
from jax.experimental.pallas import tpu as pltpu

F32 = jnp.float32
BF16 = jnp.bfloat16
HIGHEST = lax.Precision.HIGHEST

HEAD_DIM = 128
WINDOW = 128
C_HEADS = 4
GLA_TAU = 16.0
CHUNK = 64
SUB = 16
N_BUCKETS = 32
MAX_DISTANCE = 128
TOP_K = 8
N_GROUPS = 8
TOPK_GROUPS = 4
ROUTED_SCALE = 2.5
LN_EPS = 1e-5
RMS_EPS = 1e-6
NEG_BIG = -1e30

VMEM_LIMIT_BYTES = 52 * 1024 * 1024


def _params(*sem):
    return pltpu.CompilerParams(dimension_semantics=sem, vmem_limit_bytes=VMEM_LIMIT_BYTES)


def _tile(n, pref, align):
    t = min(pref, n)
    t -= t % align
    while t >= align:
        if n % t == 0:
            return t
        t -= align
    return n


def _sigmoid(x):
    return 1.0 / (1.0 + jnp.exp(-x))


def _log_sigmoid(x):
    return jnp.minimum(x, 0.0) - jnp.log(1.0 + jnp.exp(-jnp.abs(x)))


def _dot_nt(a, b, **kw):
    return lax.dot_general(a, b, (((1,), (1,)), ((), ())), preferred_element_type=F32, **kw)


def _dot_tn(a, b, **kw):
    return lax.dot_general(a, b, (((0,), (0,)), ((), ())), preferred_element_type=F32, **kw)


def _mm_kernel(x_ref, w_ref, b_ref, o_ref):
    acc = jnp.dot(x_ref[...], w_ref[...], preferred_element_type=F32)
    o_ref[...] = (acc + b_ref[...]).astype(o_ref.dtype)


def matmul_bias(x, w, b, out_dtype, tm_pref=512, tn_pref=1024):
    m, k = x.shape
    n = w.shape[1]
    tm = _tile(m, tm_pref, 8)
    tn = _tile(n, tn_pref, 128)
    return pl.pallas_call(
        _mm_kernel,
        grid=(n // tn, m // tm),
        in_specs=[pl.BlockSpec((tm, k), lambda j, i: (i, 0)),
                  pl.BlockSpec((k, tn), lambda j, i: (0, j)),
                  pl.BlockSpec((1, tn), lambda j, i: (0, j))],
        out_specs=pl.BlockSpec((tm, tn), lambda j, i: (i, j)),
        out_shape=jax.ShapeDtypeStruct((m, n), out_dtype),
        compiler_params=_params("parallel", "parallel"),
        name="matmul_bias",
    )(x, w, b.reshape(1, n).astype(F32))


def _mm2_kernel(a1_ref, a2_ref, w1_ref, w2_ref, o_ref):
    acc = jnp.dot(a1_ref[...], w1_ref[...], preferred_element_type=F32)
    acc += jnp.dot(a2_ref[...], w2_ref[...], preferred_element_type=F32)
    o_ref[...] = acc.astype(o_ref.dtype)


def matmul_pair(a1, a2, w1, w2, out_dtype, tm_pref=512, tn_pref=1024):
    m, k1 = a1.shape
    k2 = a2.shape[1]
    n = w1.shape[1]
    tm = _tile(m, tm_pref, 8)
    tn = _tile(n, tn_pref, 128)
    return pl.pallas_call(
        _mm2_kernel,
        grid=(n // tn, m // tm),
        in_specs=[pl.BlockSpec((tm, k1), lambda j, i: (i, 0)),
                  pl.BlockSpec((tm, k2), lambda j, i: (i, 0)),
                  pl.BlockSpec((k1, tn), lambda j, i: (0, j)),
                  pl.BlockSpec((k2, tn), lambda j, i: (0, j))],
        out_specs=pl.BlockSpec((tm, tn), lambda j, i: (i, j)),
        out_shape=jax.ShapeDtypeStruct((m, n), out_dtype),
        compiler_params=_params("parallel", "parallel"),
        name="matmul_pair",
    )(a1, a2, w1, w2)


def _pack_halves(v):
    h = v.shape[1] // 2
    r = v.astype(BF16).astype(F32)
    lo = lax.bitcast_convert_type(r[:, :h], jnp.uint32) >> 16
    hi = lax.bitcast_convert_type(r[:, h:], jnp.uint32) & jnp.uint32(0xFFFF0000)
    return lo | hi


def _unpack_halves(w):
    left = lax.bitcast_convert_type(w << 16, F32)
    right = lax.bitcast_convert_type(w & jnp.uint32(0xFFFF0000), F32)
    return left, right


def _layer_norm(z, g, b):
    mu = jnp.mean(z, axis=-1, keepdims=True)
    zc = z - mu
    var = jnp.mean(zc * zc, axis=-1, keepdims=True)
    return zc * lax.rsqrt(var + LN_EPS) * g + b


def _add_ln_kernel(x_ref, y_ref, g_ref, b_ref, o_ref, ob_ref, ow_ref, *, alpha):
    out = _layer_norm(alpha * x_ref[...] + y_ref[...], g_ref[...], b_ref[...])
    o_ref[...] = out
    ob_ref[...] = out.astype(BF16)
    ow_ref[...] = _pack_halves(out)


def add_layer_norm(x, y, g, b, alpha, tm_pref=256):
    m, d = x.shape
    tm = _tile(m, tm_pref, 16)
    row = pl.BlockSpec((tm, d), lambda i: (i, 0))
    half = pl.BlockSpec((tm, d // 2), lambda i: (i, 0))
    vec = pl.BlockSpec((1, d), lambda i: (0, 0))
    return pl.pallas_call(
        functools.partial(_add_ln_kernel, alpha=alpha),
        grid=(m // tm,),
        in_specs=[row, row, vec, vec],
        out_specs=[row, row, half],
        out_shape=[jax.ShapeDtypeStruct((m, d), F32), jax.ShapeDtypeStruct((m, d), BF16),
                   jax.ShapeDtypeStruct((m, d // 2), jnp.uint32)],
        compiler_params=_params("parallel"),
        name="add_layer_norm",
    )(x, y, g.reshape(1, d), b.reshape(1, d))


def _lower_bounds_kernel(z_ref, o_ref):
    z = z_ref[...]
    depth = z.shape[0]
    e = jnp.exp(z - jnp.max(z, axis=0, keepdims=True))
    p = e / jnp.sum(e, axis=0, keepdims=True)
    run = jnp.zeros_like(p[0:1])
    for l in range(depth):
        run = run + p[l:l + 1]
        o_ref[l:l + 1, :] = run - p[0:1]


def lower_bounds(logits):
    return pl.pallas_call(
        _lower_bounds_kernel,
        out_shape=jax.ShapeDtypeStruct(logits.shape, F32),
        name="hgrn_lower_bounds",
    )(logits.astype(F32))


def _gla_chunk(q, k, v, g, st_ref):
    dk = q.shape[1]
    nsub = CHUNK // SUB
    row = lax.broadcasted_iota(jnp.int32, (CHUNK, CHUNK), 0)
    col = lax.broadcasted_iota(jnp.int32, (CHUNK, CHUNK), 1)
    tri = (col <= row).astype(F32)
    b = jnp.dot(tri, g, precision=HIGHEST, preferred_element_type=F32)
    b_last = b[CHUNK - 1:CHUNK, :]
    st = st_ref[...]

    qe = q * jnp.exp(b)
    o = _dot_nt(qe.astype(BF16), st.astype(BF16))

    levels = [jnp.zeros((1, dk), F32)] + [b[SUB * i - 1:SUB * i, :] for i in range(1, nsub)]
    lvl_rows = jnp.concatenate([jnp.broadcast_to(r, (SUB, dk)) for r in levels], axis=0)
    qt = q * jnp.exp(b - lvl_rows)
    row_blk = lax.broadcasted_iota(jnp.int32, (CHUNK, dk), 0) // SUB
    qhat = jnp.concatenate([jnp.where(row_blk == i, qt, 0.0) for i in range(1, nsub)], axis=1)
    khat = jnp.concatenate([k * jnp.exp(jnp.minimum(levels[i] - b, 0.0)) for i in range(1, nsub)], axis=1)
    p = _dot_nt(qhat.astype(BF16), khat.astype(BF16))
    p = jnp.where(col < (row // SUB) * SUB, p, 0.0)
    o += jnp.dot(p.astype(BF16), v.astype(BF16), preferred_element_type=F32)

    b4 = b.reshape(nsub, SUB, dk)
    q4 = q.reshape(nsub, SUB, dk)
    k4 = k.reshape(nsub, SUB, dk)
    ti = lax.broadcasted_iota(jnp.int32, (1, SUB, SUB, 1), 1)
    si = lax.broadcasted_iota(jnp.int32, (1, SUB, SUB, 1), 2)
    diff = b4[:, :, None, :] - b4[:, None, :, :]
    dec = jnp.exp(jnp.where(si <= ti, diff, NEG_BIG))
    dg = jnp.sum(q4[:, :, None, :] * k4[:, None, :, :] * dec, axis=-1)
    v4 = v.reshape(nsub, SUB, v.shape[1])
    od = jnp.einsum("its,isv->itv", dg.astype(BF16), v4.astype(BF16), preferred_element_type=F32)
    o += od.reshape(CHUNK, v.shape[1])

    kd = k * jnp.exp(b_last - b)
    st_ref[...] = st * jnp.exp(b_last) + _dot_tn(v.astype(BF16), kd.astype(BF16))
    return o


def _hgrn_kernel(q_ref, f_ref, i_ref, gate_ref, lb_ref, nw_ref, o_ref, st_ref, *, chunks):
    @pl.when(pl.program_id(2) == 0)
    def _():
        st_ref[...] = jnp.zeros_like(st_ref)

    lb = lb_ref[...]
    nw = nw_ref[...]

    def body(c, carry):
        rows = pl.ds(pl.multiple_of(c * CHUNK, CHUNK), CHUNK)
        qa = q_ref[0, rows, :]
        f = lb + (1.0 - lb) * _sigmoid(f_ref[0, rows, :])
        o = _gla_chunk(qa * _sigmoid(qa), 1.0 - f, i_ref[0, rows, :], jnp.log(f), st_ref)
        o = o * lax.rsqrt(jnp.mean(o * o, axis=-1, keepdims=True) + RMS_EPS) * nw
        o_ref[0, rows, :] = (o * _sigmoid(gate_ref[0, rows, :])).astype(o_ref.dtype)
        return carry

    lax.fori_loop(0, chunks, body, 0, unroll=2)


def hgrn2_mixer(proj, lb, norm_w, heads, t_pref=512):
    bsz, s, _ = proj.shape
    d = HEAD_DIM
    t = _tile(s, t_pref, CHUNK)

    def col(group):
        return pl.BlockSpec((1, t, d), lambda b, h, i, group=group: (b, i, group * heads + h))

    return pl.pallas_call(
        functools.partial(_hgrn_kernel, chunks=t // CHUNK),
        grid=(bsz, heads, s // t),
        in_specs=[col(0), col(1), col(2), col(3),
                  pl.BlockSpec((1, d), lambda b, h, i: (0, h)),
                  pl.BlockSpec((1, d), lambda b, h, i: (0, 0))],
        out_specs=pl.BlockSpec((1, t, d), lambda b, h, i: (b, i, h)),
        out_shape=jax.ShapeDtypeStruct((bsz, s, heads * d), BF16),
        scratch_shapes=[pltpu.VMEM((d, d), F32)],
        compiler_params=_params("parallel", "parallel", "arbitrary"),
        name="hgrn2_mixer",
    )(proj, proj, proj, proj, lb.reshape(1, heads * d), norm_w.reshape(1, d))


def _gla_kernel(q_ref, k_ref, v_ref, gate_ref, a_ref, w2_ref, gb_ref, nw_ref, o_ref, st_ref, *, chunks, scale):
    @pl.when(pl.program_id(2) == 0)
    def _():
        st_ref[...] = jnp.zeros_like(st_ref)

    w2 = w2_ref[...]
    gb = gb_ref[...]
    nw = nw_ref[...]

    def body(c, carry):
        rows = pl.ds(pl.multiple_of(c * CHUNK, CHUNK), CHUNK)
        logit = jnp.dot(a_ref[0, rows, :], w2, precision=HIGHEST, preferred_element_type=F32) + gb
        g = _log_sigmoid(logit) * (1.0 / GLA_TAU)
        o = _gla_chunk(q_ref[0, rows, :] * scale, k_ref[0, rows, :], v_ref[0, rows, :], g, st_ref)
        o = o * lax.rsqrt(jnp.mean(o * o, axis=-1, keepdims=True) + RMS_EPS) * nw
        gate = gate_ref[0, rows, :]
        o_ref[0, rows, :] = (o * gate * _sigmoid(gate)).astype(o_ref.dtype)
        return carry

    lax.fori_loop(0, chunks, body, 0, unroll=2)


def gla_mixer(proj, a, w2, gb, norm_w, heads, dk, dv, t_pref=512):
    bsz, s, _ = proj.shape
    rank = a.shape[-1]
    t = _tile(s, t_pref, CHUNK)
    kv = dv // dk
    assert dv % dk == 0
    return pl.pallas_call(
        functools.partial(_gla_kernel, chunks=t // CHUNK, scale=dk ** -0.5),
        grid=(bsz, heads, s // t),
        in_specs=[pl.BlockSpec((1, t, dk), lambda b, h, i: (b, i, h)),
                  pl.BlockSpec((1, t, dk), lambda b, h, i: (b, i, heads + h)),
                  pl.BlockSpec((1, t, dv), lambda b, h, i: (b, i, 2 * heads // kv + h)),
                  pl.BlockSpec((1, t, dv), lambda b, h, i: (b, i, 2 * heads // kv + heads + h)),
                  pl.BlockSpec((1, t, rank), lambda b, h, i: (b, i, 0)),
                  pl.BlockSpec((rank, dk), lambda b, h, i: (0, h)),
                  pl.BlockSpec((1, dk), lambda b, h, i: (0, h)),
                  pl.BlockSpec((1, dv), lambda b, h, i: (0, 0))],
        out_specs=pl.BlockSpec((1, t, dv), lambda b, h, i: (b, i, h)),
        out_shape=jax.ShapeDtypeStruct((bsz, s, heads * dv), BF16),
        scratch_shapes=[pltpu.VMEM((dv, dk), F32)],
        compiler_params=_params("parallel", "parallel", "arbitrary"),
        name="gla_mixer",
    )(proj, proj, proj, proj, a, w2, gb.reshape(1, heads * dk), norm_w.reshape(1, dv))


def _t5_bucket_table():
    dist = WINDOW + np.arange(WINDOW)[:, None] - np.arange(2 * WINDOW)[None, :]
    max_exact = N_BUCKETS // 2
    d = np.maximum(dist, 0)
    large = max_exact + (np.log(np.maximum(d, 1).astype(np.float32) / max_exact)
                         / math.log(MAX_DISTANCE / max_exact) * (N_BUCKETS - max_exact)).astype(np.int32)
    large = np.minimum(large, N_BUCKETS - 1)
    return np.where(d < max_exact, d, large).astype(np.int32)


def _band_bias_kernel(rel_ref, bucket_ref, o_ref):
    h = pl.program_id(0)
    bucket = bucket_ref[...]
    acc = jnp.zeros(bucket.shape, F32)
    for n in range(N_BUCKETS):
        acc = jnp.where(bucket == n, rel_ref[n, h], acc)
    o_ref[0] = acc


def band_bias(rel_bias):
    heads = rel_bias.shape[1]
    bucket = jnp.asarray(_t5_bucket_table())
    return pl.pallas_call(
        _band_bias_kernel,
        grid=(heads,),
        in_specs=[pl.BlockSpec(memory_space=pltpu.SMEM),
                  pl.BlockSpec((WINDOW, 2 * WINDOW), lambda h: (0, 0))],
        out_specs=pl.BlockSpec((1, WINDOW, 2 * WINDOW), lambda h: (h, 0, 0)),
        out_shape=jax.ShapeDtypeStruct((heads, WINDOW, 2 * WINDOW), F32),
        name="t5_band_bias",
    )(rel_bias.astype(F32), bucket)


def _swa_kernel(sink_ref, q_ref, kp_ref, kc_ref, vp_ref, vc_ref, bias_ref, o_ref, *, group):
    n = pl.program_id(1)
    hkv = pl.program_id(2)
    w = WINDOW
    d = HEAD_DIM
    scale = d ** -0.5
    kb = jnp.concatenate([kp_ref[0], kc_ref[0]], axis=0).astype(BF16)
    vb = jnp.concatenate([vp_ref[0], vc_ref[0]], axis=0).astype(BF16)
    qi = lax.broadcasted_iota(jnp.int32, (w, 2 * w), 0)
    ki = lax.broadcasted_iota(jnp.int32, (w, 2 * w), 1)
    dist = w + qi - ki
    mask = (dist >= 0) & (dist < w) & ((n * w - w + ki) >= 0)
    for gi in range(group):
        q = q_ref[0, :, gi * d:(gi + 1) * d].astype(BF16)
        s = _dot_nt(q, kb) * scale + bias_ref[gi]
        s = jnp.where(mask, s, NEG_BIG)
        sink = sink_ref[hkv * group + gi]
        m = jnp.maximum(jnp.max(s, axis=-1, keepdims=True), sink)
        p = jnp.exp(s - m)
        p = p / (jnp.sum(p, axis=-1, keepdims=True) + jnp.exp(sink - m))
        o = jnp.dot(p.astype(BF16), vb, preferred_element_type=F32)
        o_ref[0, :, gi * d:(gi + 1) * d] = o.astype(o_ref.dtype)


def swa_mixer(proj, col0, heads, kv_heads, sinks, bias):
    bsz, s, _ = proj.shape
    w = WINDOW
    d = HEAD_DIM
    group = heads // kv_heads
    nb = s // w
    q0 = col0 // (group * d)
    k0 = (col0 + heads * d) // d
    v0 = k0 + kv_heads
    assert col0 % (group * d) == 0

    def prev(c0):
        return pl.BlockSpec((1, w, d), lambda b, n, h: (b, jnp.maximum(n - 1, 0), c0 + h))

    def cur(c0):
        return pl.BlockSpec((1, w, d), lambda b, n, h: (b, n, c0 + h))

    return pl.pallas_call(
        functools.partial(_swa_kernel, group=group),
        grid=(bsz, nb, kv_heads),
        in_specs=[pl.BlockSpec(memory_space=pltpu.SMEM),
                  pl.BlockSpec((1, w, group * d), lambda b, n, h: (b, n, q0 + h)),
                  prev(k0), cur(k0), prev(v0), cur(v0),
                  pl.BlockSpec((group, w, 2 * w), lambda b, n, h: (h, 0, 0))],
        out_specs=pl.BlockSpec((1, w, group * d), lambda b, n, h: (b, n, h)),
        out_shape=jax.ShapeDtypeStruct((bsz, s, heads * d), BF16),
        compiler_params=_params("parallel", "parallel", "parallel"),
        name="swa_mixer",
    )(sinks.astype(F32), proj, proj, proj, proj, proj, bias)


def _fox_cumsum_kernel(f_ref, o_ref, *, per_head):
    ls = _log_sigmoid(f_ref[0])
    r = ls.shape[0]
    li = lax.broadcasted_iota(jnp.int32, (128, 128), 0)
    lj = lax.broadcasted_iota(jnp.int32, (128, 128), 1)
    within = jnp.dot(ls, (li <= lj).astype(F32), precision=HIGHEST, preferred_element_type=F32)
    total = jnp.dot(ls, jnp.ones((128, 128), F32), precision=HIGHEST, preferred_element_type=F32)
    ri = lax.broadcasted_iota(jnp.int32, (r, r), 0)
    rj = lax.broadcasted_iota(jnp.int32, (r, r), 1)
    before = ((rj < ri) & (rj // per_head == ri // per_head)).astype(F32)
    o_ref[0] = within + jnp.dot(before, total, precision=HIGHEST, preferred_element_type=F32)


def fox_log_decay(f_logit):
    bsz, s, h = f_logit.shape
    per_head = s // 128
    f = jnp.transpose(f_logit, (0, 2, 1)).reshape(bsz, h * per_head, 128)
    c = pl.pallas_call(
        functools.partial(_fox_cumsum_kernel, per_head=per_head),
        grid=(bsz,),
        in_specs=[pl.BlockSpec((1, h * per_head, 128), lambda b: (b, 0, 0))],
        out_specs=pl.BlockSpec((1, h * per_head, 128), lambda b: (b, 0, 0)),
        out_shape=jax.ShapeDtypeStruct((bsz, h * per_head, 128), F32),
        compiler_params=_params("parallel"),
        name="fox_log_decay",
    )(f)
    return c.reshape(bsz, h, s)


FOX_HEADS_PER_STEP = 2
LOG2E = math.log2(math.e)


def _fox_kernel(q_ref, k_ref, v_ref, ck_ref, o_ref, m_ref, l_ref, acc_ref, *, tq, tk, hp):
    i = pl.program_id(2)
    j = pl.program_id(3)
    d = HEAD_DIM

    @pl.when(j == 0)
    def _():
        m_ref[...] = jnp.full_like(m_ref, NEG_BIG)
        l_ref[...] = jnp.zeros_like(l_ref)
        acc_ref[...] = jnp.zeros_like(acc_ref)

    def step(masked):
        if masked:
            qpos = i * tq + lax.broadcasted_iota(jnp.int32, (tq, tk), 0)
            kpos = j * tk + lax.broadcasted_iota(jnp.int32, (tq, tk), 1)
            keep = kpos <= qpos
        for h in range(hp):
            cols = slice(h * d, (h + 1) * d)
            s = _dot_nt(q_ref[0, :, cols], k_ref[0, :, cols]) - ck_ref[0, h] * LOG2E
            if masked:
                s = jnp.where(keep, s, NEG_BIG)
            m_prev = m_ref[h]
            m_new = jnp.maximum(m_prev, jnp.max(s, axis=-1, keepdims=True))
            a = jnp.exp2(m_prev - m_new)
            p = jnp.exp2(s - m_new)
            l_ref[h] = a * l_ref[h] + jnp.sum(p, axis=-1, keepdims=True)
            acc_ref[h] = a * acc_ref[h] + jnp.dot(p.astype(BF16), v_ref[0, :, cols], preferred_element_type=F32)
            m_ref[h] = m_new

    @pl.when(j < i)
    def _():
        step(False)

    @pl.when(j == i)
    def _():
        step(True)
        for h in range(hp):
            o_ref[0, :, h * d:(h + 1) * d] = (acc_ref[h] / l_ref[h]).astype(o_ref.dtype)


def fox_mixer(proj, c, heads, t_pref=512):
    bsz, s, _ = proj.shape
    d = HEAD_DIM
    hp = FOX_HEADS_PER_STEP
    assert heads % hp == 0
    t = _tile(s, t_pref, 128)
    nb = s // t
    groups = heads // hp
    c_row = c.reshape(bsz, heads, 1, s)
    return pl.pallas_call(
        functools.partial(_fox_kernel, tq=t, tk=t, hp=hp),
        grid=(bsz, groups, nb, nb),
        in_specs=[pl.BlockSpec((1, t, hp * d), lambda b, h, i, j: (b, i, h)),
                  pl.BlockSpec((1, t, hp * d), lambda b, h, i, j: (b, jnp.minimum(j, i), groups + h)),
                  pl.BlockSpec((1, t, hp * d), lambda b, h, i, j: (b, jnp.minimum(j, i), 2 * groups + h)),
                  pl.BlockSpec((1, hp, 1, t), lambda b, h, i, j: (b, h, 0, jnp.minimum(j, i)))],
        out_specs=pl.BlockSpec((1, t, hp * d), lambda b, h, i, j: (b, i, h)),
        out_shape=jax.ShapeDtypeStruct((bsz, s, heads * d), BF16),
        scratch_shapes=[pltpu.VMEM((hp, t, 1), F32), pltpu.VMEM((hp, t, 1), F32), pltpu.VMEM((hp, t, d), F32)],
        compiler_params=_params("parallel", "parallel", "parallel", "arbitrary"),
        name="fox_mixer",
    )(proj, proj, proj, c_row)


def _router_kernel(x_ref, wt_ref, rb_ref, idx_ref, wts_ref, rank_ref, cnt_ref, carry_ref):
    e = wt_ref.shape[0]
    tm = x_ref.shape[0]
    per = e // N_GROUPS

    @pl.when(pl.program_id(0) == 0)
    def _():
        carry_ref[...] = jnp.zeros_like(carry_ref)

    logits = _dot_nt(wt_ref[...], x_ref[...], precision=HIGHEST)
    scores = _sigmoid(logits)
    biased = scores + rb_ref[...]
    g3 = biased.reshape(N_GROUPS, per, tm)
    mi = lax.broadcasted_iota(jnp.int32, (N_GROUPS, per, tm), 1)
    m1 = jnp.max(g3, axis=1, keepdims=True)
    i1 = jnp.min(jnp.where(g3 == m1, mi, per), axis=1, keepdims=True)
    m2 = jnp.max(jnp.where(mi == i1, -jnp.inf, g3), axis=1, keepdims=True)
    cur = (m1 + m2).reshape(N_GROUPS, tm)
    gi = lax.broadcasted_iota(jnp.int32, (N_GROUPS, tm), 0)
    sel = jnp.zeros((N_GROUPS, tm), jnp.bool_)
    for _ in range(TOPK_GROUPS):
        mx = jnp.max(cur, axis=0, keepdims=True)
        pick = gi == jnp.min(jnp.where(cur == mx, gi, N_GROUPS), axis=0, keepdims=True)
        sel = sel | pick
        cur = jnp.where(pick, -jnp.inf, cur)
    ok = jnp.broadcast_to(sel.reshape(N_GROUPS, 1, tm), (N_GROUPS, per, tm)).reshape(e, tm)
    cur = jnp.where(ok, biased, NEG_BIG)
    ei = lax.broadcasted_iota(jnp.int32, (e, tm), 0)
    chosen = jnp.zeros((e, tm), jnp.bool_)
    picks = []
    for k in range(TOP_K):
        mx = jnp.max(cur, axis=0, keepdims=True)
        ek = jnp.min(jnp.where(cur == mx, ei, e), axis=0, keepdims=True)
        pick = ei == ek
        chosen = chosen | pick
        cur = jnp.where(pick, -jnp.inf, cur)
        picks.append((ek, pick, jnp.sum(jnp.where(pick, scores, 0.0), axis=0, keepdims=True)))
    denom = sum(w for _, _, w in picks)
    si = lax.broadcasted_iota(jnp.int32, (tm, tm), 0)
    ti = lax.broadcasted_iota(jnp.int32, (tm, tm), 1)
    before = jnp.dot(chosen.astype(BF16), (si < ti).astype(BF16), preferred_element_type=F32)
    before = before + carry_ref[...]
    for k, (ek, pick, w) in enumerate(picks):
        idx_ref[k:k + 1, :] = ek
        wts_ref[k:k + 1, :] = w / denom * ROUTED_SCALE
        rank_ref[k:k + 1, :] = jnp.sum(jnp.where(pick, before, 0.0), axis=0, keepdims=True).astype(jnp.int32)
    carry_ref[...] += jnp.sum(chosen.astype(F32), axis=1, keepdims=True)
    cnt_ref[...] = carry_ref[...].astype(jnp.int32)


def router(x, w_router, router_bias, tm_pref=512):
    m, d = x.shape
    e = w_router.shape[1]
    tm = _tile(m, tm_pref, 128)
    pick = pl.BlockSpec((TOP_K, tm), lambda i: (0, i))
    return pl.pallas_call(
        _router_kernel,
        grid=(m // tm,),
        in_specs=[pl.BlockSpec((tm, d), lambda i: (i, 0)),
                  pl.BlockSpec((e, d), lambda i: (0, 0)),
                  pl.BlockSpec((e, 1), lambda i: (0, 0))],
        out_specs=[pick, pick, pick, pl.BlockSpec((e, 1), lambda i: (0, 0))],
        out_shape=[jax.ShapeDtypeStruct((TOP_K, m), jnp.int32), jax.ShapeDtypeStruct((TOP_K, m), F32),
                   jax.ShapeDtypeStruct((TOP_K, m), jnp.int32), jax.ShapeDtypeStruct((e, 1), jnp.int32)],
        scratch_shapes=[pltpu.VMEM((e, 1), F32)],
        compiler_params=_params("arbitrary"),
        name="moe_router",
    )(x, jnp.transpose(w_router).astype(F32), router_bias.reshape(e, 1).astype(F32))


DISPATCH_UNROLL = 8


def _dispatch_kernel(dst_ref, x_hbm, xs_hbm, sem, *, tm):
    base = pl.program_id(0) * tm
    for k in range(TOP_K):
        def body(t, carry, k=k):
            d = dst_ref[0, 0, k * tm + t]
            pltpu.make_async_copy(x_hbm.at[pl.ds(base + t, 1)], xs_hbm.at[pl.ds(d, 1)], sem).start()
            return carry
        lax.fori_loop(0, tm, body, 0, unroll=DISPATCH_UNROLL)
    pltpu.make_async_copy(x_hbm.at[pl.ds(0, TOP_K * tm)], xs_hbm.at[pl.ds(0, TOP_K * tm)], sem).wait()


def moe_dispatch(xw, dst_tiles, tm):
    m, h = xw.shape
    return pl.pallas_call(
        functools.partial(_dispatch_kernel, tm=tm),
        grid=(m // tm,),
        in_specs=[pl.BlockSpec((1, 1, TOP_K * tm), lambda i: (i, 0, 0), memory_space=pltpu.SMEM),
                  pl.BlockSpec(memory_space=pl.ANY)],
        out_specs=pl.BlockSpec(memory_space=pl.ANY),
        out_shape=jax.ShapeDtypeStruct((TOP_K * m, h), jnp.uint32),
        scratch_shapes=[pltpu.SemaphoreType.DMA(())],
        compiler_params=_params("arbitrary"),
        name="moe_dispatch",
    )(dst_tiles, xw)


def _group_metadata(counts, n_rows, t):
    e = counts.shape[0]
    n_tiles = n_rows // t
    n_items = n_tiles + e - 1
    ends = jnp.cumsum(counts)
    offs = ends - counts
    first_tile = offs // t
    tiles_of = jnp.where(counts > 0, (ends - 1) // t - first_tile + 1, 0)
    item_end = jnp.cumsum(tiles_of)
    used = item_end[-1]
    ids = jnp.minimum(jnp.arange(n_items, dtype=jnp.int32), used - 1)
    exp = jnp.minimum(jnp.searchsorted(item_end, ids, side="right"), e - 1).astype(jnp.int32)
    tile = (first_tile[exp] + ids - (item_end[exp] - tiles_of[exp])).astype(jnp.int32)
    lo = jnp.clip(offs[exp] - tile * t, 0, t).astype(jnp.int32)
    hi = jnp.clip(ends[exp] - tile * t, 0, t).astype(jnp.int32)
    prev_tile = jnp.concatenate([jnp.full((1,), -1, jnp.int32), tile[:-1]])
    prev_exp = jnp.concatenate([jnp.full((1,), -1, jnp.int32), exp[:-1]])
    return (tile, exp, lo, hi, (tile != prev_tile).astype(jnp.int32), (exp != prev_exp).astype(jnp.int32),
            used.reshape(1).astype(jnp.int32)), offs


def _group_ffn_kernel(tile_ref, exp_ref, lo_ref, hi_ref, first_ref, newexp_ref, used_ref,
                      x_ref, wg_ref, wu_ref, wd_ref, o_ref, wgb, wub, wdb):
    i = pl.program_id(0)
    t, h = x_ref.shape

    @pl.when(i < used_ref[0])
    def _():
        @pl.when(newexp_ref[i] == 1)
        def _():
            wgb[...] = wg_ref[0].astype(BF16)
            wub[...] = wu_ref[0].astype(BF16)
            wdb[...] = wd_ref[0].astype(BF16)

        left, right = _unpack_halves(x_ref[...])
        left = left.astype(BF16)
        right = right.astype(BF16)
        g = jnp.dot(left, wgb[:h], preferred_element_type=F32) + jnp.dot(right, wgb[h:], preferred_element_type=F32)
        u = jnp.dot(left, wub[:h], preferred_element_type=F32) + jnp.dot(right, wub[h:], preferred_element_type=F32)
        row = lax.broadcasted_iota(jnp.int32, (t, 1), 0)
        mine = (row >= lo_ref[i]) & (row < hi_ref[i])
        a = jnp.where(mine, g * _sigmoid(g) * u, 0.0)
        y = _pack_halves(jnp.dot(a.astype(BF16), wdb[...], preferred_element_type=F32))

        @pl.when(first_ref[i] == 1)
        def _():
            o_ref[...] = y

        @pl.when(first_ref[i] == 0)
        def _():
            o_ref[...] = jnp.where(mine, y, o_ref[...])


def moe_group_ffn(xs, meta, wg, wu, wd, t):
    r, h = xs.shape
    e, d, f = wg.shape
    n_items = r // t + e - 1
    grid_spec = pltpu.PrefetchScalarGridSpec(
        num_scalar_prefetch=7,
        grid=(n_items,),
        in_specs=[pl.BlockSpec((t, h), lambda i, tile, *_: (tile[i], 0)),
                  pl.BlockSpec((1, d, f), lambda i, tile, exp, *_: (exp[i], 0, 0)),
                  pl.BlockSpec((1, d, f), lambda i, tile, exp, *_: (exp[i], 0, 0)),
                  pl.BlockSpec((1, f, d), lambda i, tile, exp, *_: (exp[i], 0, 0))],
        out_specs=pl.BlockSpec((t, h), lambda i, tile, *_: (tile[i], 0)),
        scratch_shapes=[pltpu.VMEM((d, f), BF16), pltpu.VMEM((d, f), BF16), pltpu.VMEM((f, d), BF16)],
    )
    return pl.pallas_call(
        _group_ffn_kernel,
        grid_spec=grid_spec,
        out_shape=jax.ShapeDtypeStruct((r, h), jnp.uint32),
        compiler_params=_params("arbitrary"),
        name="moe_group_ffn",
    )(*meta, xs, wg, wu, wd)


def _combine_kernel(dst_ref, nxt_ref, wts_ref, x_ref, xb_ref, sg_ref, su_ref, sd_ref, g_ref, b_ref, ys_hbm,
                    o_ref, ob_ref, buf, sem, *, alpha, tm, steps):
    i = pl.program_id(0)
    slot = i % 2

    def gather(idx_ref, s):
        for k in range(TOP_K):
            def body(t, carry, k=k):
                d = idx_ref[0, 0, k * tm + t]
                pltpu.make_async_copy(ys_hbm.at[pl.ds(d, 1)], buf.at[s, k, pl.ds(t, 1)], sem.at[s]).start()
                return carry
            lax.fori_loop(0, tm, body, 0, unroll=DISPATCH_UNROLL)

    @pl.when(i == 0)
    def _():
        gather(dst_ref, 0)

    @pl.when(i + 1 < steps)
    def _():
        gather(nxt_ref, 1 - slot)

    for k in range(TOP_K):
        pltpu.make_async_copy(ys_hbm.at[pl.ds(0, tm)], buf.at[slot, k], sem.at[slot]).wait()

    h = buf.shape[-1]
    acc_l = jnp.zeros((tm, h), F32)
    acc_r = jnp.zeros((tm, h), F32)
    for k in range(TOP_K):
        left, right = _unpack_halves(buf[slot, k])
        c = wts_ref[:, k:k + 1]
        acc_l += c * left
        acc_r += c * right
    xb = xb_ref[...]
    hs = jnp.dot(xb, sg_ref[...], preferred_element_type=F32)
    hs = hs * _sigmoid(hs) * jnp.dot(xb, su_ref[...], preferred_element_type=F32)
    y = jnp.concatenate([acc_l, acc_r], axis=1) + jnp.dot(hs.astype(BF16), sd_ref[...], preferred_element_type=F32)
    out = _layer_norm(alpha * x_ref[...] + y, g_ref[...], b_ref[...])
    o_ref[...] = out
    ob_ref[...] = out.astype(BF16)


def moe_combine_ln(ys, dst_tiles, wts, x, xb, sg, su, sd, g, b, alpha, tm):
    m, d = x.shape
    h = ys.shape[1]
    fs = sg.shape[1]
    steps = m // tm
    row = pl.BlockSpec((tm, d), lambda i: (i, 0))
    vec = pl.BlockSpec((1, d), lambda i: (0, 0))
    return pl.pallas_call(
        functools.partial(_combine_kernel, alpha=alpha, tm=tm, steps=steps),
        grid=(steps,),
        in_specs=[pl.BlockSpec((1, 1, TOP_K * tm), lambda i: (i, 0, 0), memory_space=pltpu.SMEM),
                  pl.BlockSpec((1, 1, TOP_K * tm), lambda i: (jnp.minimum(i + 1, steps - 1), 0, 0),
                               memory_space=pltpu.SMEM),
                  pl.BlockSpec((tm, TOP_K), lambda i: (i, 0)),
                  row, row,
                  pl.BlockSpec((d, fs), lambda i: (0, 0)),
                  pl.BlockSpec((d, fs), lambda i: (0, 0)),
                  pl.BlockSpec((fs, d), lambda i: (0, 0)),
                  vec, vec,
                  pl.BlockSpec(memory_space=pl.ANY)],
        out_specs=[row, row],
        out_shape=[jax.ShapeDtypeStruct((m, d), F32), jax.ShapeDtypeStruct((m, d), BF16)],
        scratch_shapes=[pltpu.VMEM((2, TOP_K, tm, h), jnp.uint32), pltpu.SemaphoreType.DMA((2,))],
        compiler_params=_params("arbitrary"),
        name="moe_combine_ln",
    )(dst_tiles, dst_tiles, wts, x, xb, sg, su, sd, g.reshape(1, d), b.reshape(1, d), ys)


def _tiles_of(dst, tm):
    k, m = dst.shape
    return dst.reshape(k, m // tm, tm).transpose(1, 0, 2).reshape(m // tm, 1, k * tm)


def moe_block(xf, xb, xw, w_router, router_bias, wg, wu, wd, sg, su, sd, g, b, alpha,
              t_rows=256, tm_dispatch=512, tm_combine=128):
    m = xf.shape[0]
    t_rows = _tile(TOP_K * m, t_rows, 8)
    tm_dispatch = _tile(m, tm_dispatch, 8)
    tm_combine = _tile(m, tm_combine, 16)
    idx, wts, rank, counts = router(xf, w_router, router_bias)
    meta, offs = _group_metadata(counts.reshape(-1), TOP_K * m, t_rows)
    dst = offs[idx] + rank
    xs = moe_dispatch(xw, _tiles_of(dst, tm_dispatch), tm_dispatch)
    ys = moe_group_ffn(xs, meta, wg, wu, wd, t_rows)
    return moe_combine_ln(ys, _tiles_of(dst, tm_combine), jnp.transpose(wts), xf, xb,
                          sg.astype(BF16), su.astype(BF16), sd.astype(BF16), g, b, alpha, tm_combine)


def kernel(x, w_in_ab, b_in_ab, w_in_cd, b_in_cd, hgrn_lb_logits, hgrn_norm, sinks, rel_bias, gla_w2, gla_b,
           gla_norm, w_out, ln_g, ln_b, w_router, router_bias, w_exp_gate, w_exp_up, w_exp_down, w_sh_gate,
           w_sh_up, w_sh_down):
    bsz, s, d = x.shape
    m = bsz * s
    depth = w_out.shape[0]
    alpha = (2.0 * depth) ** 0.25
    half = d // 2
    a_heads = half // HEAD_DIM
    b_heads = half // HEAD_DIM
    b_kv_heads = b_heads // 4
    c_dk = (d // 4) // C_HEADS
    c_dv = half // C_HEADS
    d_heads = half // HEAD_DIM
    rank = gla_w2.shape[1]
    c_cols = 2 * C_HEADS * c_dk + 2 * C_HEADS * c_dv
    d_cols = 3 * d_heads * HEAD_DIM

    lbs = lower_bounds(hgrn_lb_logits)
    bias = band_bias(rel_bias)

    xf = x.reshape(m, d).astype(F32)
    xb = xf.astype(BF16)
    for l in range(depth):
        j = l // 2
        w_o = w_out[l].astype(BF16)
        if l % 2 == 0:
            proj = matmul_bias(xb, w_in_ab[j].astype(BF16), b_in_ab[j], F32).reshape(bsz, s, -1)
            o1 = hgrn2_mixer(proj, lbs[l], hgrn_norm[j], a_heads)
            o2 = swa_mixer(proj, 4 * a_heads * HEAD_DIM, b_heads, b_kv_heads, sinks[j], bias)
        else:
            w_in = w_in_cd[j]
            b_in = b_in_cd[j]
            d0 = c_cols + rank
            f0 = d0 + d_cols
            proj_c = matmul_bias(xb, w_in[:, :c_cols].astype(BF16), b_in[:c_cols], F32).reshape(bsz, s, -1)
            qfold = jnp.where(jnp.arange(d_cols) < d_heads * HEAD_DIM, LOG2E * HEAD_DIM ** -0.5, 1.0).astype(F32)
            proj_d = matmul_bias(xb, (w_in[:, d0:f0] * qfold).astype(BF16), b_in[d0:f0] * qfold,
                                 BF16).reshape(bsz, s, -1)
            w_small = jnp.concatenate([w_in[:, c_cols:d0], w_in[:, f0:]], axis=1).astype(BF16)
            b_small = jnp.concatenate([b_in[c_cols:d0], b_in[f0:]])
            small = matmul_bias(xb, w_small, b_small, F32).reshape(bsz, s, -1)
            o1 = gla_mixer(proj_c, small[..., :rank], gla_w2[j].astype(F32), gla_b[j].astype(F32), gla_norm[j],
                           C_HEADS, c_dk, c_dv)
            o2 = fox_mixer(proj_d, fox_log_decay(small[..., rank:]), d_heads)
        mix = matmul_pair(o1.reshape(m, half), o2.reshape(m, half), w_o[:half], w_o[half:], F32)
        xf, xb, xw = add_layer_norm(xf, mix, ln_g[l, 0], ln_b[l, 0], alpha)
        xf, xb = moe_block(xf, xb, xw, w_router[l], router_bias[l], w_exp_gate[l], w_exp_up[l], w_exp_down[l],
                           w_sh_gate[l], w_sh_up[l], w_sh_down[l], ln_g[l, 1], ln_b[l, 1], alpha)
    return xf.reshape(bsz, s, d).astype(x.dtype)
```

```python
import functools
import math

import numpy as np
import jax
import jax.numpy as jnp
from jax import lax
from jax.experimental import pallas as pl
from jax.experimental.pallas import tpu as pltpu

F32 = jnp.float32
BF16 = jnp.bfloat16
HIGHEST = lax.Precision.HIGHEST

HEAD_DIM = 128
WINDOW = 128
C_HEADS = 4
GLA_TAU = 16.0
CHUNK = 64
SUB = 16
N_BUCKETS = 32
MAX_DISTANCE = 128
TOP_K = 8
N_GROUPS = 8
TOPK_GROUPS = 4
ROUTED_SCALE = 2.5
LN_EPS = 1e-5
RMS_EPS = 1e-6
NEG_BIG = -1e30

VMEM_LIMIT_BYTES = 52 * 1024 * 1024


def _params(*sem):
    return pltpu.CompilerParams(dimension_semantics=sem, vmem_limit_bytes=VMEM_LIMIT_BYTES)


def _tile(n, pref, align):
    t = min(pref, n)
    t -= t % align
    while t >= align:
        if n % t == 0:
            return t
        t -= align
    return n


def _sigmoid(x):
    return 1.0 / (1.0 + jnp.exp(-x))


def _log_sigmoid(x):
    return jnp.minimum(x, 0.0) - jnp.log(1.0 + jnp.exp(-jnp.abs(x)))


def _dot_nt(a, b, **kw):
    return lax.dot_general(a, b, (((1,), (1,)), ((), ())), preferred_element_type=F32, **kw)


def _dot_tn(a, b, **kw):
    return lax.dot_general(a, b, (((0,), (0,)), ((), ())), preferred_element_type=F32, **kw)


def _mm_kernel(x_ref, w_ref, b_ref, o_ref):
    acc = jnp.dot(x_ref[...], w_ref[...], preferred_element_type=F32)
    o_ref[...] = (acc + b_ref[...]).astype(o_ref.dtype)


def matmul_bias(x, w, b, out_dtype, tm_pref=512, tn_pref=1024):
    m, k = x.shape
    n = w.shape[1]
    tm = _tile(m, tm_pref, 8)
    tn = _tile(n, tn_pref, 128)
    return pl.pallas_call(
        _mm_kernel,
        grid=(n // tn, m // tm),
        in_specs=[pl.BlockSpec((tm, k), lambda j, i: (i, 0)),
                  pl.BlockSpec((k, tn), lambda j, i: (0, j)),
                  pl.BlockSpec((1, tn), lambda j, i: (0, j))],
        out_specs=pl.BlockSpec((tm, tn), lambda j, i: (i, j)),
        out_shape=jax.ShapeDtypeStruct((m, n), out_dtype),
        compiler_params=_params("parallel", "parallel"),
        name="matmul_bias",
    )(x, w, b.reshape(1, n).astype(F32))


def _mm2_kernel(a1_ref, a2_ref, w1_ref, w2_ref, o_ref):
    acc = jnp.dot(a1_ref[...], w1_ref[...], preferred_element_type=F32)
    acc += jnp.dot(a2_ref[...], w2_ref[...], preferred_element_type=F32)
    o_ref[...] = acc.astype(o_ref.dtype)


def matmul_pair(a1, a2, w1, w2, out_dtype, tm_pref=512, tn_pref=1024):
    m, k1 = a1.shape
    k2 = a2.shape[1]
    n = w1.shape[1]
    tm = _tile(m, tm_pref, 8)
    tn = _tile(n, tn_pref, 128)
    return pl.pallas_call(
        _mm2_kernel,
        grid=(n // tn, m // tm),
        in_specs=[pl.BlockSpec((tm, k1), lambda j, i: (i, 0)),
                  pl.BlockSpec((tm, k2), lambda j, i: (i, 0)),
                  pl.BlockSpec((k1, tn), lambda j, i: (0, j)),
                  pl.BlockSpec((k2, tn), lambda j, i: (0, j))],
        out_specs=pl.BlockSpec((tm, tn), lambda j, i: (i, j)),
        out_shape=jax.ShapeDtypeStruct((m, n), out_dtype),
        compiler_params=_params("parallel", "parallel"),
        name="matmul_pair",
    )(a1, a2, w1, w2)


def _pack_halves(v):
    h = v.shape[1] // 2
    r = v.astype(BF16).astype(F32)
    lo = lax.bitcast_convert_type(r[:, :h], jnp.uint32) >> 16
    hi = lax.bitcast_convert_type(r[:, h:], jnp.uint32) & jnp.uint32(0xFFFF0000)
    return lo | hi


def _unpack_halves(w):
    left = lax.bitcast_convert_type(w << 16, F32)
    right = lax.bitcast_convert_type(w & jnp.uint32(0xFFFF0000), F32)
    return left, right


def _layer_norm(z, g, b):
    mu = jnp.mean(z, axis=-1, keepdims=True)
    zc = z - mu
    var = jnp.mean(zc * zc, axis=-1, keepdims=True)
    return zc * lax.rsqrt(var + LN_EPS) * g + b


def _add_ln_kernel(x_ref, y_ref, g_ref, b_ref, o_ref, ob_ref, ow_ref, *, alpha):
    out = _layer_norm(alpha * x_ref[...] + y_ref[...], g_ref[...], b_ref[...])
    o_ref[...] = out
    ob_ref[...] = out.astype(BF16)
    ow_ref[...] = _pack_halves(out)


def add_layer_norm(x, y, g, b, alpha, tm_pref=256):
    m, d = x.shape
    tm = _tile(m, tm_pref, 16)
    row = pl.BlockSpec((tm, d), lambda i: (i, 0))
    half = pl.BlockSpec((tm, d // 2), lambda i: (i, 0))
    vec = pl.BlockSpec((1, d), lambda i: (0, 0))
    return pl.pallas_call(
        functools.partial(_add_ln_kernel, alpha=alpha),
        grid=(m // tm,),
        in_specs=[row, row, vec, vec],
        out_specs=[row, row, half],
        out_shape=[jax.ShapeDtypeStruct((m, d), F32), jax.ShapeDtypeStruct((m, d), BF16),
                   jax.ShapeDtypeStruct((m, d // 2), jnp.uint32)],
        compiler_params=_params("parallel"),
        name="add_layer_norm",
    )(x, y, g.reshape(1, d), b.reshape(1, d))


def _lower_bounds_kernel(z_ref, o_ref):
    z = z_ref[...]
    depth = z.shape[0]
    e = jnp.exp(z - jnp.max(z, axis=0, keepdims=True))
    p = e / jnp.sum(e, axis=0, keepdims=True)
    run = jnp.zeros_like(p[0:1])
    for l in range(depth):
        run = run + p[l:l + 1]
        o_ref[l:l + 1, :] = run - p[0:1]


def lower_bounds(logits):
    return pl.pallas_call(
        _lower_bounds_kernel,
        out_shape=jax.ShapeDtypeStruct(logits.shape, F32),
        name="hgrn_lower_bounds",
    )(logits.astype(F32))


def _gla_chunk(q, k, v, g, st_ref):
    dk = q.shape[1]
    nsub = CHUNK // SUB
    row = lax.broadcasted_iota(jnp.int32, (CHUNK, CHUNK), 0)
    col = lax.broadcasted_iota(jnp.int32, (CHUNK, CHUNK), 1)
    tri = (col <= row).astype(F32)
    b = jnp.dot(tri, g, precision=HIGHEST, preferred_element_type=F32)
    b_last = b[CHUNK - 1:CHUNK, :]
    st = st_ref[...]

    qe = q * jnp.exp(b)
    o = _dot_nt(qe.astype(BF16), st.astype(BF16))

    levels = [jnp.zeros((1, dk), F32)] + [b[SUB * i - 1:SUB * i, :] for i in range(1, nsub)]
    lvl_rows = jnp.concatenate([jnp.broadcast_to(r, (SUB, dk)) for r in levels], axis=0)
    qt = q * jnp.exp(b - lvl_rows)
    row_blk = lax.broadcasted_iota(jnp.int32, (CHUNK, dk), 0) // SUB
    qhat = jnp.concatenate([jnp.where(row_blk == i, qt, 0.0) for i in range(1, nsub)], axis=1)
    khat = jnp.concatenate([k * jnp.exp(jnp.minimum(levels[i] - b, 0.0)) for i in range(1, nsub)], axis=1)
    p = _dot_nt(qhat.astype(BF16), khat.astype(BF16))
    p = jnp.where(col < (row // SUB) * SUB, p, 0.0)
    o += jnp.dot(p.astype(BF16), v.astype(BF16), preferred_element_type=F32)

    b4 = b.reshape(nsub, SUB, dk)
    q4 = q.reshape(nsub, SUB, dk)
    k4 = k.reshape(nsub, SUB, dk)
    ti = lax.broadcasted_iota(jnp.int32, (1, SUB, SUB, 1), 1)
    si = lax.broadcasted_iota(jnp.int32, (1, SUB, SUB, 1), 2)
    diff = b4[:, :, None, :] - b4[:, None, :, :]
    dec = jnp.exp(jnp.where(si <= ti, diff, NEG_BIG))
    dg = jnp.sum(q4[:, :, None, :] * k4[:, None, :, :] * dec, axis=-1)
    v4 = v.reshape(nsub, SUB, v.shape[1])
    od = jnp.einsum("its,isv->itv", dg.astype(BF16), v4.astype(BF16), preferred_element_type=F32)
    o += od.reshape(CHUNK, v.shape[1])

    kd = k * jnp.exp(b_last - b)
    st_ref[...] = st * jnp.exp(b_last) + _dot_tn(v.astype(BF16), kd.astype(BF16))
    return o


def _hgrn_kernel(q_ref, f_ref, i_ref, gate_ref, lb_ref, nw_ref, o_ref, st_ref, *, chunks):
    @pl.when(pl.program_id(2) == 0)
    def _():
        st_ref[...] = jnp.zeros_like(st_ref)

    lb = lb_ref[...]
    nw = nw_ref[...]

    def body(c, carry):
        rows = pl.ds(pl.multiple_of(c * CHUNK, CHUNK), CHUNK)
        qa = q_ref[0, rows, :]
        f = lb + (1.0 - lb) * _sigmoid(f_ref[0, rows, :])
        o = _gla_chunk(qa * _sigmoid(qa), 1.0 - f, i_ref[0, rows, :], jnp.log(f), st_ref)
        o = o * lax.rsqrt(jnp.mean(o * o, axis=-1, keepdims=True) + RMS_EPS) * nw
        o_ref[0, rows, :] = (o * _sigmoid(gate_ref[0, rows, :])).astype(o_ref.dtype)
        return carry

    lax.fori_loop(0, chunks, body, 0, unroll=2)


def hgrn2_mixer(proj, lb, norm_w, heads, t_pref=512):
    bsz, s, _ = proj.shape
    d = HEAD_DIM
    t = _tile(s, t_pref, CHUNK)

    def col(group):
        return pl.BlockSpec((1, t, d), lambda b, h, i, group=group: (b, i, group * heads + h))

    return pl.pallas_call(
        functools.partial(_hgrn_kernel, chunks=t // CHUNK),
        grid=(bsz, heads, s // t),
        in_specs=[col(0), col(1), col(2), col(3),
                  pl.BlockSpec((1, d), lambda b, h, i: (0, h)),
                  pl.BlockSpec((1, d), lambda b, h, i: (0, 0))],
        out_specs=pl.BlockSpec((1, t, d), lambda b, h, i: (b, i, h)),
        out_shape=jax.ShapeDtypeStruct((bsz, s, heads * d), BF16),
        scratch_shapes=[pltpu.VMEM((d, d), F32)],
        compiler_params=_params("parallel", "parallel", "arbitrary"),
        name="hgrn2_mixer",
    )(proj, proj, proj, proj, lb.reshape(1, heads * d), norm_w.reshape(1, d))


def _gla_kernel(q_ref, k_ref, v_ref, gate_ref, a_ref, w2_ref, gb_ref, nw_ref, o_ref, st_ref, *, chunks, scale):
    @pl.when(pl.program_id(2) == 0)
    def _():
        st_ref[...] = jnp.zeros_like(st_ref)

    w2 = w2_ref[...]
    gb = gb_ref[...]
    nw = nw_ref[...]

    def body(c, carry):
        rows = pl.ds(pl.multiple_of(c * CHUNK, CHUNK), CHUNK)
        logit = jnp.dot(a_ref[0, rows, :], w2, precision=HIGHEST, preferred_element_type=F32) + gb
        g = _log_sigmoid(logit) * (1.0 / GLA_TAU)
        o = _gla_chunk(q_ref[0, rows, :] * scale, k_ref[0, rows, :], v_ref[0, rows, :], g, st_ref)
        o = o * lax.rsqrt(jnp.mean(o * o, axis=-1, keepdims=True) + RMS_EPS) * nw
        gate = gate_ref[0, rows, :]
        o_ref[0, rows, :] = (o * gate * _sigmoid(gate)).astype(o_ref.dtype)
        return carry

    lax.fori_loop(0, chunks, body, 0, unroll=2)


def gla_mixer(proj, a, w2, gb, norm_w, heads, dk, dv, t_pref=512):
    bsz, s, _ = proj.shape
    rank = a.shape[-1]
    t = _tile(s, t_pref, CHUNK)
    kv = dv // dk
    assert dv % dk == 0
    return pl.pallas_call(
        functools.partial(_gla_kernel, chunks=t // CHUNK, scale=dk ** -0.5),
        grid=(bsz, heads, s // t),
        in_specs=[pl.BlockSpec((1, t, dk), lambda b, h, i: (b, i, h)),
                  pl.BlockSpec((1, t, dk), lambda b, h, i: (b, i, heads + h)),
                  pl.BlockSpec((1, t, dv), lambda b, h, i: (b, i, 2 * heads // kv + h)),
                  pl.BlockSpec((1, t, dv), lambda b, h, i: (b, i, 2 * heads // kv + heads + h)),
                  pl.BlockSpec((1, t, rank), lambda b, h, i: (b, i, 0)),
                  pl.BlockSpec((rank, dk), lambda b, h, i: (0, h)),
                  pl.BlockSpec((1, dk), lambda b, h, i: (0, h)),
                  pl.BlockSpec((1, dv), lambda b, h, i: (0, 0))],
        out_specs=pl.BlockSpec((1, t, dv), lambda b, h, i: (b, i, h)),
        out_shape=jax.ShapeDtypeStruct((bsz, s, heads * dv), BF16),
        scratch_shapes=[pltpu.VMEM((dv, dk), F32)],
        compiler_params=_params("parallel", "parallel", "arbitrary"),
        name="gla_mixer",
    )(proj, proj, proj, proj, a, w2, gb.reshape(1, heads * dk), norm_w.reshape(1, dv))


def _t5_bucket_table():
    dist = WINDOW + np.arange(WINDOW)[:, None] - np.arange(2 * WINDOW)[None, :]
    max_exact = N_BUCKETS // 2
    d = np.maximum(dist, 0)
    large = max_exact + (np.log(np.maximum(d, 1).astype(np.float32) / max_exact)
                         / math.log(MAX_DISTANCE / max_exact) * (N_BUCKETS - max_exact)).astype(np.int32)
    large = np.minimum(large, N_BUCKETS - 1)
    return np.where(d < max_exact, d, large).astype(np.int32)


def _band_bias_kernel(rel_ref, bucket_ref, o_ref):
    h = pl.program_id(0)
    bucket = bucket_ref[...]
    acc = jnp.zeros(bucket.shape, F32)
    for n in range(N_BUCKETS):
        acc = jnp.where(bucket == n, rel_ref[n, h], acc)
    o_ref[0] = acc


def band_bias(rel_bias):
    heads = rel_bias.shape[1]
    bucket = jnp.asarray(_t5_bucket_table())
    return pl.pallas_call(
        _band_bias_kernel,
        grid=(heads,),
        in_specs=[pl.BlockSpec(memory_space=pltpu.SMEM),
                  pl.BlockSpec((WINDOW, 2 * WINDOW), lambda h: (0, 0))],
        out_specs=pl.BlockSpec((1, WINDOW, 2 * WINDOW), lambda h: (h, 0, 0)),
        out_shape=jax.ShapeDtypeStruct((heads, WINDOW, 2 * WINDOW), F32),
        name="t5_band_bias",
    )(rel_bias.astype(F32), bucket)


def _swa_kernel(sink_ref, q_ref, kp_ref, kc_ref, vp_ref, vc_ref, bias_ref, o_ref, *, group):
    n = pl.program_id(1)
    hkv = pl.program_id(2)
    w = WINDOW
    d = HEAD_DIM
    scale = d ** -0.5
    kb = jnp.concatenate([kp_ref[0], kc_ref[0]], axis=0).astype(BF16)
    vb = jnp.concatenate([vp_ref[0], vc_ref[0]], axis=0).astype(BF16)
    qi = lax.broadcasted_iota(jnp.int32, (w, 2 * w), 0)
    ki = lax.broadcasted_iota(jnp.int32, (w, 2 * w), 1)
    dist = w + qi - ki
    mask = (dist >= 0) & (dist < w) & ((n * w - w + ki) >= 0)
    for gi in range(group):
        q = q_ref[0, :, gi * d:(gi + 1) * d].astype(BF16)
        s = _dot_nt(q, kb) * scale + bias_ref[gi]
        s = jnp.where(mask, s, NEG_BIG)
        sink = sink_ref[hkv * group + gi]
        m = jnp.maximum(jnp.max(s, axis=-1, keepdims=True), sink)
        p = jnp.exp(s - m)
        p = p / (jnp.sum(p, axis=-1, keepdims=True) + jnp.exp(sink - m))
        o = jnp.dot(p.astype(BF16), vb, preferred_element_type=F32)
        o_ref[0, :, gi * d:(gi + 1) * d] = o.astype(o_ref.dtype)


def swa_mixer(proj, col0, heads, kv_heads, sinks, bias):
    bsz, s, _ = proj.shape
    w = WINDOW
    d = HEAD_DIM
    group = heads // kv_heads
    nb = s // w
    q0 = col0 // (group * d)
    k0 = (col0 + heads * d) // d
    v0 = k0 + kv_heads
    assert col0 % (group * d) == 0

    def prev(c0):
        return pl.BlockSpec((1, w, d), lambda b, n, h: (b, jnp.maximum(n - 1, 0), c0 + h))

    def cur(c0):
        return pl.BlockSpec((1, w, d), lambda b, n, h: (b, n, c0 + h))

    return pl.pallas_call(
        functools.partial(_swa_kernel, group=group),
        grid=(bsz, nb, kv_heads),
        in_specs=[pl.BlockSpec(memory_space=pltpu.SMEM),
                  pl.BlockSpec((1, w, group * d), lambda b, n, h: (b, n, q0 + h)),
                  prev(k0), cur(k0), prev(v0), cur(v0),
                  pl.BlockSpec((group, w, 2 * w), lambda b, n, h: (h, 0, 0))],
        out_specs=pl.BlockSpec((1, w, group * d), lambda b, n, h: (b, n, h)),
        out_shape=jax.ShapeDtypeStruct((bsz, s, heads * d), BF16),
        compiler_params=_params("parallel", "parallel", "parallel"),
        name="swa_mixer",
    )(sinks.astype(F32), proj, proj, proj, proj, proj, bias)


def _fox_cumsum_kernel(f_ref, o_ref, *, per_head):
    ls = _log_sigmoid(f_ref[0])
    r = ls.shape[0]
    li = lax.broadcasted_iota(jnp.int32, (128, 128), 0)
    lj = lax.broadcasted_iota(jnp.int32, (128, 128), 1)
    within = jnp.dot(ls, (li <= lj).astype(F32), precision=HIGHEST, preferred_element_type=F32)
    total = jnp.dot(ls, jnp.ones((128, 128), F32), precision=HIGHEST, preferred_element_type=F32)
    ri = lax.broadcasted_iota(jnp.int32, (r, r), 0)
    rj = lax.broadcasted_iota(jnp.int32, (r, r), 1)
    before = ((rj < ri) & (rj // per_head == ri // per_head)).astype(F32)
    o_ref[0] = within + jnp.dot(before, total, precision=HIGHEST, preferred_element_type=F32)


def fox_log_decay(f_logit):
    bsz, s, h = f_logit.shape
    per_head = s // 128
    f = jnp.transpose(f_logit, (0, 2, 1)).reshape(bsz, h * per_head, 128)
    c = pl.pallas_call(
        functools.partial(_fox_cumsum_kernel, per_head=per_head),
        grid=(bsz,),
        in_specs=[pl.BlockSpec((1, h * per_head, 128), lambda b: (b, 0, 0))],
        out_specs=pl.BlockSpec((1, h * per_head, 128), lambda b: (b, 0, 0)),
        out_shape=jax.ShapeDtypeStruct((bsz, h * per_head, 128), F32),
        compiler_params=_params("parallel"),
        name="fox_log_decay",
    )(f)
    return c.reshape(bsz, h, s)


FOX_HEADS_PER_STEP = 2
LOG2E = math.log2(math.e)


def _fox_kernel(q_ref, k_ref, v_ref, ck_ref, o_ref, m_ref, l_ref, acc_ref, *, tq, tk, hp):
    i = pl.program_id(2)
    j = pl.program_id(3)
    d = HEAD_DIM

    @pl.when(j == 0)
    def _():
        m_ref[...] = jnp.full_like(m_ref, NEG_BIG)
        l_ref[...] = jnp.zeros_like(l_ref)
        acc_ref[...] = jnp.zeros_like(acc_ref)

    def step(masked):
        if masked:
            qpos = i * tq + lax.broadcasted_iota(jnp.int32, (tq, tk), 0)
            kpos = j * tk + lax.broadcasted_iota(jnp.int32, (tq, tk), 1)
            keep = kpos <= qpos
        for h in range(hp):
            cols = slice(h * d, (h + 1) * d)
            s = _dot_nt(q_ref[0, :, cols], k_ref[0, :, cols]) - ck_ref[0, h] * LOG2E
            if masked:
                s = jnp.where(keep, s, NEG_BIG)
            m_prev = m_ref[h]
            m_new = jnp.maximum(m_prev, jnp.max(s, axis=-1, keepdims=True))
            a = jnp.exp2(m_prev - m_new)
            p = jnp.exp2(s - m_new)
            l_ref[h] = a * l_ref[h] + jnp.sum(p, axis=-1, keepdims=True)
            acc_ref[h] = a * acc_ref[h] + jnp.dot(p.astype(BF16), v_ref[0, :, cols], preferred_element_type=F32)
            m_ref[h] = m_new

    @pl.when(j < i)
    def _():
        step(False)

    @pl.when(j == i)
    def _():
        step(True)
        for h in range(hp):
            o_ref[0, :, h * d:(h + 1) * d] = (acc_ref[h] / l_ref[h]).astype(o_ref.dtype)


def fox_mixer(proj, c, heads, t_pref=512):
    bsz, s, _ = proj.shape
    d = HEAD_DIM
    hp = FOX_HEADS_PER_STEP
    assert heads % hp == 0
    t = _tile(s, t_pref, 128)
    nb = s // t
    groups = heads // hp
    c_row = c.reshape(bsz, heads, 1, s)
    return pl.pallas_call(
        functools.partial(_fox_kernel, tq=t, tk=t, hp=hp),
        grid=(bsz, groups, nb, nb),
        in_specs=[pl.BlockSpec((1, t, hp * d), lambda b, h, i, j: (b, i, h)),
                  pl.BlockSpec((1, t, hp * d), lambda b, h, i, j: (b, jnp.minimum(j, i), groups + h)),
                  pl.BlockSpec((1, t, hp * d), lambda b, h, i, j: (b, jnp.minimum(j, i), 2 * groups + h)),
                  pl.BlockSpec((1, hp, 1, t), lambda b, h, i, j: (b, h, 0, jnp.minimum(j, i)))],
        out_specs=pl.BlockSpec((1, t, hp * d), lambda b, h, i, j: (b, i, h)),
        out_shape=jax.ShapeDtypeStruct((bsz, s, heads * d), BF16),
        scratch_shapes=[pltpu.VMEM((hp, t, 1), F32), pltpu.VMEM((hp, t, 1), F32), pltpu.VMEM((hp, t, d), F32)],
        compiler_params=_params("parallel", "parallel", "parallel", "arbitrary"),
        name="fox_mixer",
    )(proj, proj, proj, c_row)


def _router_kernel(x_ref, wt_ref, rb_ref, idx_ref, wts_ref, rank_ref, cnt_ref, carry_ref):
    e = wt_ref.shape[0]
    tm = x_ref.shape[0]
    per = e // N_GROUPS

    @pl.when(pl.program_id(0) == 0)
    def _():
        carry_ref[...] = jnp.zeros_like(carry_ref)

    logits = _dot_nt(wt_ref[...], x_ref[...], precision=HIGHEST)
    scores = _sigmoid(logits)
    biased = scores + rb_ref[...]
    g3 = biased.reshape(N_GROUPS, per, tm)
    mi = lax.broadcasted_iota(jnp.int32, (N_GROUPS, per, tm), 1)
    m1 = jnp.max(g3, axis=1, keepdims=True)
    i1 = jnp.min(jnp.where(g3 == m1, mi, per), axis=1, keepdims=True)
    m2 = jnp.max(jnp.where(mi == i1, -jnp.inf, g3), axis=1, keepdims=True)
    cur = (m1 + m2).reshape(N_GROUPS, tm)
    gi = lax.broadcasted_iota(jnp.int32, (N_GROUPS, tm), 0)
    sel = jnp.zeros((N_GROUPS, tm), jnp.bool_)
    for _ in range(TOPK_GROUPS):
        mx = jnp.max(cur, axis=0, keepdims=True)
        pick = gi == jnp.min(jnp.where(cur == mx, gi, N_GROUPS), axis=0, keepdims=True)
        sel = sel | pick
        cur = jnp.where(pick, -jnp.inf, cur)
    ok = jnp.broadcast_to(sel.reshape(N_GROUPS, 1, tm), (N_GROUPS, per, tm)).reshape(e, tm)
    cur = jnp.where(ok, biased, NEG_BIG)
    ei = lax.broadcasted_iota(jnp.int32, (e, tm), 0)
    chosen = jnp.zeros((e, tm), jnp.bool_)
    picks = []
    for k in range(TOP_K):
        mx = jnp.max(cur, axis=0, keepdims=True)
        ek = jnp.min(jnp.where(cur == mx, ei, e), axis=0, keepdims=True)
        pick = ei == ek
        chosen = chosen | pick
        cur = jnp.where(pick, -jnp.inf, cur)
        picks.append((ek, pick, jnp.sum(jnp.where(pick, scores, 0.0), axis=0, keepdims=True)))
    denom = sum(w for _, _, w in picks)
    si = lax.broadcasted_iota(jnp.int32, (tm, tm), 0)
    ti = lax.broadcasted_iota(jnp.int32, (tm, tm), 1)
    before = jnp.dot(chosen.astype(BF16), (si < ti).astype(BF16), preferred_element_type=F32)
    before = before + carry_ref[...]
    for k, (ek, pick, w) in enumerate(picks):
        idx_ref[k:k + 1, :] = ek
        wts_ref[k:k + 1, :] = w / denom * ROUTED_SCALE
        rank_ref[k:k + 1, :] = jnp.sum(jnp.where(pick, before, 0.0), axis=0, keepdims=True).astype(jnp.int32)
    carry_ref[...] += jnp.sum(chosen.astype(F32), axis=1, keepdims=True)
    cnt_ref[...] = carry_ref[...].astype(jnp.int32)


def router(x, w_router, router_bias, tm_pref=512):
    m, d = x.shape
    e = w_router.shape[1]
    tm = _tile(m, tm_pref, 128)
    pick = pl.BlockSpec((TOP_K, tm), lambda i: (0, i))
    return pl.pallas_call(
        _router_kernel,
        grid=(m // tm,),
        in_specs=[pl.BlockSpec((tm, d), lambda i: (i, 0)),
                  pl.BlockSpec((e, d), lambda i: (0, 0)),
                  pl.BlockSpec((e, 1), lambda i: (0, 0))],
        out_specs=[pick, pick, pick, pl.BlockSpec((e, 1), lambda i: (0, 0))],
        out_shape=[jax.ShapeDtypeStruct((TOP_K, m), jnp.int32), jax.ShapeDtypeStruct((TOP_K, m), F32),
                   jax.ShapeDtypeStruct((TOP_K, m), jnp.int32), jax.ShapeDtypeStruct((e, 1), jnp.int32)],
        scratch_shapes=[pltpu.VMEM((e, 1), F32)],
        compiler_params=_params("arbitrary"),
        name="moe_router",
    )(x, jnp.transpose(w_router).astype(F32), router_bias.reshape(e, 1).astype(F32))


DISPATCH_UNROLL = 8


def _dispatch_kernel(dst_ref, x_ref, xs_hbm, sem, *, tm):
    for k in range(TOP_K):
        def body(t, carry, k=k):
            pltpu.make_async_copy(x_ref.at[pl.ds(t, 1)], xs_hbm.at[pl.ds(dst_ref[k, t], 1)], sem).start()
            return carry
        lax.fori_loop(0, tm, body, 0, unroll=DISPATCH_UNROLL)
    for k in range(TOP_K):
        pltpu.make_async_copy(x_ref, xs_hbm.at[pl.ds(0, tm)], sem).wait()


def moe_dispatch(xw, dst, tm):
    m, h = xw.shape
    return pl.pallas_call(
        functools.partial(_dispatch_kernel, tm=tm),
        grid=(m // tm,),
        in_specs=[pl.BlockSpec((TOP_K, tm), lambda i: (0, i), memory_space=pltpu.SMEM),
                  pl.BlockSpec((tm, h), lambda i: (i, 0))],
        out_specs=pl.BlockSpec(memory_space=pl.ANY),
        out_shape=jax.ShapeDtypeStruct((TOP_K * m, h), jnp.uint32),
        scratch_shapes=[pltpu.SemaphoreType.DMA(())],
        compiler_params=_params("arbitrary"),
        name="moe_dispatch",
    )(dst, xw)


def _group_metadata(counts, n_rows, t):
    e = counts.shape[0]
    n_tiles = n_rows // t
    n_items = n_tiles + e - 1
    ends = jnp.cumsum(counts)
    offs = ends - counts
    first_tile = offs // t
    tiles_of = jnp.where(counts > 0, (ends - 1) // t - first_tile + 1, 0)
    item_end = jnp.cumsum(tiles_of)
    used = item_end[-1]
    ids = jnp.minimum(jnp.arange(n_items, dtype=jnp.int32), used - 1)
    exp = jnp.minimum(jnp.sum(item_end[None, :] <= ids[:, None], axis=1), e - 1).astype(jnp.int32)
    tile = (first_tile[exp] + ids - (item_end[exp] - tiles_of[exp])).astype(jnp.int32)
    lo = jnp.clip(offs[exp] - tile * t, 0, t).astype(jnp.int32)
    hi = jnp.clip(ends[exp] - tile * t, 0, t).astype(jnp.int32)
    prev_tile = jnp.concatenate([jnp.full((1,), -1, jnp.int32), tile[:-1]])
    prev_exp = jnp.concatenate([jnp.full((1,), -1, jnp.int32), exp[:-1]])
    return (tile, exp, lo, hi, (tile != prev_tile).astype(jnp.int32), (exp != prev_exp).astype(jnp.int32),
            used.reshape(1).astype(jnp.int32)), offs


def _group_ffn_kernel(tile_ref, exp_ref, lo_ref, hi_ref, first_ref, newexp_ref, used_ref,
                      x_ref, wg_ref, wu_ref, wd_ref, o_ref, wgb, wub, wdb):
    i = pl.program_id(0)
    t, h = x_ref.shape

    @pl.when(i < used_ref[0])
    def _():
        @pl.when(newexp_ref[i] == 1)
        def _():
            wgb[...] = wg_ref[0].astype(BF16)
            wub[...] = wu_ref[0].astype(BF16)
            wdb[...] = wd_ref[0].astype(BF16)

        left, right = _unpack_halves(x_ref[...])
        left = left.astype(BF16)
        right = right.astype(BF16)
        g = jnp.dot(left, wgb[:h], preferred_element_type=F32) + jnp.dot(right, wgb[h:], preferred_element_type=F32)
        u = jnp.dot(left, wub[:h], preferred_element_type=F32) + jnp.dot(right, wub[h:], preferred_element_type=F32)
        row = lax.broadcasted_iota(jnp.int32, (t, 1), 0)
        mine = (row >= lo_ref[i]) & (row < hi_ref[i])
        a = jnp.where(mine, g * _sigmoid(g) * u, 0.0)
        y = _pack_halves(jnp.dot(a.astype(BF16), wdb[...], preferred_element_type=F32))

        @pl.when(first_ref[i] == 1)
        def _():
            o_ref[...] = y

        @pl.when(first_ref[i] == 0)
        def _():
            o_ref[...] = jnp.where(mine, y, o_ref[...])


def moe_group_ffn(xs, meta, wg, wu, wd, t):
    r, h = xs.shape
    e, d, f = wg.shape
    n_items = r // t + e - 1
    grid_spec = pltpu.PrefetchScalarGridSpec(
        num_scalar_prefetch=7,
        grid=(n_items,),
        in_specs=[pl.BlockSpec((t, h), lambda i, tile, *_: (tile[i], 0)),
                  pl.BlockSpec((1, d, f), lambda i, tile, exp, *_: (exp[i], 0, 0)),
                  pl.BlockSpec((1, d, f), lambda i, tile, exp, *_: (exp[i], 0, 0)),
                  pl.BlockSpec((1, f, d), lambda i, tile, exp, *_: (exp[i], 0, 0))],
        out_specs=pl.BlockSpec((t, h), lambda i, tile, *_: (tile[i], 0)),
        scratch_shapes=[pltpu.VMEM((d, f), BF16), pltpu.VMEM((d, f), BF16), pltpu.VMEM((f, d), BF16)],
    )
    return pl.pallas_call(
        _group_ffn_kernel,
        grid_spec=grid_spec,
        out_shape=jax.ShapeDtypeStruct((r, h), jnp.uint32),
        compiler_params=_params("arbitrary"),
        name="moe_group_ffn",
    )(*meta, xs, wg, wu, wd)


def _combine_kernel(dst_ref, nxt_ref, wts_ref, x_ref, xb_ref, sg_ref, su_ref, sd_ref, g_ref, b_ref, ys_hbm,
                    o_ref, ob_ref, buf, sem, *, alpha, tm, steps):
    i = pl.program_id(0)
    slot = i % 2

    def gather(idx_ref, s):
        for k in range(TOP_K):
            def body(t, carry, k=k):
                pltpu.make_async_copy(ys_hbm.at[pl.ds(idx_ref[k, t], 1)], buf.at[s, k, pl.ds(t, 1)],
                                      sem.at[s]).start()
                return carry
            lax.fori_loop(0, tm, body, 0, unroll=DISPATCH_UNROLL)

    @pl.when(i == 0)
    def _():
        gather(dst_ref, 0)

    @pl.when(i + 1 < steps)
    def _():
        gather(nxt_ref, 1 - slot)

    for k in range(TOP_K):
        pltpu.make_async_copy(ys_hbm.at[pl.ds(0, tm)], buf.at[slot, k], sem.at[slot]).wait()

    h = buf.shape[-1]
    acc_l = jnp.zeros((tm, h), F32)
    acc_r = jnp.zeros((tm, h), F32)
    wts = jnp.transpose(wts_ref[...])
    for k in range(TOP_K):
        left, right = _unpack_halves(buf[slot, k])
        c = wts[:, k:k + 1]
        acc_l += c * left
        acc_r += c * right
    xb = xb_ref[...]
    hs = jnp.dot(xb, sg_ref[...], preferred_element_type=F32)
    hs = hs * _sigmoid(hs) * jnp.dot(xb, su_ref[...], preferred_element_type=F32)
    y = jnp.concatenate([acc_l, acc_r], axis=1) + jnp.dot(hs.astype(BF16), sd_ref[...], preferred_element_type=F32)
    out = _layer_norm(alpha * x_ref[...] + y, g_ref[...], b_ref[...])
    o_ref[...] = out
    ob_ref[...] = out.astype(BF16)


def moe_combine_ln(ys, dst_tiles, wts, x, xb, sg, su, sd, g, b, alpha, tm):
    m, d = x.shape
    h = ys.shape[1]
    fs = sg.shape[1]
    steps = m // tm
    row = pl.BlockSpec((tm, d), lambda i: (i, 0))
    vec = pl.BlockSpec((1, d), lambda i: (0, 0))
    return pl.pallas_call(
        functools.partial(_combine_kernel, alpha=alpha, tm=tm, steps=steps),
        grid=(steps,),
        in_specs=[pl.BlockSpec((TOP_K, tm), lambda i: (0, i), memory_space=pltpu.SMEM),
                  pl.BlockSpec((TOP_K, tm), lambda i: (0, jnp.minimum(i + 1, steps - 1)), memory_space=pltpu.SMEM),
                  pl.BlockSpec((TOP_K, tm), lambda i: (0, i)),
                  row, row,
                  pl.BlockSpec((d, fs), lambda i: (0, 0)),
                  pl.BlockSpec((d, fs), lambda i: (0, 0)),
                  pl.BlockSpec((fs, d), lambda i: (0, 0)),
                  vec, vec,
                  pl.BlockSpec(memory_space=pl.ANY)],
        out_specs=[row, row],
        out_shape=[jax.ShapeDtypeStruct((m, d), F32), jax.ShapeDtypeStruct((m, d), BF16)],
        scratch_shapes=[pltpu.VMEM((2, TOP_K, tm, h), jnp.uint32), pltpu.SemaphoreType.DMA((2,))],
        compiler_params=_params("arbitrary"),
        name="moe_combine_ln",
    )(dst_tiles, dst_tiles, wts, x, xb, sg, su, sd, g.reshape(1, d), b.reshape(1, d), ys)


def moe_block(xf, xb, xw, w_router, router_bias, wg, wu, wd, sg, su, sd, g, b, alpha,
              t_rows=256, tm_dispatch=512, tm_combine=128):
    m = xf.shape[0]
    e = w_router.shape[1]
    t_rows = _tile(TOP_K * m, t_rows, 8)
    tm_dispatch = _tile(m, tm_dispatch, 128)
    tm_combine = _tile(m, tm_combine, 128)
    idx, wts, rank, counts = router(xf, w_router, router_bias)
    meta, offs = _group_metadata(counts.reshape(-1), TOP_K * m, t_rows)
    start = jnp.sum(jnp.where(idx[:, None, :] == jnp.arange(e, dtype=jnp.int32)[None, :, None],
                              offs[None, :, None], 0), axis=1)
    dst = (start + rank).astype(jnp.int32)
    xs = moe_dispatch(xw, dst, tm_dispatch)
    ys = moe_group_ffn(xs, meta, wg, wu, wd, t_rows)
    return moe_combine_ln(ys, dst, wts, xf, xb, sg.astype(BF16), su.astype(BF16), sd.astype(BF16),
                          g, b, alpha, tm_combine)


def kernel(x, w_in_ab, b_in_ab, w_in_cd, b_in_cd, hgrn_lb_logits, hgrn_norm, sinks, rel_bias, gla_w2, gla_b,
           gla_norm, w_out, ln_g, ln_b, w_router, router_bias, w_exp_gate, w_exp_up, w_exp_down, w_sh_gate,
           w_sh_up, w_sh_down):
    bsz, s, d = x.shape
    m = bsz * s
    depth = w_out.shape[0]
    alpha = (2.0 * depth) ** 0.25
    half = d // 2
    a_heads = half // HEAD_DIM
    b_heads = half // HEAD_DIM
    b_kv_heads = b_heads // 4
    c_dk = (d // 4) // C_HEADS
    c_dv = half // C_HEADS
    d_heads = half // HEAD_DIM
    rank = gla_w2.shape[1]
    c_cols = 2 * C_HEADS * c_dk + 2 * C_HEADS * c_dv
    d_cols = 3 * d_heads * HEAD_DIM

    lbs = lower_bounds(hgrn_lb_logits)
    bias = band_bias(rel_bias)

    xf = x.reshape(m, d).astype(F32)
    xb = xf.astype(BF16)
    for l in range(depth):
        j = l // 2
        w_o = w_out[l].astype(BF16)
        if l % 2 == 0:
            proj = matmul_bias(xb, w_in_ab[j].astype(BF16), b_in_ab[j], F32).reshape(bsz, s, -1)
            o1 = hgrn2_mixer(proj, lbs[l], hgrn_norm[j], a_heads)
            o2 = swa_mixer(proj, 4 * a_heads * HEAD_DIM, b_heads, b_kv_heads, sinks[j], bias)
        else:
            w_in = w_in_cd[j]
            b_in = b_in_cd[j]
            d0 = c_cols + rank
            f0 = d0 + d_cols
            proj_c = matmul_bias(xb, w_in[:, :c_cols].astype(BF16), b_in[:c_cols], F32).reshape(bsz, s, -1)
            qfold = jnp.where(jnp.arange(d_cols) < d_heads * HEAD_DIM, LOG2E * HEAD_DIM ** -0.5, 1.0).astype(F32)
            proj_d = matmul_bias(xb, (w_in[:, d0:f0] * qfold).astype(BF16), b_in[d0:f0] * qfold,
                                 BF16).reshape(bsz, s, -1)
            w_small = jnp.concatenate([w_in[:, c_cols:d0], w_in[:, f0:]], axis=1).astype(BF16)
            b_small = jnp.concatenate([b_in[c_cols:d0], b_in[f0:]])
            small = matmul_bias(xb, w_small, b_small, F32).reshape(bsz, s, -1)
            o1 = gla_mixer(proj_c, small[..., :rank], gla_w2[j].astype(F32), gla_b[j].astype(F32), gla_norm[j],
                           C_HEADS, c_dk, c_dv)
            o2 = fox_mixer(proj_d, fox_log_decay(small[..., rank:]), d_heads)
        mix = matmul_pair(o1.reshape(m, half), o2.reshape(m, half), w_o[:half], w_o[half:], F32)
        xf, xb, xw = add_layer_norm(xf, mix, ln_g[l, 0], ln_b[l, 0], alpha)
        xf, xb = moe_block(xf, xb, xw, w_router[l], router_bias[l], w_exp_gate[l], w_exp_up[l], w_exp_down[l],
                           w_sh_gate[l], w_sh_up[l], w_sh_down[l], ln_g[l, 1], ln_b[l, 1], alpha)
    return xf.reshape(bsz, s, d).astype(x.dtype)
```

```python
import functools
import math

import numpy as np
import jax
import jax.numpy as jnp
from jax import lax
from jax.experimental import pallas as pl
from jax.experimental.pallas import tpu as pltpu

F32 = jnp.float32
BF16 = jnp.bfloat16
HIGHEST = lax.Precision.HIGHEST

HEAD_DIM = 128
WINDOW = 128
C_HEADS = 4
GLA_TAU = 16.0
CHUNK = 64
SUB = 16
N_BUCKETS = 32
MAX_DISTANCE = 128
TOP_K = 8
N_GROUPS = 8
TOPK_GROUPS = 4
ROUTED_SCALE = 2.5
LN_EPS = 1e-5
RMS_EPS = 1e-6
NEG_BIG = -1e30

VMEM_LIMIT_BYTES = 52 * 1024 * 1024


def _params(*sem):
    return pltpu.CompilerParams(dimension_semantics=sem, vmem_limit_bytes=VMEM_LIMIT_BYTES)


def _tile(n, pref, align):
    t = min(pref, n)
    t -= t % align
    while t >= align:
        if n % t == 0:
            return t
        t -= align
    return n


def _sigmoid(x):
    return 1.0 / (1.0 + jnp.exp(-x))


def _log_sigmoid(x):
    return jnp.minimum(x, 0.0) - jnp.log(1.0 + jnp.exp(-jnp.abs(x)))


def _dot_nt(a, b, **kw):
    return lax.dot_general(a, b, (((1,), (1,)), ((), ())), preferred_element_type=F32, **kw)


def _dot_tn(a, b, **kw):
    return lax.dot_general(a, b, (((0,), (0,)), ((), ())), preferred_element_type=F32, **kw)


def _mm_kernel(x_ref, w_ref, b_ref, o_ref):
    acc = jnp.dot(x_ref[...], w_ref[...], preferred_element_type=F32)
    o_ref[...] = (acc + b_ref[...]).astype(o_ref.dtype)


def matmul_bias(x, w, b, out_dtype, tm_pref=512, tn_pref=1024):
    m, k = x.shape
    n = w.shape[1]
    tm = _tile(m, tm_pref, 8)
    tn = _tile(n, tn_pref, 128)
    return pl.pallas_call(
        _mm_kernel,
        grid=(n // tn, m // tm),
        in_specs=[pl.BlockSpec((tm, k), lambda j, i: (i, 0)),
                  pl.BlockSpec((k, tn), lambda j, i: (0, j)),
                  pl.BlockSpec((1, tn), lambda j, i: (0, j))],
        out_specs=pl.BlockSpec((tm, tn), lambda j, i: (i, j)),
        out_shape=jax.ShapeDtypeStruct((m, n), out_dtype),
        compiler_params=_params("parallel", "parallel"),
        name="matmul_bias",
    )(x, w, b.reshape(1, n).astype(F32))


def _mm2_kernel(a1_ref, a2_ref, w1_ref, w2_ref, o_ref):
    acc = jnp.dot(a1_ref[...], w1_ref[...], preferred_element_type=F32)
    acc += jnp.dot(a2_ref[...], w2_ref[...], preferred_element_type=F32)
    o_ref[...] = acc.astype(o_ref.dtype)


def matmul_pair(a1, a2, w1, w2, out_dtype, tm_pref=512, tn_pref=1024):
    m, k1 = a1.shape
    k2 = a2.shape[1]
    n = w1.shape[1]
    tm = _tile(m, tm_pref, 8)
    tn = _tile(n, tn_pref, 128)
    return pl.pallas_call(
        _mm2_kernel,
        grid=(n // tn, m // tm),
        in_specs=[pl.BlockSpec((tm, k1), lambda j, i: (i, 0)),
                  pl.BlockSpec((tm, k2), lambda j, i: (i, 0)),
                  pl.BlockSpec((k1, tn), lambda j, i: (0, j)),
                  pl.BlockSpec((k2, tn), lambda j, i: (0, j))],
        out_specs=pl.BlockSpec((tm, tn), lambda j, i: (i, j)),
        out_shape=jax.ShapeDtypeStruct((m, n), out_dtype),
        compiler_params=_params("parallel", "parallel"),
        name="matmul_pair",
    )(a1, a2, w1, w2)


def _pack_halves(v):
    h = v.shape[1] // 2
    r = v.astype(BF16).astype(F32)
    lo = lax.bitcast_convert_type(r[:, :h], jnp.uint32) >> 16
    hi = lax.bitcast_convert_type(r[:, h:], jnp.uint32) & jnp.uint32(0xFFFF0000)
    return lo | hi


def _unpack_halves(w):
    left = lax.bitcast_convert_type(w << 16, F32)
    right = lax.bitcast_convert_type(w & jnp.uint32(0xFFFF0000), F32)
    return left, right


def _layer_norm(z, g, b):
    mu = jnp.mean(z, axis=-1, keepdims=True)
    zc = z - mu
    var = jnp.mean(zc * zc, axis=-1, keepdims=True)
    return zc * lax.rsqrt(var + LN_EPS) * g + b


def _add_ln_kernel(x_ref, y_ref, g_ref, b_ref, o_ref, ob_ref, ow_ref, *, alpha):
    out = _layer_norm(alpha * x_ref[...] + y_ref[...], g_ref[...], b_ref[...])
    o_ref[...] = out
    ob_ref[...] = out.astype(BF16)
    ow_ref[...] = _pack_halves(out)


def add_layer_norm(x, y, g, b, alpha, tm_pref=256):
    m, d = x.shape
    tm = _tile(m, tm_pref, 16)
    row = pl.BlockSpec((tm, d), lambda i: (i, 0))
    half = pl.BlockSpec((tm, d // 2), lambda i: (i, 0))
    vec = pl.BlockSpec((1, d), lambda i: (0, 0))
    return pl.pallas_call(
        functools.partial(_add_ln_kernel, alpha=alpha),
        grid=(m // tm,),
        in_specs=[row, row, vec, vec],
        out_specs=[row, row, half],
        out_shape=[jax.ShapeDtypeStruct((m, d), F32), jax.ShapeDtypeStruct((m, d), BF16),
                   jax.ShapeDtypeStruct((m, d // 2), jnp.uint32)],
        compiler_params=_params("parallel"),
        name="add_layer_norm",
    )(x, y, g.reshape(1, d), b.reshape(1, d))


def _lower_bounds_kernel(z_ref, o_ref):
    z = z_ref[...]
    depth = z.shape[0]
    e = jnp.exp(z - jnp.max(z, axis=0, keepdims=True))
    p = e / jnp.sum(e, axis=0, keepdims=True)
    run = jnp.zeros_like(p[0:1])
    for l in range(depth):
        run = run + p[l:l + 1]
        o_ref[l:l + 1, :] = run - p[0:1]


def lower_bounds(logits):
    return pl.pallas_call(
        _lower_bounds_kernel,
        out_shape=jax.ShapeDtypeStruct(logits.shape, F32),
        name="hgrn_lower_bounds",
    )(logits.astype(F32))


def _gla_chunk(q, k, v, g, st_ref):
    dk = q.shape[1]
    nsub = CHUNK // SUB
    row = lax.broadcasted_iota(jnp.int32, (CHUNK, CHUNK), 0)
    col = lax.broadcasted_iota(jnp.int32, (CHUNK, CHUNK), 1)
    tri = (col <= row).astype(F32)
    b = jnp.dot(tri, g, precision=HIGHEST, preferred_element_type=F32)
    b_last = b[CHUNK - 1:CHUNK, :]
    st = st_ref[...]

    qe = q * jnp.exp(b)
    o = _dot_nt(qe.astype(BF16), st.astype(BF16))

    levels = [jnp.zeros((1, dk), F32)] + [b[SUB * i - 1:SUB * i, :] for i in range(1, nsub)]
    lvl_rows = jnp.concatenate([jnp.broadcast_to(r, (SUB, dk)) for r in levels], axis=0)
    qt = q * jnp.exp(b - lvl_rows)
    row_blk = lax.broadcasted_iota(jnp.int32, (CHUNK, dk), 0) // SUB
    qhat = jnp.concatenate([jnp.where(row_blk == i, qt, 0.0) for i in range(1, nsub)], axis=1)
    khat = jnp.concatenate([k * jnp.exp(jnp.minimum(levels[i] - b, 0.0)) for i in range(1, nsub)], axis=1)
    p = _dot_nt(qhat.astype(BF16), khat.astype(BF16))
    p = jnp.where(col < (row // SUB) * SUB, p, 0.0)
    o += jnp.dot(p.astype(BF16), v.astype(BF16), preferred_element_type=F32)

    b4 = b.reshape(nsub, SUB, dk)
    q4 = q.reshape(nsub, SUB, dk)
    k4 = k.reshape(nsub, SUB, dk)
    ti = lax.broadcasted_iota(jnp.int32, (1, SUB, SUB, 1), 1)
    si = lax.broadcasted_iota(jnp.int32, (1, SUB, SUB, 1), 2)
    diff = b4[:, :, None, :] - b4[:, None, :, :]
    dec = jnp.exp(jnp.where(si <= ti, diff, NEG_BIG))
    dg = jnp.sum(q4[:, :, None, :] * k4[:, None, :, :] * dec, axis=-1)
    v4 = v.reshape(nsub, SUB, v.shape[1])
    od = jnp.einsum("its,isv->itv", dg.astype(BF16), v4.astype(BF16), preferred_element_type=F32)
    o += od.reshape(CHUNK, v.shape[1])

    kd = k * jnp.exp(b_last - b)
    st_ref[...] = st * jnp.exp(b_last) + _dot_tn(v.astype(BF16), kd.astype(BF16))
    return o


HGRN_HEADS_PER_STEP = 4
GLA_HEADS_PER_STEP = 2
CHUNK_UNROLL = 2


def _hgrn_kernel(q_ref, f_ref, i_ref, gate_ref, lb_ref, nw_ref, o_ref, st_ref, *, chunks, hp):
    @pl.when(pl.program_id(2) == 0)
    def _():
        st_ref[...] = jnp.zeros_like(st_ref)

    d = HEAD_DIM
    nw = nw_ref[...]

    def body(c, carry):
        rows = pl.ds(pl.multiple_of(c * CHUNK, CHUNK), CHUNK)
        for h in range(hp):
            cols = slice(h * d, (h + 1) * d)
            lb = lb_ref[:, cols]
            qa = q_ref[0, rows, cols]
            f = lb + (1.0 - lb) * _sigmoid(f_ref[0, rows, cols])
            o = _gla_chunk(qa * _sigmoid(qa), 1.0 - f, i_ref[0, rows, cols], jnp.log(f), st_ref.at[h])
            o = o * lax.rsqrt(jnp.mean(o * o, axis=-1, keepdims=True) + RMS_EPS) * nw
            o_ref[0, rows, cols] = (o * _sigmoid(gate_ref[0, rows, cols])).astype(o_ref.dtype)
        return carry

    lax.fori_loop(0, chunks, body, 0, unroll=CHUNK_UNROLL)


def hgrn2_mixer(proj, lb, norm_w, heads, t_pref=512):
    bsz, s, _ = proj.shape
    d = HEAD_DIM
    hp = HGRN_HEADS_PER_STEP
    assert heads % hp == 0
    groups = heads // hp
    t = _tile(s, t_pref, CHUNK)

    def col(group):
        return pl.BlockSpec((1, t, hp * d), lambda b, h, i, group=group: (b, i, group * groups + h))

    return pl.pallas_call(
        functools.partial(_hgrn_kernel, chunks=t // CHUNK, hp=hp),
        grid=(bsz, groups, s // t),
        in_specs=[col(0), col(1), col(2), col(3),
                  pl.BlockSpec((1, hp * d), lambda b, h, i: (0, h)),
                  pl.BlockSpec((1, d), lambda b, h, i: (0, 0))],
        out_specs=pl.BlockSpec((1, t, hp * d), lambda b, h, i: (b, i, h)),
        out_shape=jax.ShapeDtypeStruct((bsz, s, heads * d), BF16),
        scratch_shapes=[pltpu.VMEM((hp, d, d), F32)],
        compiler_params=_params("parallel", "parallel", "arbitrary"),
        name="hgrn2_mixer",
    )(proj, proj, proj, proj, lb.reshape(1, heads * d), norm_w.reshape(1, d))


def _gla_kernel(q_ref, k_ref, v_ref, gate_ref, a_ref, w2_ref, gb_ref, nw_ref, o_ref, st_ref,
                *, chunks, scale, hp, dk, dv):
    @pl.when(pl.program_id(2) == 0)
    def _():
        st_ref[...] = jnp.zeros_like(st_ref)

    nw = nw_ref[...]

    def body(c, carry):
        rows = pl.ds(pl.multiple_of(c * CHUNK, CHUNK), CHUNK)
        a = a_ref[0, rows, :]
        for h in range(hp):
            kc = slice(h * dk, (h + 1) * dk)
            vc = slice(h * dv, (h + 1) * dv)
            logit = jnp.dot(a, w2_ref[:, kc], precision=HIGHEST, preferred_element_type=F32) + gb_ref[:, kc]
            g = _log_sigmoid(logit) * (1.0 / GLA_TAU)
            o = _gla_chunk(q_ref[0, rows, kc] * scale, k_ref[0, rows, kc], v_ref[0, rows, vc], g, st_ref.at[h])
            o = o * lax.rsqrt(jnp.mean(o * o, axis=-1, keepdims=True) + RMS_EPS) * nw
            gate = gate_ref[0, rows, vc]
            o_ref[0, rows, vc] = (o * gate * _sigmoid(gate)).astype(o_ref.dtype)
        return carry

    lax.fori_loop(0, chunks, body, 0, unroll=CHUNK_UNROLL)


def gla_mixer(proj, a, w2, gb, norm_w, heads, dk, dv, t_pref=512):
    bsz, s, _ = proj.shape
    rank = a.shape[-1]
    t = _tile(s, t_pref, CHUNK)
    hp = GLA_HEADS_PER_STEP
    assert heads % hp == 0 and (2 * heads * dk) % (hp * dv) == 0
    groups = heads // hp
    v0 = 2 * heads * dk // (hp * dv)
    return pl.pallas_call(
        functools.partial(_gla_kernel, chunks=t // CHUNK, scale=dk ** -0.5, hp=hp, dk=dk, dv=dv),
        grid=(bsz, groups, s // t),
        in_specs=[pl.BlockSpec((1, t, hp * dk), lambda b, h, i: (b, i, h)),
                  pl.BlockSpec((1, t, hp * dk), lambda b, h, i: (b, i, groups + h)),
                  pl.BlockSpec((1, t, hp * dv), lambda b, h, i: (b, i, v0 + h)),
                  pl.BlockSpec((1, t, hp * dv), lambda b, h, i: (b, i, v0 + groups + h)),
                  pl.BlockSpec((1, t, rank), lambda b, h, i: (b, i, 0)),
                  pl.BlockSpec((rank, hp * dk), lambda b, h, i: (0, h)),
                  pl.BlockSpec((1, hp * dk), lambda b, h, i: (0, h)),
                  pl.BlockSpec((1, dv), lambda b, h, i: (0, 0))],
        out_specs=pl.BlockSpec((1, t, hp * dv), lambda b, h, i: (b, i, h)),
        out_shape=jax.ShapeDtypeStruct((bsz, s, heads * dv), BF16),
        scratch_shapes=[pltpu.VMEM((hp, dv, dk), F32)],
        compiler_params=_params("parallel", "parallel", "arbitrary"),
        name="gla_mixer",
    )(proj, proj, proj, proj, a, w2, gb.reshape(1, heads * dk), norm_w.reshape(1, dv))


def _t5_bucket_table():
    dist = WINDOW + np.arange(WINDOW)[:, None] - np.arange(2 * WINDOW)[None, :]
    max_exact = N_BUCKETS // 2
    d = np.maximum(dist, 0)
    large = max_exact + (np.log(np.maximum(d, 1).astype(np.float32) / max_exact)
                         / math.log(MAX_DISTANCE / max_exact) * (N_BUCKETS - max_exact)).astype(np.int32)
    large = np.minimum(large, N_BUCKETS - 1)
    return np.where(d < max_exact, d, large).astype(np.int32)


def _band_bias_kernel(rel_ref, bucket_ref, o_ref):
    h = pl.program_id(0)
    bucket = bucket_ref[...]
    acc = jnp.zeros(bucket.shape, F32)
    for n in range(N_BUCKETS):
        acc = jnp.where(bucket == n, rel_ref[n, h], acc)
    o_ref[0] = acc


def band_bias(rel_bias):
    heads = rel_bias.shape[1]
    bucket = jnp.asarray(_t5_bucket_table())
    return pl.pallas_call(
        _band_bias_kernel,
        grid=(heads,),
        in_specs=[pl.BlockSpec(memory_space=pltpu.SMEM),
                  pl.BlockSpec((WINDOW, 2 * WINDOW), lambda h: (0, 0))],
        out_specs=pl.BlockSpec((1, WINDOW, 2 * WINDOW), lambda h: (h, 0, 0)),
        out_shape=jax.ShapeDtypeStruct((heads, WINDOW, 2 * WINDOW), F32),
        name="t5_band_bias",
    )(rel_bias.astype(F32), bucket)


def _swa_kernel(sink_ref, q_ref, kp_ref, kc_ref, vp_ref, vc_ref, bias_ref, o_ref, *, group):
    n = pl.program_id(1)
    hkv = pl.program_id(2)
    w = WINDOW
    d = HEAD_DIM
    scale = d ** -0.5
    kb = jnp.concatenate([kp_ref[0], kc_ref[0]], axis=0).astype(BF16)
    vb = jnp.concatenate([vp_ref[0], vc_ref[0]], axis=0).astype(BF16)
    qi = lax.broadcasted_iota(jnp.int32, (w, 2 * w), 0)
    ki = lax.broadcasted_iota(jnp.int32, (w, 2 * w), 1)
    dist = w + qi - ki
    mask = (dist >= 0) & (dist < w) & ((n * w - w + ki) >= 0)
    for gi in range(group):
        q = q_ref[0, :, gi * d:(gi + 1) * d].astype(BF16)
        s = _dot_nt(q, kb) * scale + bias_ref[gi]
        s = jnp.where(mask, s, NEG_BIG)
        sink = sink_ref[hkv * group + gi]
        m = jnp.maximum(jnp.max(s, axis=-1, keepdims=True), sink)
        p = jnp.exp(s - m)
        p = p / (jnp.sum(p, axis=-1, keepdims=True) + jnp.exp(sink - m))
        o = jnp.dot(p.astype(BF16), vb, preferred_element_type=F32)
        o_ref[0, :, gi * d:(gi + 1) * d] = o.astype(o_ref.dtype)


def swa_mixer(proj, col0, heads, kv_heads, sinks, bias):
    bsz, s, _ = proj.shape
    w = WINDOW
    d = HEAD_DIM
    group = heads // kv_heads
    nb = s // w
    q0 = col0 // (group * d)
    k0 = (col0 + heads * d) // d
    v0 = k0 + kv_heads
    assert col0 % (group * d) == 0

    def prev(c0):
        return pl.BlockSpec((1, w, d), lambda b, n, h: (b, jnp.maximum(n - 1, 0), c0 + h))

    def cur(c0):
        return pl.BlockSpec((1, w, d), lambda b, n, h: (b, n, c0 + h))

    return pl.pallas_call(
        functools.partial(_swa_kernel, group=group),
        grid=(bsz, nb, kv_heads),
        in_specs=[pl.BlockSpec(memory_space=pltpu.SMEM),
                  pl.BlockSpec((1, w, group * d), lambda b, n, h: (b, n, q0 + h)),
                  prev(k0), cur(k0), prev(v0), cur(v0),
                  pl.BlockSpec((group, w, 2 * w), lambda b, n, h: (h, 0, 0))],
        out_specs=pl.BlockSpec((1, w, group * d), lambda b, n, h: (b, n, h)),
        out_shape=jax.ShapeDtypeStruct((bsz, s, heads * d), BF16),
        compiler_params=_params("parallel", "parallel", "parallel"),
        name="swa_mixer",
    )(sinks.astype(F32), proj, proj, proj, proj, proj, bias)


def _fox_cumsum_kernel(f_ref, o_ref, *, per_head):
    ls = _log_sigmoid(f_ref[0])
    r = ls.shape[0]
    li = lax.broadcasted_iota(jnp.int32, (128, 128), 0)
    lj = lax.broadcasted_iota(jnp.int32, (128, 128), 1)
    within = jnp.dot(ls, (li <= lj).astype(F32), precision=HIGHEST, preferred_element_type=F32)
    total = jnp.dot(ls, jnp.ones((128, 128), F32), precision=HIGHEST, preferred_element_type=F32)
    ri = lax.broadcasted_iota(jnp.int32, (r, r), 0)
    rj = lax.broadcasted_iota(jnp.int32, (r, r), 1)
    before = ((rj < ri) & (rj // per_head == ri // per_head)).astype(F32)
    o_ref[0] = within + jnp.dot(before, total, precision=HIGHEST, preferred_element_type=F32)


def fox_log_decay(f_logit):
    bsz, s, h = f_logit.shape
    per_head = s // 128
    f = jnp.transpose(f_logit, (0, 2, 1)).reshape(bsz, h * per_head, 128)
    c = pl.pallas_call(
        functools.partial(_fox_cumsum_kernel, per_head=per_head),
        grid=(bsz,),
        in_specs=[pl.BlockSpec((1, h * per_head, 128), lambda b: (b, 0, 0))],
        out_specs=pl.BlockSpec((1, h * per_head, 128), lambda b: (b, 0, 0)),
        out_shape=jax.ShapeDtypeStruct((bsz, h * per_head, 128), F32),
        compiler_params=_params("parallel"),
        name="fox_log_decay",
    )(f)
    return c.reshape(bsz, h, s)


FOX_HEADS_PER_STEP = 8
LOG2E = math.log2(math.e)


def _fox_kernel(q_ref, k_ref, v_ref, ck_ref, o_ref, m_ref, acc_ref, *, tq, tk, hp):
    i = pl.program_id(2)
    j = pl.program_id(3)
    d = HEAD_DIM

    @pl.when(j == 0)
    def _():
        m_ref[...] = jnp.full_like(m_ref, NEG_BIG)
        acc_ref[...] = jnp.zeros_like(acc_ref)

    def step(masked):
        if masked:
            qpos = i * tq + lax.broadcasted_iota(jnp.int32, (tq, tk), 0)
            kpos = j * tk + lax.broadcasted_iota(jnp.int32, (tq, tk), 1)
            keep = kpos <= qpos
        ones = jnp.ones((tk, d), BF16)
        for h in range(hp):
            cols = slice(h * d, (h + 1) * d)
            s = _dot_nt(q_ref[0, :, cols], k_ref[0, :, cols]) - ck_ref[0, h] * LOG2E
            if masked:
                s = jnp.where(keep, s, NEG_BIG)
            m_prev = m_ref[h]
            m_new = jnp.maximum(m_prev, jnp.max(s, axis=-1, keepdims=True))
            a = jnp.exp2(m_prev - m_new)
            p = jnp.concatenate([jnp.exp2(s[:, c * d:(c + 1) * d] - m_new) for c in range(tk // d)], axis=1)
            v1 = jnp.concatenate([v_ref[0, :, cols], ones], axis=1)
            acc_ref[h] = jnp.concatenate([a, a], axis=1) * acc_ref[h] + jnp.dot(
                p.astype(BF16), v1, preferred_element_type=F32)
            m_ref[h] = m_new

    @pl.when(j < i)
    def _():
        step(False)

    @pl.when(j == i)
    def _():
        step(True)
        for h in range(hp):
            acc = acc_ref[h]
            o_ref[0, :, h * d:(h + 1) * d] = (acc[:, :d] / acc[:, d:]).astype(o_ref.dtype)


def fox_mixer(proj, c, heads, t_pref=512):
    bsz, s, _ = proj.shape
    d = HEAD_DIM
    hp = FOX_HEADS_PER_STEP
    assert heads % hp == 0
    t = _tile(s, t_pref, 128)
    nb = s // t
    groups = heads // hp
    c_row = c.reshape(bsz, heads, 1, s)
    return pl.pallas_call(
        functools.partial(_fox_kernel, tq=t, tk=t, hp=hp),
        grid=(bsz, groups, nb, nb),
        in_specs=[pl.BlockSpec((1, t, hp * d), lambda b, h, i, j: (b, i, h)),
                  pl.BlockSpec((1, t, hp * d), lambda b, h, i, j: (b, jnp.minimum(j, i), groups + h)),
                  pl.BlockSpec((1, t, hp * d), lambda b, h, i, j: (b, jnp.minimum(j, i), 2 * groups + h)),
                  pl.BlockSpec((1, hp, 1, t), lambda b, h, i, j: (b, h, 0, jnp.minimum(j, i)))],
        out_specs=pl.BlockSpec((1, t, hp * d), lambda b, h, i, j: (b, i, h)),
        out_shape=jax.ShapeDtypeStruct((bsz, s, heads * d), BF16),
        scratch_shapes=[pltpu.VMEM((hp, t, d), F32), pltpu.VMEM((hp, t, 2 * d), F32)],
        compiler_params=_params("parallel", "parallel", "parallel", "arbitrary"),
        name="fox_mixer",
    )(proj, proj, proj, c_row)


def _router_kernel(x_ref, wt_ref, rb_ref, idx_ref, wts_ref, rank_ref, cnt_ref, carry_ref):
    e = wt_ref.shape[0]
    tm = x_ref.shape[0]
    per = e // N_GROUPS

    @pl.when(pl.program_id(0) == 0)
    def _():
        carry_ref[...] = jnp.zeros_like(carry_ref)

    logits = _dot_nt(wt_ref[...], x_ref[...], precision=HIGHEST)
    scores = _sigmoid(logits)
    biased = scores + rb_ref[...]
    g3 = biased.reshape(N_GROUPS, per, tm)
    mi = lax.broadcasted_iota(jnp.int32, (N_GROUPS, per, tm), 1)
    m1 = jnp.max(g3, axis=1, keepdims=True)
    i1 = jnp.min(jnp.where(g3 == m1, mi, per), axis=1, keepdims=True)
    m2 = jnp.max(jnp.where(mi == i1, -jnp.inf, g3), axis=1, keepdims=True)
    cur = (m1 + m2).reshape(N_GROUPS, tm)
    gi = lax.broadcasted_iota(jnp.int32, (N_GROUPS, tm), 0)
    sel = jnp.zeros((N_GROUPS, tm), jnp.bool_)
    for _ in range(TOPK_GROUPS):
        mx = jnp.max(cur, axis=0, keepdims=True)
        pick = gi == jnp.min(jnp.where(cur == mx, gi, N_GROUPS), axis=0, keepdims=True)
        sel = sel | pick
        cur = jnp.where(pick, -jnp.inf, cur)
    ok = jnp.broadcast_to(sel.reshape(N_GROUPS, 1, tm), (N_GROUPS, per, tm)).reshape(e, tm)
    cur = jnp.where(ok, biased, NEG_BIG)
    ei = lax.broadcasted_iota(jnp.int32, (e, tm), 0)
    chosen = jnp.zeros((e, tm), jnp.bool_)
    picks = []
    for k in range(TOP_K):
        mx = jnp.max(cur, axis=0, keepdims=True)
        ek = jnp.min(jnp.where(cur == mx, ei, e), axis=0, keepdims=True)
        pick = ei == ek
        chosen = chosen | pick
        cur = jnp.where(pick, -jnp.inf, cur)
        picks.append((ek, pick, jnp.sum(jnp.where(pick, scores, 0.0), axis=0, keepdims=True)))
    denom = sum(w for _, _, w in picks)
    si = lax.broadcasted_iota(jnp.int32, (tm, tm), 0)
    ti = lax.broadcasted_iota(jnp.int32, (tm, tm), 1)
    before = jnp.dot(chosen.astype(BF16), (si < ti).astype(BF16), preferred_element_type=F32)
    before = before + carry_ref[...]
    for k, (ek, pick, w) in enumerate(picks):
        idx_ref[k:k + 1, :] = ek
        wts_ref[k:k + 1, :] = w / denom * ROUTED_SCALE
        rank_ref[k:k + 1, :] = jnp.sum(jnp.where(pick, before, 0.0), axis=0, keepdims=True).astype(jnp.int32)
    carry_ref[...] += jnp.sum(chosen.astype(F32), axis=1, keepdims=True)
    cnt_ref[...] = carry_ref[...].astype(jnp.int32)


def router(x, w_router, router_bias, tm_pref=512):
    m, d = x.shape
    e = w_router.shape[1]
    tm = _tile(m, tm_pref, 128)
    pick = pl.BlockSpec((TOP_K, tm), lambda i: (0, i))
    return pl.pallas_call(
        _router_kernel,
        grid=(m // tm,),
        in_specs=[pl.BlockSpec((tm, d), lambda i: (i, 0)),
                  pl.BlockSpec((e, d), lambda i: (0, 0)),
                  pl.BlockSpec((e, 1), lambda i: (0, 0))],
        out_specs=[pick, pick, pick, pl.BlockSpec((e, 1), lambda i: (0, 0))],
        out_shape=[jax.ShapeDtypeStruct((TOP_K, m), jnp.int32), jax.ShapeDtypeStruct((TOP_K, m), F32),
                   jax.ShapeDtypeStruct((TOP_K, m), jnp.int32), jax.ShapeDtypeStruct((e, 1), jnp.int32)],
        scratch_shapes=[pltpu.VMEM((e, 1), F32)],
        compiler_params=_params("arbitrary"),
        name="moe_router",
    )(x, jnp.transpose(w_router).astype(F32), router_bias.reshape(e, 1).astype(F32))


DISPATCH_UNROLL = 8


def _dispatch_kernel(dst_ref, x_ref, xs_hbm, sem, *, tm):
    for k in range(TOP_K):
        def body(t, carry, k=k):
            pltpu.make_async_copy(x_ref.at[pl.ds(t, 1)], xs_hbm.at[pl.ds(dst_ref[k, t], 1)], sem).start()
            return carry
        lax.fori_loop(0, tm, body, 0, unroll=DISPATCH_UNROLL)
    for k in range(TOP_K):
        pltpu.make_async_copy(x_ref, xs_hbm.at[pl.ds(0, tm)], sem).wait()


def moe_dispatch(xw, dst, tm):
    m, h = xw.shape
    return pl.pallas_call(
        functools.partial(_dispatch_kernel, tm=tm),
        grid=(m // tm,),
        in_specs=[pl.BlockSpec((TOP_K, tm), lambda i: (0, i), memory_space=pltpu.SMEM),
                  pl.BlockSpec((tm, h), lambda i: (i, 0))],
        out_specs=pl.BlockSpec(memory_space=pl.ANY),
        out_shape=jax.ShapeDtypeStruct((TOP_K * m, h), jnp.uint32),
        scratch_shapes=[pltpu.SemaphoreType.DMA(())],
        compiler_params=_params("arbitrary"),
        name="moe_dispatch",
    )(dst, xw)


def _group_metadata(counts, n_rows, t):
    e = counts.shape[0]
    n_tiles = n_rows // t
    n_items = n_tiles + e - 1
    ends = jnp.cumsum(counts)
    offs = ends - counts
    first_tile = offs // t
    tiles_of = jnp.where(counts > 0, (ends - 1) // t - first_tile + 1, 0)
    item_end = jnp.cumsum(tiles_of)
    used = item_end[-1]
    ids = jnp.minimum(jnp.arange(n_items, dtype=jnp.int32), used - 1)
    exp = jnp.minimum(jnp.sum(item_end[None, :] <= ids[:, None], axis=1), e - 1).astype(jnp.int32)
    tile = (first_tile[exp] + ids - (item_end[exp] - tiles_of[exp])).astype(jnp.int32)
    lo = jnp.clip(offs[exp] - tile * t, 0, t).astype(jnp.int32)
    hi = jnp.clip(ends[exp] - tile * t, 0, t).astype(jnp.int32)
    prev_tile = jnp.concatenate([jnp.full((1,), -1, jnp.int32), tile[:-1]])
    prev_exp = jnp.concatenate([jnp.full((1,), -1, jnp.int32), exp[:-1]])
    return (tile, exp, lo, hi, (tile != prev_tile).astype(jnp.int32), (exp != prev_exp).astype(jnp.int32),
            used.reshape(1).astype(jnp.int32)), offs


def _group_ffn_kernel(tile_ref, exp_ref, lo_ref, hi_ref, first_ref, newexp_ref, used_ref,
                      x_ref, wg_ref, wu_ref, wd_ref, o_ref, wgb, wub, wdb):
    i = pl.program_id(0)
    t, h = x_ref.shape

    @pl.when(i < used_ref[0])
    def _():
        @pl.when(newexp_ref[i] == 1)
        def _():
            wgb[...] = wg_ref[0].astype(BF16)
            wub[...] = wu_ref[0].astype(BF16)
            wdb[...] = wd_ref[0].astype(BF16)

        left, right = _unpack_halves(x_ref[...])
        left = left.astype(BF16)
        right = right.astype(BF16)
        g = jnp.dot(left, wgb[:h], preferred_element_type=F32) + jnp.dot(right, wgb[h:], preferred_element_type=F32)
        u = jnp.dot(left, wub[:h], preferred_element_type=F32) + jnp.dot(right, wub[h:], preferred_element_type=F32)
        row = lax.broadcasted_iota(jnp.int32, (t, 1), 0)
        mine = (row >= lo_ref[i]) & (row < hi_ref[i])
        a = jnp.where(mine, g * _sigmoid(g) * u, 0.0)
        y = _pack_halves(jnp.dot(a.astype(BF16), wdb[...], preferred_element_type=F32))

        @pl.when(first_ref[i] == 1)
        def _():
            o_ref[...] = y

        @pl.when(first_ref[i] == 0)
        def _():
            o_ref[...] = jnp.where(mine, y, o_ref[...])


def moe_group_ffn(xs, meta, wg, wu, wd, t):
    r, h = xs.shape
    e, d, f = wg.shape
    n_items = r // t + e - 1
    grid_spec = pltpu.PrefetchScalarGridSpec(
        num_scalar_prefetch=7,
        grid=(n_items,),
        in_specs=[pl.BlockSpec((t, h), lambda i, tile, *_: (tile[i], 0)),
                  pl.BlockSpec((1, d, f), lambda i, tile, exp, *_: (exp[i], 0, 0)),
                  pl.BlockSpec((1, d, f), lambda i, tile, exp, *_: (exp[i], 0, 0)),
                  pl.BlockSpec((1, f, d), lambda i, tile, exp, *_: (exp[i], 0, 0))],
        out_specs=pl.BlockSpec((t, h), lambda i, tile, *_: (tile[i], 0)),
        scratch_shapes=[pltpu.VMEM((d, f), BF16), pltpu.VMEM((d, f), BF16), pltpu.VMEM((f, d), BF16)],
    )
    return pl.pallas_call(
        _group_ffn_kernel,
        grid_spec=grid_spec,
        out_shape=jax.ShapeDtypeStruct((r, h), jnp.uint32),
        compiler_params=_params("arbitrary"),
        name="moe_group_ffn",
    )(*meta, xs, wg, wu, wd)


def _combine_kernel(dst_ref, nxt_ref, wts_ref, x_ref, xb_ref, sg_ref, su_ref, sd_ref, g_ref, b_ref, ys_hbm,
                    o_ref, ob_ref, buf, sem, *, alpha, tm, steps):
    i = pl.program_id(0)
    slot = i % 2

    def gather(idx_ref, s):
        for k in range(TOP_K):
            def body(t, carry, k=k):
                pltpu.make_async_copy(ys_hbm.at[pl.ds(idx_ref[k, t], 1)], buf.at[s, k, pl.ds(t, 1)],
                                      sem.at[s]).start()
                return carry
            lax.fori_loop(0, tm, body, 0, unroll=DISPATCH_UNROLL)

    @pl.when(i == 0)
    def _():
        gather(dst_ref, 0)

    @pl.when(i + 1 < steps)
    def _():
        gather(nxt_ref, 1 - slot)

    for k in range(TOP_K):
        pltpu.make_async_copy(ys_hbm.at[pl.ds(0, tm)], buf.at[slot, k], sem.at[slot]).wait()

    h = buf.shape[-1]
    acc_l = jnp.zeros((tm, h), F32)
    acc_r = jnp.zeros((tm, h), F32)
    wts = jnp.transpose(wts_ref[...])
    for k in range(TOP_K):
        left, right = _unpack_halves(buf[slot, k])
        c = wts[:, k:k + 1]
        acc_l += c * left
        acc_r += c * right
    xb = xb_ref[...]
    hs = jnp.dot(xb, sg_ref[...], preferred_element_type=F32)
    hs = hs * _sigmoid(hs) * jnp.dot(xb, su_ref[...], preferred_element_type=F32)
    y = jnp.concatenate([acc_l, acc_r], axis=1) + jnp.dot(hs.astype(BF16), sd_ref[...], preferred_element_type=F32)
    out = _layer_norm(alpha * x_ref[...] + y, g_ref[...], b_ref[...])
    o_ref[...] = out
    ob_ref[...] = out.astype(BF16)


def moe_combine_ln(ys, dst_tiles, wts, x, xb, sg, su, sd, g, b, alpha, tm):
    m, d = x.shape
    h = ys.shape[1]
    fs = sg.shape[1]
    steps = m // tm
    row = pl.BlockSpec((tm, d), lambda i: (i, 0))
    vec = pl.BlockSpec((1, d), lambda i: (0, 0))
    return pl.pallas_call(
        functools.partial(_combine_kernel, alpha=alpha, tm=tm, steps=steps),
        grid=(steps,),
        in_specs=[pl.BlockSpec((TOP_K, tm), lambda i: (0, i), memory_space=pltpu.SMEM),
                  pl.BlockSpec((TOP_K, tm), lambda i: (0, jnp.minimum(i + 1, steps - 1)), memory_space=pltpu.SMEM),
                  pl.BlockSpec((TOP_K, tm), lambda i: (0, i)),
                  row, row,
                  pl.BlockSpec((d, fs), lambda i: (0, 0)),
                  pl.BlockSpec((d, fs), lambda i: (0, 0)),
                  pl.BlockSpec((fs, d), lambda i: (0, 0)),
                  vec, vec,
                  pl.BlockSpec(memory_space=pl.ANY)],
        out_specs=[row, row],
        out_shape=[jax.ShapeDtypeStruct((m, d), F32), jax.ShapeDtypeStruct((m, d), BF16)],
        scratch_shapes=[pltpu.VMEM((2, TOP_K, tm, h), jnp.uint32), pltpu.SemaphoreType.DMA((2,))],
        compiler_params=_params("arbitrary"),
        name="moe_combine_ln",
    )(dst_tiles, dst_tiles, wts, x, xb, sg, su, sd, g.reshape(1, d), b.reshape(1, d), ys)


def moe_block(xf, xb, xw, w_router, router_bias, wg, wu, wd, sg, su, sd, g, b, alpha,
              t_rows=256, tm_dispatch=512, tm_combine=128):
    m = xf.shape[0]
    e = w_router.shape[1]
    t_rows = _tile(TOP_K * m, t_rows, 8)
    tm_dispatch = _tile(m, tm_dispatch, 128)
    tm_combine = _tile(m, tm_combine, 128)
    idx, wts, rank, counts = router(xf, w_router, router_bias)
    meta, offs = _group_metadata(counts.reshape(-1), TOP_K * m, t_rows)
    start = jnp.sum(jnp.where(idx[:, None, :] == jnp.arange(e, dtype=jnp.int32)[None, :, None],
                              offs[None, :, None], 0), axis=1)
    dst = (start + rank).astype(jnp.int32)
    xs = moe_dispatch(xw, dst, tm_dispatch)
    ys = moe_group_ffn(xs, meta, wg, wu, wd, t_rows)
    return moe_combine_ln(ys, dst, wts, xf, xb, sg.astype(BF16), su.astype(BF16), sd.astype(BF16),
                          g, b, alpha, tm_combine)


def kernel(x, w_in_ab, b_in_ab, w_in_cd, b_in_cd, hgrn_lb_logits, hgrn_norm, sinks, rel_bias, gla_w2, gla_b,
           gla_norm, w_out, ln_g, ln_b, w_router, router_bias, w_exp_gate, w_exp_up, w_exp_down, w_sh_gate,
           w_sh_up, w_sh_down):
    bsz, s, d = x.shape
    m = bsz * s
    depth = w_out.shape[0]
    alpha = (2.0 * depth) ** 0.25
    half = d // 2
    a_heads = half // HEAD_DIM
    b_heads = half // HEAD_DIM
    b_kv_heads = b_heads // 4
    c_dk = (d // 4) // C_HEADS
    c_dv = half // C_HEADS
    d_heads = half // HEAD_DIM
    rank = gla_w2.shape[1]
    c_cols = 2 * C_HEADS * c_dk + 2 * C_HEADS * c_dv
    d_cols = 3 * d_heads * HEAD_DIM

    lbs = lower_bounds(hgrn_lb_logits)
    bias = band_bias(rel_bias)

    xf = x.reshape(m, d).astype(F32)
    xb = xf.astype(BF16)
    for l in range(depth):
        j = l // 2
        w_o = w_out[l].astype(BF16)
        if l % 2 == 0:
            proj = matmul_bias(xb, w_in_ab[j].astype(BF16), b_in_ab[j], F32).reshape(bsz, s, -1)
            o1 = hgrn2_mixer(proj, lbs[l], hgrn_norm[j], a_heads)
            o2 = swa_mixer(proj, 4 * a_heads * HEAD_DIM, b_heads, b_kv_heads, sinks[j], bias)
        else:
            w_in = w_in_cd[j]
            b_in = b_in_cd[j]
            d0 = c_cols + rank
            f0 = d0 + d_cols
            proj_c = matmul_bias(xb, w_in[:, :c_cols].astype(BF16), b_in[:c_cols], F32).reshape(bsz, s, -1)
            qfold = jnp.where(jnp.arange(d_cols) < d_heads * HEAD_DIM, LOG2E * HEAD_DIM ** -0.5, 1.0).astype(F32)
            proj_d = matmul_bias(xb, (w_in[:, d0:f0] * qfold).astype(BF16), b_in[d0:f0] * qfold,
                                 BF16).reshape(bsz, s, -1)
            w_small = jnp.concatenate([w_in[:, c_cols:d0], w_in[:, f0:]], axis=1).astype(BF16)
            b_small = jnp.concatenate([b_in[c_cols:d0], b_in[f0:]])
            small = matmul_bias(xb, w_small, b_small, F32).reshape(bsz, s, -1)
            o1 = gla_mixer(proj_c, small[..., :rank], gla_w2[j].astype(F32), gla_b[j].astype(F32), gla_norm[j],
                           C_HEADS, c_dk, c_dv)
            o2 = fox_mixer(proj_d, fox_log_decay(small[..., rank:]), d_heads)
        mix = matmul_pair(o1.reshape(m, half), o2.reshape(m, half), w_o[:half], w_o[half:], F32)
        xf, xb, xw = add_layer_norm(xf, mix, ln_g[l, 0], ln_b[l, 0], alpha)
        xf, xb = moe_block(xf, xb, xw, w_router[l], router_bias[l], w_exp_gate[l], w_exp_up[l], w_exp_down[l],
                           w_sh_gate[l], w_sh_up[l], w_sh_down[l], ln_g[l, 1], ln_b[l, 1], alpha)
    return xf.reshape(bsz, s, d).astype(x.dtype)
```

```python
import functools
import math

import numpy as np
import jax
import jax.numpy as jnp
from jax import lax
from jax.experimental import pallas as pl
from jax.experimental.pallas import tpu as pltpu

F32 = jnp.float32
BF16 = jnp.bfloat16
HIGHEST = lax.Precision.HIGHEST

HEAD_DIM = 128
WINDOW = 128
C_HEADS = 4
GLA_TAU = 16.0
CHUNK = 64
SUB = 16
N_BUCKETS = 32
MAX_DISTANCE = 128
TOP_K = 8
N_GROUPS = 8
TOPK_GROUPS = 4
ROUTED_SCALE = 2.5
LN_EPS = 1e-5
RMS_EPS = 1e-6
NEG_BIG = -1e30

VMEM_LIMIT_BYTES = 52 * 1024 * 1024


def _params(*sem):
    return pltpu.CompilerParams(dimension_semantics=sem, vmem_limit_bytes=VMEM_LIMIT_BYTES)


def _tile(n, pref, align):
    t = min(pref, n)
    t -= t % align
    while t >= align:
        if n % t == 0:
            return t
        t -= align
    return n


def _sigmoid(x):
    return 1.0 / (1.0 + jnp.exp(-x))


def _log_sigmoid(x):
    return jnp.minimum(x, 0.0) - jnp.log(1.0 + jnp.exp(-jnp.abs(x)))


def _dot_nt(a, b, **kw):
    return lax.dot_general(a, b, (((1,), (1,)), ((), ())), preferred_element_type=F32, **kw)


def _dot_tn(a, b, **kw):
    return lax.dot_general(a, b, (((0,), (0,)), ((), ())), preferred_element_type=F32, **kw)


def _mm_kernel(x_ref, w_ref, b_ref, o_ref):
    acc = jnp.dot(x_ref[...], w_ref[...], preferred_element_type=F32)
    o_ref[...] = (acc + b_ref[...]).astype(o_ref.dtype)


def matmul_bias(x, w, b, out_dtype, tm_pref=512, tn_pref=1024):
    m, k = x.shape
    n = w.shape[1]
    tm = _tile(m, tm_pref, 8)
    tn = _tile(n, tn_pref, 128)
    return pl.pallas_call(
        _mm_kernel,
        grid=(n // tn, m // tm),
        in_specs=[pl.BlockSpec((tm, k), lambda j, i: (i, 0)),
                  pl.BlockSpec((k, tn), lambda j, i: (0, j)),
                  pl.BlockSpec((1, tn), lambda j, i: (0, j))],
        out_specs=pl.BlockSpec((tm, tn), lambda j, i: (i, j)),
        out_shape=jax.ShapeDtypeStruct((m, n), out_dtype),
        compiler_params=_params("parallel", "parallel"),
        name="matmul_bias",
    )(x, w, b.reshape(1, n).astype(F32))


def _mm_layer_kernel(x_ref, w_ref, b_ref, o_ref, wb_ref):
    @pl.when(pl.program_id(1) == 0)
    def _():
        wb_ref[...] = w_ref[0].astype(BF16)

    acc = jnp.dot(x_ref[...], wb_ref[...], preferred_element_type=F32)
    o_ref[...] = (acc + b_ref[...]).astype(o_ref.dtype)


def matmul_bias_layer(x, w_stack, layer, n_cols, b, out_dtype, tm_pref=512, tn_pref=512):
    m, k = x.shape
    tm = _tile(m, tm_pref, 8)
    tn = _tile(n_cols, tn_pref, 128)
    return pl.pallas_call(
        _mm_layer_kernel,
        grid=(n_cols // tn, m // tm),
        in_specs=[pl.BlockSpec((tm, k), lambda j, i: (i, 0)),
                  pl.BlockSpec((1, k, tn), lambda j, i: (layer, 0, j)),
                  pl.BlockSpec((1, tn), lambda j, i: (0, j))],
        out_specs=pl.BlockSpec((tm, tn), lambda j, i: (i, j)),
        out_shape=jax.ShapeDtypeStruct((m, n_cols), out_dtype),
        scratch_shapes=[pltpu.VMEM((k, tn), BF16)],
        compiler_params=_params("parallel", "arbitrary"),
        name="matmul_bias_layer",
    )(x, w_stack, b.reshape(1, n_cols).astype(F32))


def _mm2_kernel(a1_ref, a2_ref, w1_ref, w2_ref, o_ref, w1b_ref, w2b_ref):
    @pl.when(pl.program_id(1) == 0)
    def _():
        w1b_ref[...] = w1_ref[0].astype(BF16)
        w2b_ref[...] = w2_ref[0].astype(BF16)

    acc = jnp.dot(a1_ref[...], w1b_ref[...], preferred_element_type=F32)
    acc += jnp.dot(a2_ref[...], w2b_ref[...], preferred_element_type=F32)
    o_ref[...] = acc.astype(o_ref.dtype)


def matmul_pair(a1, a2, w_stack, layer, out_dtype, tm_pref=512, tn_pref=512):
    m, k1 = a1.shape
    assert a2.shape[1] == k1 and w_stack.shape[1] == 2 * k1
    n = w_stack.shape[2]
    tm = _tile(m, tm_pref, 8)
    tn = _tile(n, tn_pref, 128)
    return pl.pallas_call(
        _mm2_kernel,
        grid=(n // tn, m // tm),
        in_specs=[pl.BlockSpec((tm, k1), lambda j, i: (i, 0)),
                  pl.BlockSpec((tm, k1), lambda j, i: (i, 0)),
                  pl.BlockSpec((1, k1, tn), lambda j, i: (layer, 0, j)),
                  pl.BlockSpec((1, k1, tn), lambda j, i: (layer, 1, j))],
        out_specs=pl.BlockSpec((tm, tn), lambda j, i: (i, j)),
        out_shape=jax.ShapeDtypeStruct((m, n), out_dtype),
        scratch_shapes=[pltpu.VMEM((k1, tn), BF16), pltpu.VMEM((k1, tn), BF16)],
        compiler_params=_params("parallel", "arbitrary"),
        name="matmul_pair",
    )(a1, a2, w_stack, w_stack)


def _pack_halves(v):
    h = v.shape[1] // 2
    r = v.astype(BF16).astype(F32)
    lo = lax.bitcast_convert_type(r[:, :h], jnp.uint32) >> 16
    hi = lax.bitcast_convert_type(r[:, h:], jnp.uint32) & jnp.uint32(0xFFFF0000)
    return lo | hi


def _unpack_halves(w):
    left = lax.bitcast_convert_type(w << 16, F32)
    right = lax.bitcast_convert_type(w & jnp.uint32(0xFFFF0000), F32)
    return left, right


def _layer_norm(z, g, b):
    mu = jnp.mean(z, axis=-1, keepdims=True)
    zc = z - mu
    var = jnp.mean(zc * zc, axis=-1, keepdims=True)
    return zc * lax.rsqrt(var + LN_EPS) * g + b


def _add_ln_kernel(x_ref, y_ref, g_ref, b_ref, o_ref, ob_ref, ow_ref, *, alpha):
    out = _layer_norm(alpha * x_ref[...] + y_ref[...], g_ref[...], b_ref[...])
    o_ref[...] = out
    ob_ref[...] = out.astype(BF16)
    ow_ref[...] = _pack_halves(out)


def add_layer_norm(x, y, g, b, alpha, tm_pref=256):
    m, d = x.shape
    tm = _tile(m, tm_pref, 16)
    row = pl.BlockSpec((tm, d), lambda i: (i, 0))
    half = pl.BlockSpec((tm, d // 2), lambda i: (i, 0))
    vec = pl.BlockSpec((1, d), lambda i: (0, 0))
    return pl.pallas_call(
        functools.partial(_add_ln_kernel, alpha=alpha),
        grid=(m // tm,),
        in_specs=[row, row, vec, vec],
        out_specs=[row, row, half],
        out_shape=[jax.ShapeDtypeStruct((m, d), F32), jax.ShapeDtypeStruct((m, d), BF16),
                   jax.ShapeDtypeStruct((m, d // 2), jnp.uint32)],
        compiler_params=_params("parallel"),
        name="add_layer_norm",
    )(x, y, g.reshape(1, d), b.reshape(1, d))


def _lower_bounds_kernel(z_ref, o_ref):
    z = z_ref[...]
    depth = z.shape[0]
    e = jnp.exp(z - jnp.max(z, axis=0, keepdims=True))
    p = e / jnp.sum(e, axis=0, keepdims=True)
    run = jnp.zeros_like(p[0:1])
    for l in range(depth):
        run = run + p[l:l + 1]
        o_ref[l:l + 1, :] = run - p[0:1]


def lower_bounds(logits):
    return pl.pallas_call(
        _lower_bounds_kernel,
        out_shape=jax.ShapeDtypeStruct(logits.shape, F32),
        name="hgrn_lower_bounds",
    )(logits.astype(F32))


def _gla_chunk(q, k, v, g, st_ref):
    dk = q.shape[1]
    nsub = CHUNK // SUB
    row = lax.broadcasted_iota(jnp.int32, (CHUNK, CHUNK), 0)
    col = lax.broadcasted_iota(jnp.int32, (CHUNK, CHUNK), 1)
    tri = (col <= row).astype(F32)
    b = jnp.dot(tri, g, precision=HIGHEST, preferred_element_type=F32)
    b_last = b[CHUNK - 1:CHUNK, :]
    st = st_ref[...]

    qe = q * jnp.exp(b)
    o = _dot_nt(qe.astype(BF16), st.astype(BF16))

    levels = [jnp.zeros((1, dk), F32)] + [b[SUB * i - 1:SUB * i, :] for i in range(1, nsub)]
    lvl_rows = jnp.concatenate([jnp.broadcast_to(r, (SUB, dk)) for r in levels], axis=0)
    qt = q * jnp.exp(b - lvl_rows)
    row_blk = lax.broadcasted_iota(jnp.int32, (CHUNK, dk), 0) // SUB
    qhat = jnp.concatenate([jnp.where(row_blk == i, qt, 0.0) for i in range(1, nsub)], axis=1)
    khat = jnp.concatenate([k * jnp.exp(jnp.minimum(levels[i] - b, 0.0)) for i in range(1, nsub)], axis=1)
    p = _dot_nt(qhat.astype(BF16), khat.astype(BF16))
    p = jnp.where(col < (row // SUB) * SUB, p, 0.0)
    o += jnp.dot(p.astype(BF16), v.astype(BF16), preferred_element_type=F32)

    b4 = b.reshape(nsub, SUB, dk)
    q4 = q.reshape(nsub, SUB, dk)
    k4 = k.reshape(nsub, SUB, dk)
    ti = lax.broadcasted_iota(jnp.int32, (1, SUB, SUB, 1), 1)
    si = lax.broadcasted_iota(jnp.int32, (1, SUB, SUB, 1), 2)
    diff = b4[:, :, None, :] - b4[:, None, :, :]
    dec = jnp.exp(jnp.where(si <= ti, diff, NEG_BIG))
    dg = jnp.sum(q4[:, :, None, :] * k4[:, None, :, :] * dec, axis=-1)
    v4 = v.reshape(nsub, SUB, v.shape[1])
    od = jnp.einsum("its,isv->itv", dg.astype(BF16), v4.astype(BF16), preferred_element_type=F32)
    o += od.reshape(CHUNK, v.shape[1])

    kd = k * jnp.exp(b_last - b)
    st_ref[...] = st * jnp.exp(b_last) + _dot_tn(v.astype(BF16), kd.astype(BF16))
    return o


HGRN_HEADS_PER_STEP = 4
GLA_HEADS_PER_STEP = 2
CHUNK_UNROLL = 2


def _hgrn_kernel(q_ref, f_ref, i_ref, gate_ref, lb_ref, nw_ref, o_ref, st_ref, *, chunks, hp):
    @pl.when(pl.program_id(2) == 0)
    def _():
        st_ref[...] = jnp.zeros_like(st_ref)

    d = HEAD_DIM
    nw = nw_ref[...]

    def body(c, carry):
        rows = pl.ds(pl.multiple_of(c * CHUNK, CHUNK), CHUNK)
        for h in range(hp):
            cols = slice(h * d, (h + 1) * d)
            lb = lb_ref[:, cols]
            qa = q_ref[0, rows, cols]
            f = lb + (1.0 - lb) * _sigmoid(f_ref[0, rows, cols])
            o = _gla_chunk(qa * _sigmoid(qa), 1.0 - f, i_ref[0, rows, cols], jnp.log(f), st_ref.at[h])
            o = o * lax.rsqrt(jnp.mean(o * o, axis=-1, keepdims=True) + RMS_EPS) * nw
            o_ref[0, rows, cols] = (o * _sigmoid(gate_ref[0, rows, cols])).astype(o_ref.dtype)
        return carry

    lax.fori_loop(0, chunks, body, 0, unroll=CHUNK_UNROLL)


def hgrn2_mixer(proj, lb, norm_w, heads, t_pref=512):
    bsz, s, _ = proj.shape
    d = HEAD_DIM
    hp = HGRN_HEADS_PER_STEP
    assert heads % hp == 0
    groups = heads // hp
    t = _tile(s, t_pref, CHUNK)

    def col(group):
        return pl.BlockSpec((1, t, hp * d), lambda b, h, i, group=group: (b, i, group * groups + h))

    return pl.pallas_call(
        functools.partial(_hgrn_kernel, chunks=t // CHUNK, hp=hp),
        grid=(bsz, groups, s // t),
        in_specs=[col(0), col(1), col(2), col(3),
                  pl.BlockSpec((1, hp * d), lambda b, h, i: (0, h)),
                  pl.BlockSpec((1, d), lambda b, h, i: (0, 0))],
        out_specs=pl.BlockSpec((1, t, hp * d), lambda b, h, i: (b, i, h)),
        out_shape=jax.ShapeDtypeStruct((bsz, s, heads * d), BF16),
        scratch_shapes=[pltpu.VMEM((hp, d, d), F32)],
        compiler_params=_params("parallel", "parallel", "arbitrary"),
        name="hgrn2_mixer",
    )(proj, proj, proj, proj, lb.reshape(1, heads * d), norm_w.reshape(1, d))


def _gla_kernel(q_ref, k_ref, v_ref, gate_ref, a_ref, w2_ref, gb_ref, nw_ref, o_ref, st_ref,
                *, chunks, scale, hp, dk, dv):
    @pl.when(pl.program_id(2) == 0)
    def _():
        st_ref[...] = jnp.zeros_like(st_ref)

    nw = nw_ref[...]

    def body(c, carry):
        rows = pl.ds(pl.multiple_of(c * CHUNK, CHUNK), CHUNK)
        a = a_ref[0, rows, :]
        for h in range(hp):
            kc = slice(h * dk, (h + 1) * dk)
            vc = slice(h * dv, (h + 1) * dv)
            logit = jnp.dot(a, w2_ref[:, kc], precision=HIGHEST, preferred_element_type=F32) + gb_ref[:, kc]
            g = _log_sigmoid(logit) * (1.0 / GLA_TAU)
            o = _gla_chunk(q_ref[0, rows, kc] * scale, k_ref[0, rows, kc], v_ref[0, rows, vc], g, st_ref.at[h])
            o = o * lax.rsqrt(jnp.mean(o * o, axis=-1, keepdims=True) + RMS_EPS) * nw
            gate = gate_ref[0, rows, vc]
            o_ref[0, rows, vc] = (o * gate * _sigmoid(gate)).astype(o_ref.dtype)
        return carry

    lax.fori_loop(0, chunks, body, 0, unroll=CHUNK_UNROLL)


def gla_mixer(proj, a, w2, gb, norm_w, heads, dk, dv, t_pref=512):
    bsz, s, _ = proj.shape
    rank = a.shape[-1]
    t = _tile(s, t_pref, CHUNK)
    hp = GLA_HEADS_PER_STEP
    assert heads % hp == 0 and (2 * heads * dk) % (hp * dv) == 0
    groups = heads // hp
    v0 = 2 * heads * dk // (hp * dv)
    return pl.pallas_call(
        functools.partial(_gla_kernel, chunks=t // CHUNK, scale=dk ** -0.5, hp=hp, dk=dk, dv=dv),
        grid=(bsz, groups, s // t),
        in_specs=[pl.BlockSpec((1, t, hp * dk), lambda b, h, i: (b, i, h)),
                  pl.BlockSpec((1, t, hp * dk), lambda b, h, i: (b, i, groups + h)),
                  pl.BlockSpec((1, t, hp * dv), lambda b, h, i: (b, i, v0 + h)),
                  pl.BlockSpec((1, t, hp * dv), lambda b, h, i: (b, i, v0 + groups + h)),
                  pl.BlockSpec((1, t, rank), lambda b, h, i: (b, i, 0)),
                  pl.BlockSpec((rank, hp * dk), lambda b, h, i: (0, h)),
                  pl.BlockSpec((1, hp * dk), lambda b, h, i: (0, h)),
                  pl.BlockSpec((1, dv), lambda b, h, i: (0, 0))],
        out_specs=pl.BlockSpec((1, t, hp * dv), lambda b, h, i: (b, i, h)),
        out_shape=jax.ShapeDtypeStruct((bsz, s, heads * dv), BF16),
        scratch_shapes=[pltpu.VMEM((hp, dv, dk), F32)],
        compiler_params=_params("parallel", "parallel", "arbitrary"),
        name="gla_mixer",
    )(proj, proj, proj, proj, a, w2, gb.reshape(1, heads * dk), norm_w.reshape(1, dv))


def _t5_bucket_table():
    dist = WINDOW + np.arange(WINDOW)[:, None] - np.arange(2 * WINDOW)[None, :]
    max_exact = N_BUCKETS // 2
    d = np.maximum(dist, 0)
    large = max_exact + (np.log(np.maximum(d, 1).astype(np.float32) / max_exact)
                         / math.log(MAX_DISTANCE / max_exact) * (N_BUCKETS - max_exact)).astype(np.int32)
    large = np.minimum(large, N_BUCKETS - 1)
    return np.where(d < max_exact, d, large).astype(np.int32)


def _band_bias_kernel(rel_ref, bucket_ref, o_ref):
    h = pl.program_id(0)
    bucket = bucket_ref[...]
    acc = jnp.zeros(bucket.shape, F32)
    for n in range(N_BUCKETS):
        acc = jnp.where(bucket == n, rel_ref[n, h], acc)
    o_ref[0] = acc


def band_bias(rel_bias):
    heads = rel_bias.shape[1]
    bucket = jnp.asarray(_t5_bucket_table())
    return pl.pallas_call(
        _band_bias_kernel,
        grid=(heads,),
        in_specs=[pl.BlockSpec(memory_space=pltpu.SMEM),
                  pl.BlockSpec((WINDOW, 2 * WINDOW), lambda h: (0, 0))],
        out_specs=pl.BlockSpec((1, WINDOW, 2 * WINDOW), lambda h: (h, 0, 0)),
        out_shape=jax.ShapeDtypeStruct((heads, WINDOW, 2 * WINDOW), F32),
        name="t5_band_bias",
    )(rel_bias.astype(F32), bucket)


def _swa_kernel(sink_ref, q_ref, kp_ref, kc_ref, vp_ref, vc_ref, bias_ref, o_ref, *, group):
    n = pl.program_id(1)
    hkv = pl.program_id(2)
    w = WINDOW
    d = HEAD_DIM
    scale = d ** -0.5
    kb = jnp.concatenate([kp_ref[0], kc_ref[0]], axis=0).astype(BF16)
    vb = jnp.concatenate([vp_ref[0], vc_ref[0]], axis=0).astype(BF16)
    qi = lax.broadcasted_iota(jnp.int32, (w, 2 * w), 0)
    ki = lax.broadcasted_iota(jnp.int32, (w, 2 * w), 1)
    dist = w + qi - ki
    mask = (dist >= 0) & (dist < w) & ((n * w - w + ki) >= 0)
    for gi in range(group):
        q = q_ref[0, :, gi * d:(gi + 1) * d].astype(BF16)
        s = _dot_nt(q, kb) * scale + bias_ref[gi]
        s = jnp.where(mask, s, NEG_BIG)
        sink = sink_ref[hkv * group + gi]
        m = jnp.maximum(jnp.max(s, axis=-1, keepdims=True), sink)
        p = jnp.exp(s - m)
        p = p / (jnp.sum(p, axis=-1, keepdims=True) + jnp.exp(sink - m))
        o = jnp.dot(p.astype(BF16), vb, preferred_element_type=F32)
        o_ref[0, :, gi * d:(gi + 1) * d] = o.astype(o_ref.dtype)


def swa_mixer(proj, col0, heads, kv_heads, sinks, bias):
    bsz, s, _ = proj.shape
    w = WINDOW
    d = HEAD_DIM
    group = heads // kv_heads
    nb = s // w
    q0 = col0 // (group * d)
    k0 = (col0 + heads * d) // d
    v0 = k0 + kv_heads
    assert col0 % (group * d) == 0

    def prev(c0):
        return pl.BlockSpec((1, w, d), lambda b, n, h: (b, jnp.maximum(n - 1, 0), c0 + h))

    def cur(c0):
        return pl.BlockSpec((1, w, d), lambda b, n, h: (b, n, c0 + h))

    return pl.pallas_call(
        functools.partial(_swa_kernel, group=group),
        grid=(bsz, nb, kv_heads),
        in_specs=[pl.BlockSpec(memory_space=pltpu.SMEM),
                  pl.BlockSpec((1, w, group * d), lambda b, n, h: (b, n, q0 + h)),
                  prev(k0), cur(k0), prev(v0), cur(v0),
                  pl.BlockSpec((group, w, 2 * w), lambda b, n, h: (h, 0, 0))],
        out_specs=pl.BlockSpec((1, w, group * d), lambda b, n, h: (b, n, h)),
        out_shape=jax.ShapeDtypeStruct((bsz, s, heads * d), BF16),
        compiler_params=_params("parallel", "parallel", "parallel"),
        name="swa_mixer",
    )(sinks.astype(F32), proj, proj, proj, proj, proj, bias)


def _fox_cumsum_kernel(f_ref, o_ref, *, per_head):
    ls = _log_sigmoid(f_ref[0])
    r = ls.shape[0]
    li = lax.broadcasted_iota(jnp.int32, (128, 128), 0)
    lj = lax.broadcasted_iota(jnp.int32, (128, 128), 1)
    within = jnp.dot(ls, (li <= lj).astype(F32), precision=HIGHEST, preferred_element_type=F32)
    total = jnp.dot(ls, jnp.ones((128, 128), F32), precision=HIGHEST, preferred_element_type=F32)
    ri = lax.broadcasted_iota(jnp.int32, (r, r), 0)
    rj = lax.broadcasted_iota(jnp.int32, (r, r), 1)
    before = ((rj < ri) & (rj // per_head == ri // per_head)).astype(F32)
    o_ref[0] = within + jnp.dot(before, total, precision=HIGHEST, preferred_element_type=F32)


def fox_log_decay(f_logit):
    bsz, s, h = f_logit.shape
    per_head = s // 128
    f = jnp.transpose(f_logit, (0, 2, 1)).reshape(bsz, h * per_head, 128)
    c = pl.pallas_call(
        functools.partial(_fox_cumsum_kernel, per_head=per_head),
        grid=(bsz,),
        in_specs=[pl.BlockSpec((1, h * per_head, 128), lambda b: (b, 0, 0))],
        out_specs=pl.BlockSpec((1, h * per_head, 128), lambda b: (b, 0, 0)),
        out_shape=jax.ShapeDtypeStruct((bsz, h * per_head, 128), F32),
        compiler_params=_params("parallel"),
        name="fox_log_decay",
    )(f)
    return c.reshape(bsz, h, s)


FOX_HEADS_PER_STEP = 8
LOG2E = math.log2(math.e)


def _fox_kernel(q_ref, k_ref, v_ref, ck_ref, o_ref, m_ref, acc_ref, *, tq, tk, hp):
    i = pl.program_id(2)
    j = pl.program_id(3)
    d = HEAD_DIM

    @pl.when(j == 0)
    def _():
        m_ref[...] = jnp.full_like(m_ref, NEG_BIG)
        acc_ref[...] = jnp.zeros_like(acc_ref)

    def step(masked):
        if masked:
            qpos = i * tq + lax.broadcasted_iota(jnp.int32, (tq, tk), 0)
            kpos = j * tk + lax.broadcasted_iota(jnp.int32, (tq, tk), 1)
            keep = kpos <= qpos
        ones = jnp.ones((tk, d), BF16)
        for h in range(hp):
            cols = slice(h * d, (h + 1) * d)
            s = _dot_nt(q_ref[0, :, cols], k_ref[0, :, cols]) - ck_ref[0, h] * LOG2E
            if masked:
                s = jnp.where(keep, s, NEG_BIG)
            m_prev = m_ref[h]
            m_new = jnp.maximum(m_prev, jnp.max(s, axis=-1, keepdims=True))
            a = jnp.exp2(m_prev - m_new)
            p = jnp.concatenate([jnp.exp2(s[:, c * d:(c + 1) * d] - m_new) for c in range(tk // d)], axis=1)
            v1 = jnp.concatenate([v_ref[0, :, cols], ones], axis=1)
            acc_ref[h] = jnp.concatenate([a, a], axis=1) * acc_ref[h] + jnp.dot(
                p.astype(BF16), v1, preferred_element_type=F32)
            m_ref[h] = m_new

    @pl.when(j < i)
    def _():
        step(False)

    @pl.when(j == i)
    def _():
        step(True)
        for h in range(hp):
            acc = acc_ref[h]
            o_ref[0, :, h * d:(h + 1) * d] = (acc[:, :d] / acc[:, d:]).astype(o_ref.dtype)


def fox_mixer(proj, c, heads, t_pref=512):
    bsz, s, _ = proj.shape
    d = HEAD_DIM
    hp = FOX_HEADS_PER_STEP
    assert heads % hp == 0
    t = _tile(s, t_pref, 128)
    nb = s // t
    groups = heads // hp
    c_row = c.reshape(bsz, heads, 1, s)
    return pl.pallas_call(
        functools.partial(_fox_kernel, tq=t, tk=t, hp=hp),
        grid=(bsz, groups, nb, nb),
        in_specs=[pl.BlockSpec((1, t, hp * d), lambda b, h, i, j: (b, i, h)),
                  pl.BlockSpec((1, t, hp * d), lambda b, h, i, j: (b, jnp.minimum(j, i), groups + h)),
                  pl.BlockSpec((1, t, hp * d), lambda b, h, i, j: (b, jnp.minimum(j, i), 2 * groups + h)),
                  pl.BlockSpec((1, hp, 1, t), lambda b, h, i, j: (b, h, 0, jnp.minimum(j, i)))],
        out_specs=pl.BlockSpec((1, t, hp * d), lambda b, h, i, j: (b, i, h)),
        out_shape=jax.ShapeDtypeStruct((bsz, s, heads * d), BF16),
        scratch_shapes=[pltpu.VMEM((hp, t, d), F32), pltpu.VMEM((hp, t, 2 * d), F32)],
        compiler_params=_params("parallel", "parallel", "parallel", "arbitrary"),
        name="fox_mixer",
    )(proj, proj, proj, c_row)


def _router_kernel(x_ref, wt_ref, rb_ref, idx_ref, wts_ref, rank_ref, cnt_ref, carry_ref):
    e = wt_ref.shape[0]
    tm = x_ref.shape[0]
    per = e // N_GROUPS

    @pl.when(pl.program_id(0) == 0)
    def _():
        carry_ref[...] = jnp.zeros_like(carry_ref)

    logits = _dot_nt(wt_ref[...], x_ref[...], precision=HIGHEST)
    scores = _sigmoid(logits)
    biased = scores + rb_ref[...]
    g3 = biased.reshape(N_GROUPS, per, tm)
    mi = lax.broadcasted_iota(jnp.int32, (N_GROUPS, per, tm), 1)
    m1 = jnp.max(g3, axis=1, keepdims=True)
    i1 = jnp.min(jnp.where(g3 == m1, mi, per), axis=1, keepdims=True)
    m2 = jnp.max(jnp.where(mi == i1, -jnp.inf, g3), axis=1, keepdims=True)
    cur = (m1 + m2).reshape(N_GROUPS, tm)
    gi = lax.broadcasted_iota(jnp.int32, (N_GROUPS, tm), 0)
    sel = jnp.zeros((N_GROUPS, tm), jnp.bool_)
    for _ in range(TOPK_GROUPS):
        mx = jnp.max(cur, axis=0, keepdims=True)
        pick = gi == jnp.min(jnp.where(cur == mx, gi, N_GROUPS), axis=0, keepdims=True)
        sel = sel | pick
        cur = jnp.where(pick, -jnp.inf, cur)
    ok = jnp.broadcast_to(sel.reshape(N_GROUPS, 1, tm), (N_GROUPS, per, tm)).reshape(e, tm)
    cur = jnp.where(ok, biased, NEG_BIG)
    ei = lax.broadcasted_iota(jnp.int32, (e, tm), 0)
    chosen = jnp.zeros((e, tm), jnp.bool_)
    picks = []
    for k in range(TOP_K):
        mx = jnp.max(cur, axis=0, keepdims=True)
        ek = jnp.min(jnp.where(cur == mx, ei, e), axis=0, keepdims=True)
        pick = ei == ek
        chosen = chosen | pick
        cur = jnp.where(pick, -jnp.inf, cur)
        picks.append((ek, pick, jnp.sum(jnp.where(pick, scores, 0.0), axis=0, keepdims=True)))
    denom = sum(w for _, _, w in picks)
    si = lax.broadcasted_iota(jnp.int32, (tm, tm), 0)
    ti = lax.broadcasted_iota(jnp.int32, (tm, tm), 1)
    before = jnp.dot(chosen.astype(BF16), (si < ti).astype(BF16), preferred_element_type=F32)
    before = before + carry_ref[...]
    for k, (ek, pick, w) in enumerate(picks):
        idx_ref[k:k + 1, :] = ek
        wts_ref[k:k + 1, :] = w / denom * ROUTED_SCALE
        rank_ref[k:k + 1, :] = jnp.sum(jnp.where(pick, before, 0.0), axis=0, keepdims=True).astype(jnp.int32)
    carry_ref[...] += jnp.sum(chosen.astype(F32), axis=1, keepdims=True)
    cnt_ref[...] = carry_ref[...].astype(jnp.int32)


def router(x, w_router, router_bias, tm_pref=512):
    m, d = x.shape
    e = w_router.shape[1]
    tm = _tile(m, tm_pref, 128)
    pick = pl.BlockSpec((TOP_K, tm), lambda i: (0, i))
    return pl.pallas_call(
        _router_kernel,
        grid=(m // tm,),
        in_specs=[pl.BlockSpec((tm, d), lambda i: (i, 0)),
                  pl.BlockSpec((e, d), lambda i: (0, 0)),
                  pl.BlockSpec((e, 1), lambda i: (0, 0))],
        out_specs=[pick, pick, pick, pl.BlockSpec((e, 1), lambda i: (0, 0))],
        out_shape=[jax.ShapeDtypeStruct((TOP_K, m), jnp.int32), jax.ShapeDtypeStruct((TOP_K, m), F32),
                   jax.ShapeDtypeStruct((TOP_K, m), jnp.int32), jax.ShapeDtypeStruct((e, 1), jnp.int32)],
        scratch_shapes=[pltpu.VMEM((e, 1), F32)],
        compiler_params=_params("arbitrary"),
        name="moe_router",
    )(x, jnp.transpose(w_router).astype(F32), router_bias.reshape(e, 1).astype(F32))


DISPATCH_UNROLL = 8


def _dispatch_kernel(dst_ref, x_ref, xs_hbm, sem, *, tm):
    for k in range(TOP_K):
        def body(t, carry, k=k):
            pltpu.make_async_copy(x_ref.at[pl.ds(t, 1)], xs_hbm.at[pl.ds(dst_ref[k, t], 1)], sem).start()
            return carry
        lax.fori_loop(0, tm, body, 0, unroll=DISPATCH_UNROLL)
    for k in range(TOP_K):
        pltpu.make_async_copy(x_ref, xs_hbm.at[pl.ds(0, tm)], sem).wait()


def moe_dispatch(xw, dst, tm):
    m, h = xw.shape
    return pl.pallas_call(
        functools.partial(_dispatch_kernel, tm=tm),
        grid=(m // tm,),
        in_specs=[pl.BlockSpec((TOP_K, tm), lambda i: (0, i), memory_space=pltpu.SMEM),
                  pl.BlockSpec((tm, h), lambda i: (i, 0))],
        out_specs=pl.BlockSpec(memory_space=pl.ANY),
        out_shape=jax.ShapeDtypeStruct((TOP_K * m, h), jnp.uint32),
        scratch_shapes=[pltpu.SemaphoreType.DMA(())],
        compiler_params=_params("arbitrary"),
        name="moe_dispatch",
    )(dst, xw)


def _group_metadata(counts, n_rows, t):
    e = counts.shape[0]
    n_tiles = n_rows // t
    n_items = n_tiles + e - 1
    ends = jnp.cumsum(counts)
    offs = ends - counts
    first_tile = offs // t
    tiles_of = jnp.where(counts > 0, (ends - 1) // t - first_tile + 1, 0)
    item_end = jnp.cumsum(tiles_of)
    used = item_end[-1]
    ids = jnp.minimum(jnp.arange(n_items, dtype=jnp.int32), used - 1)
    exp = jnp.minimum(jnp.sum(item_end[None, :] <= ids[:, None], axis=1), e - 1).astype(jnp.int32)
    tile = (first_tile[exp] + ids - (item_end[exp] - tiles_of[exp])).astype(jnp.int32)
    lo = jnp.clip(offs[exp] - tile * t, 0, t).astype(jnp.int32)
    hi = jnp.clip(ends[exp] - tile * t, 0, t).astype(jnp.int32)
    prev_tile = jnp.concatenate([jnp.full((1,), -1, jnp.int32), tile[:-1]])
    prev_exp = jnp.concatenate([jnp.full((1,), -1, jnp.int32), exp[:-1]])
    return (tile, exp, lo, hi, (tile != prev_tile).astype(jnp.int32), (exp != prev_exp).astype(jnp.int32),
            used.reshape(1).astype(jnp.int32)), offs


def _group_ffn_kernel(tile_ref, exp_ref, lo_ref, hi_ref, first_ref, newexp_ref, used_ref,
                      x_ref, wg_ref, wu_ref, wd_ref, o_ref, wgb, wub, wdb):
    i = pl.program_id(0)
    t, h = x_ref.shape

    @pl.when(i < used_ref[0])
    def _():
        @pl.when(newexp_ref[i] == 1)
        def _():
            wgb[...] = wg_ref[0, 0].astype(BF16)
            wub[...] = wu_ref[0, 0].astype(BF16)
            wdb[...] = wd_ref[0, 0].astype(BF16)

        def ffn(mine):
            left, right = _unpack_halves(x_ref[...])
            left = left.astype(BF16)
            right = right.astype(BF16)
            g = (jnp.dot(left, wgb[:h], preferred_element_type=F32)
                 + jnp.dot(right, wgb[h:], preferred_element_type=F32))
            u = (jnp.dot(left, wub[:h], preferred_element_type=F32)
                 + jnp.dot(right, wub[h:], preferred_element_type=F32))
            a = g * _sigmoid(g) * u
            if mine is not None:
                a = jnp.where(mine, a, 0.0)
            return _pack_halves(jnp.dot(a.astype(BF16), wdb[...], preferred_element_type=F32))

        whole = (lo_ref[i] == 0) & (hi_ref[i] == t)

        @pl.when(whole)
        def _():
            o_ref[...] = ffn(None)

        @pl.when(jnp.logical_not(whole))
        def _():
            row = lax.broadcasted_iota(jnp.int32, (t, 1), 0)
            mine = (row >= lo_ref[i]) & (row < hi_ref[i])
            y = ffn(mine)

            @pl.when(first_ref[i] == 1)
            def _():
                o_ref[...] = y

            @pl.when(first_ref[i] == 0)
            def _():
                o_ref[...] = jnp.where(mine, y, o_ref[...])


def moe_group_ffn(xs, meta, wg, wu, wd, layer, t):
    r, h = xs.shape
    _, e, d, f = wg.shape
    n_items = r // t + e - 1
    grid_spec = pltpu.PrefetchScalarGridSpec(
        num_scalar_prefetch=7,
        grid=(n_items,),
        in_specs=[pl.BlockSpec((t, h), lambda i, tile, *_: (tile[i], 0)),
                  pl.BlockSpec((1, 1, d, f), lambda i, tile, exp, *_: (layer, exp[i], 0, 0)),
                  pl.BlockSpec((1, 1, d, f), lambda i, tile, exp, *_: (layer, exp[i], 0, 0)),
                  pl.BlockSpec((1, 1, f, d), lambda i, tile, exp, *_: (layer, exp[i], 0, 0))],
        out_specs=pl.BlockSpec((t, h), lambda i, tile, *_: (tile[i], 0)),
        scratch_shapes=[pltpu.VMEM((d, f), BF16), pltpu.VMEM((d, f), BF16), pltpu.VMEM((f, d), BF16)],
    )
    return pl.pallas_call(
        _group_ffn_kernel,
        grid_spec=grid_spec,
        out_shape=jax.ShapeDtypeStruct((r, h), jnp.uint32),
        compiler_params=_params("arbitrary"),
        name="moe_group_ffn",
    )(*meta, xs, wg, wu, wd)


def _combine_kernel(dst_ref, nxt_ref, wts_ref, x_ref, xb_ref, sg_ref, su_ref, sd_ref, g_ref, b_ref, ys_hbm,
                    o_ref, ob_ref, buf, sem, *, alpha, tm, steps):
    i = pl.program_id(0)
    slot = i % 2

    def gather(idx_ref, s):
        for k in range(TOP_K):
            def body(t, carry, k=k):
                pltpu.make_async_copy(ys_hbm.at[pl.ds(idx_ref[k, t], 1)], buf.at[s, k, pl.ds(t, 1)],
                                      sem.at[s]).start()
                return carry
            lax.fori_loop(0, tm, body, 0, unroll=DISPATCH_UNROLL)

    @pl.when(i == 0)
    def _():
        gather(dst_ref, 0)

    @pl.when(i + 1 < steps)
    def _():
        gather(nxt_ref, 1 - slot)

    for k in range(TOP_K):
        pltpu.make_async_copy(ys_hbm.at[pl.ds(0, tm)], buf.at[slot, k], sem.at[slot]).wait()

    h = buf.shape[-1]
    acc_l = jnp.zeros((tm, h), F32)
    acc_r = jnp.zeros((tm, h), F32)
    wts = jnp.transpose(wts_ref[...])
    for k in range(TOP_K):
        left, right = _unpack_halves(buf[slot, k])
        c = wts[:, k:k + 1]
        acc_l += c * left
        acc_r += c * right
    xb = xb_ref[...]
    hs = jnp.dot(xb, sg_ref[...], preferred_element_type=F32)
    hs = hs * _sigmoid(hs) * jnp.dot(xb, su_ref[...], preferred_element_type=F32)
    y = jnp.concatenate([acc_l, acc_r], axis=1) + jnp.dot(hs.astype(BF16), sd_ref[...], preferred_element_type=F32)
    out = _layer_norm(alpha * x_ref[...] + y, g_ref[...], b_ref[...])
    o_ref[...] = out
    ob_ref[...] = out.astype(BF16)


def moe_combine_ln(ys, dst_tiles, wts, x, xb, sg, su, sd, g, b, alpha, tm):
    m, d = x.shape
    h = ys.shape[1]
    fs = sg.shape[1]
    steps = m // tm
    row = pl.BlockSpec((tm, d), lambda i: (i, 0))
    vec = pl.BlockSpec((1, d), lambda i: (0, 0))
    return pl.pallas_call(
        functools.partial(_combine_kernel, alpha=alpha, tm=tm, steps=steps),
        grid=(steps,),
        in_specs=[pl.BlockSpec((TOP_K, tm), lambda i: (0, i), memory_space=pltpu.SMEM),
                  pl.BlockSpec((TOP_K, tm), lambda i: (0, jnp.minimum(i + 1, steps - 1)), memory_space=pltpu.SMEM),
                  pl.BlockSpec((TOP_K, tm), lambda i: (0, i)),
                  row, row,
                  pl.BlockSpec((d, fs), lambda i: (0, 0)),
                  pl.BlockSpec((d, fs), lambda i: (0, 0)),
                  pl.BlockSpec((fs, d), lambda i: (0, 0)),
                  vec, vec,
                  pl.BlockSpec(memory_space=pl.ANY)],
        out_specs=[row, row],
        out_shape=[jax.ShapeDtypeStruct((m, d), F32), jax.ShapeDtypeStruct((m, d), BF16)],
        scratch_shapes=[pltpu.VMEM((2, TOP_K, tm, h), jnp.uint32), pltpu.SemaphoreType.DMA((2,))],
        compiler_params=_params("arbitrary"),
        name="moe_combine_ln",
    )(dst_tiles, dst_tiles, wts, x, xb, sg, su, sd, g.reshape(1, d), b.reshape(1, d), ys)


def moe_block(xf, xb, xw, w_router, router_bias, wg, wu, wd, layer, sg, su, sd, g, b, alpha,
              t_rows=256, tm_dispatch=512, tm_combine=128):
    m = xf.shape[0]
    e = w_router.shape[1]
    t_rows = _tile(TOP_K * m, t_rows, 8)
    tm_dispatch = _tile(m, tm_dispatch, 128)
    tm_combine = _tile(m, tm_combine, 128)
    idx, wts, rank, counts = router(xf, w_router, router_bias)
    meta, offs = _group_metadata(counts.reshape(-1), TOP_K * m, t_rows)
    start = jnp.sum(jnp.where(idx[:, None, :] == jnp.arange(e, dtype=jnp.int32)[None, :, None],
                              offs[None, :, None], 0), axis=1)
    dst = (start + rank).astype(jnp.int32)
    xs = moe_dispatch(xw, dst, tm_dispatch)
    ys = moe_group_ffn(xs, meta, wg, wu, wd, layer, t_rows)
    return moe_combine_ln(ys, dst, wts, xf, xb, sg.astype(BF16), su.astype(BF16), sd.astype(BF16),
                          g, b, alpha, tm_combine)


def kernel(x, w_in_ab, b_in_ab, w_in_cd, b_in_cd, hgrn_lb_logits, hgrn_norm, sinks, rel_bias, gla_w2, gla_b,
           gla_norm, w_out, ln_g, ln_b, w_router, router_bias, w_exp_gate, w_exp_up, w_exp_down, w_sh_gate,
           w_sh_up, w_sh_down):
    bsz, s, d = x.shape
    m = bsz * s
    depth = w_out.shape[0]
    alpha = (2.0 * depth) ** 0.25
    half = d // 2
    a_heads = half // HEAD_DIM
    b_heads = half // HEAD_DIM
    b_kv_heads = b_heads // 4
    c_dk = (d // 4) // C_HEADS
    c_dv = half // C_HEADS
    d_heads = half // HEAD_DIM
    rank = gla_w2.shape[1]
    c_cols = 2 * C_HEADS * c_dk + 2 * C_HEADS * c_dv
    d_cols = 3 * d_heads * HEAD_DIM

    lbs = lower_bounds(hgrn_lb_logits)
    bias = band_bias(rel_bias)

    xf = x.reshape(m, d).astype(F32)
    xb = xf.astype(BF16)
    for l in range(depth):
        j = l // 2
        if l % 2 == 0:
            proj = matmul_bias_layer(xb, w_in_ab, j, w_in_ab.shape[2], b_in_ab[j], F32).reshape(bsz, s, -1)
            o1 = hgrn2_mixer(proj, lbs[l], hgrn_norm[j], a_heads)
            o2 = swa_mixer(proj, 4 * a_heads * HEAD_DIM, b_heads, b_kv_heads, sinks[j], bias)
        else:
            w_in = w_in_cd[j]
            b_in = b_in_cd[j]
            d0 = c_cols + rank
            f0 = d0 + d_cols
            proj_c = matmul_bias_layer(xb, w_in_cd, j, c_cols, b_in[:c_cols], F32).reshape(bsz, s, -1)
            qfold = jnp.where(jnp.arange(d_cols) < d_heads * HEAD_DIM, LOG2E * HEAD_DIM ** -0.5, 1.0).astype(F32)
            proj_d = matmul_bias(xb, (w_in[:, d0:f0] * qfold).astype(BF16), b_in[d0:f0] * qfold,
                                 BF16).reshape(bsz, s, -1)
            w_small = jnp.concatenate([w_in[:, c_cols:d0], w_in[:, f0:]], axis=1).astype(BF16)
            b_small = jnp.concatenate([b_in[c_cols:d0], b_in[f0:]])
            small = matmul_bias(xb, w_small, b_small, F32).reshape(bsz, s, -1)
            o1 = gla_mixer(proj_c, small[..., :rank], gla_w2[j].astype(F32), gla_b[j].astype(F32), gla_norm[j],
                           C_HEADS, c_dk, c_dv)
            o2 = fox_mixer(proj_d, fox_log_decay(small[..., rank:]), d_heads)
        mix = matmul_pair(o1.reshape(m, half), o2.reshape(m, half), w_out, l, F32)
        xf, xb, xw = add_layer_norm(xf, mix, ln_g[l, 0], ln_b[l, 0], alpha)
        xf, xb = moe_block(xf, xb, xw, w_router[l], router_bias[l], w_exp_gate, w_exp_up, w_exp_down, l,
                           w_sh_gate[l], w_sh_up[l], w_sh_down[l], ln_g[l, 1], ln_b[l, 1], alpha)
    return xf.reshape(bsz, s, d).astype(x.dtype)
```

```python
import functools
import math

import numpy as np
import jax
import jax.numpy as jnp
from jax import lax
from jax.experimental import pallas as pl
from jax.experimental.pallas import tpu as pltpu

F32 = jnp.float32
BF16 = jnp.bfloat16
HIGHEST = lax.Precision.HIGHEST

HEAD_DIM = 128
WINDOW = 128
C_HEADS = 4
GLA_TAU = 16.0
CHUNK = 64
SUB = 16
N_BUCKETS = 32
MAX_DISTANCE = 128
TOP_K = 8
N_GROUPS = 8
TOPK_GROUPS = 4
ROUTED_SCALE = 2.5
LN_EPS = 1e-5
RMS_EPS = 1e-6
NEG_BIG = -1e30

VMEM_LIMIT_BYTES = 52 * 1024 * 1024


def _params(*sem):
    return pltpu.CompilerParams(dimension_semantics=sem, vmem_limit_bytes=VMEM_LIMIT_BYTES)


def _tile(n, pref, align):
    t = min(pref, n)
    t -= t % align
    while t >= align:
        if n % t == 0:
            return t
        t -= align
    return n


def _sigmoid(x):
    return 1.0 / (1.0 + jnp.exp(-x))


def _log_sigmoid(x):
    return jnp.minimum(x, 0.0) - jnp.log(1.0 + jnp.exp(-jnp.abs(x)))


def _dot_nt(a, b, **kw):
    return lax.dot_general(a, b, (((1,), (1,)), ((), ())), preferred_element_type=F32, **kw)


def _dot_tn(a, b, **kw):
    return lax.dot_general(a, b, (((0,), (0,)), ((), ())), preferred_element_type=F32, **kw)


def _mm_kernel(x_ref, w_ref, b_ref, o_ref):
    acc = jnp.dot(x_ref[...], w_ref[...], preferred_element_type=F32)
    o_ref[...] = (acc + b_ref[...]).astype(o_ref.dtype)


def matmul_bias(x, w, b, out_dtype, tm_pref=512, tn_pref=1024):
    m, k = x.shape
    n = w.shape[1]
    tm = _tile(m, tm_pref, 8)
    tn = _tile(n, tn_pref, 128)
    return pl.pallas_call(
        _mm_kernel,
        grid=(n // tn, m // tm),
        in_specs=[pl.BlockSpec((tm, k), lambda j, i: (i, 0)),
                  pl.BlockSpec((k, tn), lambda j, i: (0, j)),
                  pl.BlockSpec((1, tn), lambda j, i: (0, j))],
        out_specs=pl.BlockSpec((tm, tn), lambda j, i: (i, j)),
        out_shape=jax.ShapeDtypeStruct((m, n), out_dtype),
        compiler_params=_params("parallel", "parallel"),
        name="matmul_bias",
    )(x, w, b.reshape(1, n).astype(F32))


def _mm_layer_kernel(x_ref, w_ref, b_ref, o_ref, wb_ref):
    @pl.when(pl.program_id(1) == 0)
    def _():
        wb_ref[...] = w_ref[0].astype(BF16)

    acc = jnp.dot(x_ref[...], wb_ref[...], preferred_element_type=F32)
    o_ref[...] = (acc + b_ref[...]).astype(o_ref.dtype)


def matmul_bias_layer(x, w_stack, layer, n_cols, b, out_dtype, tm_pref=512, tn_pref=1024):
    m, k = x.shape
    tm = _tile(m, tm_pref, 8)
    tn = _tile(n_cols, tn_pref, 128)
    return pl.pallas_call(
        _mm_layer_kernel,
        grid=(n_cols // tn, m // tm),
        in_specs=[pl.BlockSpec((tm, k), lambda j, i: (i, 0)),
                  pl.BlockSpec((1, k, tn), lambda j, i: (layer, 0, j), pipeline_mode=pl.Buffered(1)),
                  pl.BlockSpec((1, tn), lambda j, i: (0, j))],
        out_specs=pl.BlockSpec((tm, tn), lambda j, i: (i, j)),
        out_shape=jax.ShapeDtypeStruct((m, n_cols), out_dtype),
        scratch_shapes=[pltpu.VMEM((k, tn), BF16)],
        compiler_params=_params("parallel", "arbitrary"),
        name="matmul_bias_layer",
    )(x, w_stack, b.reshape(1, n_cols).astype(F32))


def _mm2_kernel(a1_ref, a2_ref, w1_ref, w2_ref, o_ref, w1b_ref, w2b_ref):
    @pl.when(pl.program_id(1) == 0)
    def _():
        w1b_ref[...] = w1_ref[0].astype(BF16)
        w2b_ref[...] = w2_ref[0].astype(BF16)

    acc = jnp.dot(a1_ref[...], w1b_ref[...], preferred_element_type=F32)
    acc += jnp.dot(a2_ref[...], w2b_ref[...], preferred_element_type=F32)
    o_ref[...] = acc.astype(o_ref.dtype)


def matmul_pair(a1, a2, w_stack, layer, out_dtype, tm_pref=512, tn_pref=1024):
    m, k1 = a1.shape
    assert a2.shape[1] == k1 and w_stack.shape[1] == 2 * k1
    n = w_stack.shape[2]
    tm = _tile(m, tm_pref, 8)
    tn = _tile(n, tn_pref, 128)
    return pl.pallas_call(
        _mm2_kernel,
        grid=(n // tn, m // tm),
        in_specs=[pl.BlockSpec((tm, k1), lambda j, i: (i, 0)),
                  pl.BlockSpec((tm, k1), lambda j, i: (i, 0)),
                  pl.BlockSpec((1, k1, tn), lambda j, i: (layer, 0, j), pipeline_mode=pl.Buffered(1)),
                  pl.BlockSpec((1, k1, tn), lambda j, i: (layer, 1, j), pipeline_mode=pl.Buffered(1))],
        out_specs=pl.BlockSpec((tm, tn), lambda j, i: (i, j)),
        out_shape=jax.ShapeDtypeStruct((m, n), out_dtype),
        scratch_shapes=[pltpu.VMEM((k1, tn), BF16), pltpu.VMEM((k1, tn), BF16)],
        compiler_params=_params("parallel", "arbitrary"),
        name="matmul_pair",
    )(a1, a2, w_stack, w_stack)


def _pack_halves(v):
    h = v.shape[1] // 2
    r = v.astype(BF16).astype(F32)
    lo = lax.bitcast_convert_type(r[:, :h], jnp.uint32) >> 16
    hi = lax.bitcast_convert_type(r[:, h:], jnp.uint32) & jnp.uint32(0xFFFF0000)
    return lo | hi


def _unpack_halves(w):
    left = lax.bitcast_convert_type(w << 16, F32)
    right = lax.bitcast_convert_type(w & jnp.uint32(0xFFFF0000), F32)
    return left, right


def _layer_norm(z, g, b):
    mu = jnp.mean(z, axis=-1, keepdims=True)
    zc = z - mu
    var = jnp.mean(zc * zc, axis=-1, keepdims=True)
    return zc * lax.rsqrt(var + LN_EPS) * g + b


def _add_ln_kernel(x_ref, y_ref, g_ref, b_ref, o_ref, ob_ref, ow_ref, *, alpha):
    out = _layer_norm(alpha * x_ref[...] + y_ref[...], g_ref[...], b_ref[...])
    o_ref[...] = out
    ob_ref[...] = out.astype(BF16)
    ow_ref[...] = _pack_halves(out)


def add_layer_norm(x, y, g, b, alpha, tm_pref=256):
    m, d = x.shape
    tm = _tile(m, tm_pref, 16)
    row = pl.BlockSpec((tm, d), lambda i: (i, 0))
    half = pl.BlockSpec((tm, d // 2), lambda i: (i, 0))
    vec = pl.BlockSpec((1, d), lambda i: (0, 0))
    return pl.pallas_call(
        functools.partial(_add_ln_kernel, alpha=alpha),
        grid=(m // tm,),
        in_specs=[row, row, vec, vec],
        out_specs=[row, row, half],
        out_shape=[jax.ShapeDtypeStruct((m, d), F32), jax.ShapeDtypeStruct((m, d), BF16),
                   jax.ShapeDtypeStruct((m, d // 2), jnp.uint32)],
        compiler_params=_params("parallel"),
        name="add_layer_norm",
    )(x, y, g.reshape(1, d), b.reshape(1, d))


def _lower_bounds_kernel(z_ref, o_ref):
    z = z_ref[...]
    depth = z.shape[0]
    e = jnp.exp(z - jnp.max(z, axis=0, keepdims=True))
    p = e / jnp.sum(e, axis=0, keepdims=True)
    run = jnp.zeros_like(p[0:1])
    for l in range(depth):
        run = run + p[l:l + 1]
        o_ref[l:l + 1, :] = run - p[0:1]


def lower_bounds(logits):
    return pl.pallas_call(
        _lower_bounds_kernel,
        out_shape=jax.ShapeDtypeStruct(logits.shape, F32),
        name="hgrn_lower_bounds",
    )(logits.astype(F32))


def _gla_chunk(q, k, v, g, st_ref):
    dk = q.shape[1]
    nsub = CHUNK // SUB
    row = lax.broadcasted_iota(jnp.int32, (CHUNK, CHUNK), 0)
    col = lax.broadcasted_iota(jnp.int32, (CHUNK, CHUNK), 1)
    tri = (col <= row).astype(F32)
    b = jnp.dot(tri, g, precision=HIGHEST, preferred_element_type=F32)
    b_last = b[CHUNK - 1:CHUNK, :]
    st = st_ref[...]

    qe = q * jnp.exp(b)
    o = _dot_nt(qe.astype(BF16), st.astype(BF16))

    levels = [jnp.zeros((1, dk), F32)] + [b[SUB * i - 1:SUB * i, :] for i in range(1, nsub)]
    lvl_rows = jnp.concatenate([jnp.broadcast_to(r, (SUB, dk)) for r in levels], axis=0)
    qt = q * jnp.exp(b - lvl_rows)
    row_blk = lax.broadcasted_iota(jnp.int32, (CHUNK, dk), 0) // SUB
    qhat = jnp.concatenate([jnp.where(row_blk == i, qt, 0.0) for i in range(1, nsub)], axis=1)
    khat = jnp.concatenate([k * jnp.exp(jnp.minimum(levels[i] - b, 0.0)) for i in range(1, nsub)], axis=1)
    p = _dot_nt(qhat.astype(BF16), khat.astype(BF16))
    p = jnp.where(col < (row // SUB) * SUB, p, 0.0)
    o += jnp.dot(p.astype(BF16), v.astype(BF16), preferred_element_type=F32)

    b4 = b.reshape(nsub, SUB, dk)
    q4 = q.reshape(nsub, SUB, dk)
    k4 = k.reshape(nsub, SUB, dk)
    ti = lax.broadcasted_iota(jnp.int32, (1, SUB, SUB, 1), 1)
    si = lax.broadcasted_iota(jnp.int32, (1, SUB, SUB, 1), 2)
    diff = b4[:, :, None, :] - b4[:, None, :, :]
    dec = jnp.exp(jnp.where(si <= ti, diff, NEG_BIG))
    dg = jnp.sum(q4[:, :, None, :] * k4[:, None, :, :] * dec, axis=-1)
    v4 = v.reshape(nsub, SUB, v.shape[1])
    od = jnp.einsum("its,isv->itv", dg.astype(BF16), v4.astype(BF16), preferred_element_type=F32)
    o += od.reshape(CHUNK, v.shape[1])

    kd = k * jnp.exp(b_last - b)
    st_ref[...] = st * jnp.exp(b_last) + _dot_tn(v.astype(BF16), kd.astype(BF16))
    return o


HGRN_HEADS_PER_STEP = 4
GLA_HEADS_PER_STEP = 2
CHUNK_UNROLL = 2


def _hgrn_kernel(q_ref, f_ref, i_ref, gate_ref, lb_ref, nw_ref, o_ref, st_ref, *, chunks, hp):
    @pl.when(pl.program_id(2) == 0)
    def _():
        st_ref[...] = jnp.zeros_like(st_ref)

    d = HEAD_DIM
    nw = nw_ref[...]

    def body(c, carry):
        rows = pl.ds(pl.multiple_of(c * CHUNK, CHUNK), CHUNK)
        for h in range(hp):
            cols = slice(h * d, (h + 1) * d)
            lb = lb_ref[:, cols]
            qa = q_ref[0, rows, cols]
            f = lb + (1.0 - lb) * _sigmoid(f_ref[0, rows, cols])
            o = _gla_chunk(qa * _sigmoid(qa), 1.0 - f, i_ref[0, rows, cols], jnp.log(f), st_ref.at[h])
            o = o * lax.rsqrt(jnp.mean(o * o, axis=-1, keepdims=True) + RMS_EPS) * nw
            o_ref[0, rows, cols] = (o * _sigmoid(gate_ref[0, rows, cols])).astype(o_ref.dtype)
        return carry

    lax.fori_loop(0, chunks, body, 0, unroll=CHUNK_UNROLL)


def hgrn2_mixer(proj, lb, norm_w, heads, t_pref=512):
    bsz, s, _ = proj.shape
    d = HEAD_DIM
    hp = HGRN_HEADS_PER_STEP
    assert heads % hp == 0
    groups = heads // hp
    t = _tile(s, t_pref, CHUNK)

    def col(group):
        return pl.BlockSpec((1, t, hp * d), lambda b, h, i, group=group: (b, i, group * groups + h))

    return pl.pallas_call(
        functools.partial(_hgrn_kernel, chunks=t // CHUNK, hp=hp),
        grid=(bsz, groups, s // t),
        in_specs=[col(0), col(1), col(2), col(3),
                  pl.BlockSpec((1, hp * d), lambda b, h, i: (0, h)),
                  pl.BlockSpec((1, d), lambda b, h, i: (0, 0))],
        out_specs=pl.BlockSpec((1, t, hp * d), lambda b, h, i: (b, i, h)),
        out_shape=jax.ShapeDtypeStruct((bsz, s, heads * d), BF16),
        scratch_shapes=[pltpu.VMEM((hp, d, d), F32)],
        compiler_params=_params("parallel", "parallel", "arbitrary"),
        name="hgrn2_mixer",
    )(proj, proj, proj, proj, lb.reshape(1, heads * d), norm_w.reshape(1, d))


def _gla_kernel(q_ref, k_ref, v_ref, gate_ref, a_ref, w2_ref, gb_ref, nw_ref, o_ref, st_ref,
                *, chunks, scale, hp, dk, dv):
    @pl.when(pl.program_id(2) == 0)
    def _():
        st_ref[...] = jnp.zeros_like(st_ref)

    nw = nw_ref[...]

    def body(c, carry):
        rows = pl.ds(pl.multiple_of(c * CHUNK, CHUNK), CHUNK)
        a = a_ref[0, rows, :]
        for h in range(hp):
            kc = slice(h * dk, (h + 1) * dk)
            vc = slice(h * dv, (h + 1) * dv)
            logit = jnp.dot(a, w2_ref[:, kc], precision=HIGHEST, preferred_element_type=F32) + gb_ref[:, kc]
            g = _log_sigmoid(logit) * (1.0 / GLA_TAU)
            o = _gla_chunk(q_ref[0, rows, kc] * scale, k_ref[0, rows, kc], v_ref[0, rows, vc], g, st_ref.at[h])
            o = o * lax.rsqrt(jnp.mean(o * o, axis=-1, keepdims=True) + RMS_EPS) * nw
            gate = gate_ref[0, rows, vc]
            o_ref[0, rows, vc] = (o * gate * _sigmoid(gate)).astype(o_ref.dtype)
        return carry

    lax.fori_loop(0, chunks, body, 0, unroll=CHUNK_UNROLL)


def gla_mixer(proj, a, w2, gb, norm_w, heads, dk, dv, t_pref=512):
    bsz, s, _ = proj.shape
    rank = a.shape[-1]
    t = _tile(s, t_pref, CHUNK)
    hp = GLA_HEADS_PER_STEP
    assert heads % hp == 0 and (2 * heads * dk) % (hp * dv) == 0
    groups = heads // hp
    v0 = 2 * heads * dk // (hp * dv)
    return pl.pallas_call(
        functools.partial(_gla_kernel, chunks=t // CHUNK, scale=dk ** -0.5, hp=hp, dk=dk, dv=dv),
        grid=(bsz, groups, s // t),
        in_specs=[pl.BlockSpec((1, t, hp * dk), lambda b, h, i: (b, i, h)),
                  pl.BlockSpec((1, t, hp * dk), lambda b, h, i: (b, i, groups + h)),
                  pl.BlockSpec((1, t, hp * dv), lambda b, h, i: (b, i, v0 + h)),
                  pl.BlockSpec((1, t, hp * dv), lambda b, h, i: (b, i, v0 + groups + h)),
                  pl.BlockSpec((1, t, rank), lambda b, h, i: (b, i, 0)),
                  pl.BlockSpec((rank, hp * dk), lambda b, h, i: (0, h)),
                  pl.BlockSpec((1, hp * dk), lambda b, h, i: (0, h)),
                  pl.BlockSpec((1, dv), lambda b, h, i: (0, 0))],
        out_specs=pl.BlockSpec((1, t, hp * dv), lambda b, h, i: (b, i, h)),
        out_shape=jax.ShapeDtypeStruct((bsz, s, heads * dv), BF16),
        scratch_shapes=[pltpu.VMEM((hp, dv, dk), F32)],
        compiler_params=_params("parallel", "parallel", "arbitrary"),
        name="gla_mixer",
    )(proj, proj, proj, proj, a, w2, gb.reshape(1, heads * dk), norm_w.reshape(1, dv))


def _t5_bucket_table():
    dist = WINDOW + np.arange(WINDOW)[:, None] - np.arange(2 * WINDOW)[None, :]
    max_exact = N_BUCKETS // 2
    d = np.maximum(dist, 0)
    large = max_exact + (np.log(np.maximum(d, 1).astype(np.float32) / max_exact)
                         / math.log(MAX_DISTANCE / max_exact) * (N_BUCKETS - max_exact)).astype(np.int32)
    large = np.minimum(large, N_BUCKETS - 1)
    return np.where(d < max_exact, d, large).astype(np.int32)


def _band_bias_kernel(rel_ref, bucket_ref, o_ref):
    h = pl.program_id(0)
    bucket = bucket_ref[...]
    acc = jnp.zeros(bucket.shape, F32)
    for n in range(N_BUCKETS):
        acc = jnp.where(bucket == n, rel_ref[n, h], acc)
    o_ref[0] = acc


def band_bias(rel_bias):
    heads = rel_bias.shape[1]
    bucket = jnp.asarray(_t5_bucket_table())
    return pl.pallas_call(
        _band_bias_kernel,
        grid=(heads,),
        in_specs=[pl.BlockSpec(memory_space=pltpu.SMEM),
                  pl.BlockSpec((WINDOW, 2 * WINDOW), lambda h: (0, 0))],
        out_specs=pl.BlockSpec((1, WINDOW, 2 * WINDOW), lambda h: (h, 0, 0)),
        out_shape=jax.ShapeDtypeStruct((heads, WINDOW, 2 * WINDOW), F32),
        name="t5_band_bias",
    )(rel_bias.astype(F32), bucket)


def _swa_kernel(sink_ref, q_ref, kp_ref, kc_ref, vp_ref, vc_ref, bias_ref, o_ref, *, group):
    n = pl.program_id(1)
    hkv = pl.program_id(2)
    w = WINDOW
    d = HEAD_DIM
    scale = d ** -0.5
    kb = jnp.concatenate([kp_ref[0], kc_ref[0]], axis=0).astype(BF16)
    vb = jnp.concatenate([vp_ref[0], vc_ref[0]], axis=0).astype(BF16)
    vb1 = jnp.concatenate([vb, jnp.ones((2 * w, d), BF16)], axis=1)
    qi = lax.broadcasted_iota(jnp.int32, (w, 2 * w), 0)
    ki = lax.broadcasted_iota(jnp.int32, (w, 2 * w), 1)
    dist = w + qi - ki
    mask = (dist >= 0) & (dist < w) & ((n * w - w + ki) >= 0)
    for gi in range(group):
        q = q_ref[0, :, gi * d:(gi + 1) * d].astype(BF16)
        s = _dot_nt(q, kb) * scale + bias_ref[gi]
        s = jnp.where(mask, s, NEG_BIG)
        sink = sink_ref[hkv * group + gi]
        m = jnp.broadcast_to(jnp.maximum(jnp.max(s, axis=-1, keepdims=True), sink), (w, d))
        p = jnp.concatenate([jnp.exp(s[:, :d] - m), jnp.exp(s[:, d:] - m)], axis=1)
        o = jnp.dot(p.astype(BF16), vb1, preferred_element_type=F32)
        o_ref[0, :, gi * d:(gi + 1) * d] = (o[:, :d] / (o[:, d:] + jnp.exp(sink - m))).astype(o_ref.dtype)


def swa_mixer(proj, col0, heads, kv_heads, sinks, bias):
    bsz, s, _ = proj.shape
    w = WINDOW
    d = HEAD_DIM
    group = heads // kv_heads
    nb = s // w
    q0 = col0 // (group * d)
    k0 = (col0 + heads * d) // d
    v0 = k0 + kv_heads
    assert col0 % (group * d) == 0

    def prev(c0):
        return pl.BlockSpec((1, w, d), lambda b, n, h: (b, jnp.maximum(n - 1, 0), c0 + h))

    def cur(c0):
        return pl.BlockSpec((1, w, d), lambda b, n, h: (b, n, c0 + h))

    return pl.pallas_call(
        functools.partial(_swa_kernel, group=group),
        grid=(bsz, nb, kv_heads),
        in_specs=[pl.BlockSpec(memory_space=pltpu.SMEM),
                  pl.BlockSpec((1, w, group * d), lambda b, n, h: (b, n, q0 + h)),
                  prev(k0), cur(k0), prev(v0), cur(v0),
                  pl.BlockSpec((group, w, 2 * w), lambda b, n, h: (h, 0, 0))],
        out_specs=pl.BlockSpec((1, w, group * d), lambda b, n, h: (b, n, h)),
        out_shape=jax.ShapeDtypeStruct((bsz, s, heads * d), BF16),
        compiler_params=_params("parallel", "parallel", "parallel"),
        name="swa_mixer",
    )(sinks.astype(F32), proj, proj, proj, proj, proj, bias)


def _fox_cumsum_kernel(f_ref, o_ref, *, per_head):
    ls = _log_sigmoid(f_ref[0])
    r = ls.shape[0]
    li = lax.broadcasted_iota(jnp.int32, (128, 128), 0)
    lj = lax.broadcasted_iota(jnp.int32, (128, 128), 1)
    within = jnp.dot(ls, (li <= lj).astype(F32), precision=HIGHEST, preferred_element_type=F32)
    total = jnp.dot(ls, jnp.ones((128, 128), F32), precision=HIGHEST, preferred_element_type=F32)
    ri = lax.broadcasted_iota(jnp.int32, (r, r), 0)
    rj = lax.broadcasted_iota(jnp.int32, (r, r), 1)
    before = ((rj < ri) & (rj // per_head == ri // per_head)).astype(F32)
    o_ref[0] = within + jnp.dot(before, total, precision=HIGHEST, preferred_element_type=F32)


def fox_log_decay(f_logit):
    bsz, s, h = f_logit.shape
    per_head = s // 128
    f = jnp.transpose(f_logit, (0, 2, 1)).reshape(bsz, h * per_head, 128)
    c = pl.pallas_call(
        functools.partial(_fox_cumsum_kernel, per_head=per_head),
        grid=(bsz,),
        in_specs=[pl.BlockSpec((1, h * per_head, 128), lambda b: (b, 0, 0))],
        out_specs=pl.BlockSpec((1, h * per_head, 128), lambda b: (b, 0, 0)),
        out_shape=jax.ShapeDtypeStruct((bsz, h * per_head, 128), F32),
        compiler_params=_params("parallel"),
        name="fox_log_decay",
    )(f)
    return c.reshape(bsz, h, s)


FOX_HEADS_PER_STEP = 8
LOG2E = math.log2(math.e)


def _fox_kernel(q_ref, k_ref, v_ref, ck_ref, o_ref, m_ref, acc_ref, *, tq, tk, hp):
    i = pl.program_id(2)
    j = pl.program_id(3)
    d = HEAD_DIM

    @pl.when(j == 0)
    def _():
        m_ref[...] = jnp.full_like(m_ref, NEG_BIG)
        acc_ref[...] = jnp.zeros_like(acc_ref)

    def step(masked):
        if masked:
            qpos = i * tq + lax.broadcasted_iota(jnp.int32, (tq, tk), 0)
            kpos = j * tk + lax.broadcasted_iota(jnp.int32, (tq, tk), 1)
            keep = kpos <= qpos
        ones = jnp.ones((tk, d), BF16)
        for h in range(hp):
            cols = slice(h * d, (h + 1) * d)
            s = _dot_nt(q_ref[0, :, cols], k_ref[0, :, cols]) - ck_ref[0, h] * LOG2E
            if masked:
                s = jnp.where(keep, s, NEG_BIG)
            m_prev = m_ref[h]
            m_new = jnp.maximum(m_prev, jnp.max(s, axis=-1, keepdims=True))
            a = jnp.exp2(m_prev - m_new)
            p = jnp.concatenate([jnp.exp2(s[:, c * d:(c + 1) * d] - m_new) for c in range(tk // d)], axis=1)
            v1 = jnp.concatenate([v_ref[0, :, cols], ones], axis=1)
            acc_ref[h] = jnp.concatenate([a, a], axis=1) * acc_ref[h] + jnp.dot(
                p.astype(BF16), v1, preferred_element_type=F32)
            m_ref[h] = m_new

    @pl.when(j < i)
    def _():
        step(False)

    @pl.when(j == i)
    def _():
        step(True)
        for h in range(hp):
            acc = acc_ref[h]
            o_ref[0, :, h * d:(h + 1) * d] = (acc[:, :d] / acc[:, d:]).astype(o_ref.dtype)


def fox_mixer(proj, c, heads, t_pref=512):
    bsz, s, _ = proj.shape
    d = HEAD_DIM
    hp = FOX_HEADS_PER_STEP
    assert heads % hp == 0
    t = _tile(s, t_pref, 128)
    nb = s // t
    groups = heads // hp
    c_row = c.reshape(bsz, heads, 1, s)
    return pl.pallas_call(
        functools.partial(_fox_kernel, tq=t, tk=t, hp=hp),
        grid=(bsz, groups, nb, nb),
        in_specs=[pl.BlockSpec((1, t, hp * d), lambda b, h, i, j: (b, i, h)),
                  pl.BlockSpec((1, t, hp * d), lambda b, h, i, j: (b, jnp.minimum(j, i), groups + h)),
                  pl.BlockSpec((1, t, hp * d), lambda b, h, i, j: (b, jnp.minimum(j, i), 2 * groups + h)),
                  pl.BlockSpec((1, hp, 1, t), lambda b, h, i, j: (b, h, 0, jnp.minimum(j, i)))],
        out_specs=pl.BlockSpec((1, t, hp * d), lambda b, h, i, j: (b, i, h)),
        out_shape=jax.ShapeDtypeStruct((bsz, s, heads * d), BF16),
        scratch_shapes=[pltpu.VMEM((hp, t, d), F32), pltpu.VMEM((hp, t, 2 * d), F32)],
        compiler_params=_params("parallel", "parallel", "parallel", "arbitrary"),
        name="fox_mixer",
    )(proj, proj, proj, c_row)


def _router_kernel(x_ref, wt_ref, rb_ref, idx_ref, wts_ref, rank_ref, cnt_ref, carry_ref):
    e = wt_ref.shape[0]
    tm = x_ref.shape[0]
    per = e // N_GROUPS

    @pl.when(pl.program_id(0) == 0)
    def _():
        carry_ref[...] = jnp.zeros_like(carry_ref)

    logits = _dot_nt(wt_ref[...], x_ref[...], precision=HIGHEST)
    scores = _sigmoid(logits)
    biased = scores + rb_ref[...]
    g3 = biased.reshape(N_GROUPS, per, tm)
    mi = lax.broadcasted_iota(jnp.int32, (N_GROUPS, per, tm), 1)
    m1 = jnp.max(g3, axis=1, keepdims=True)
    i1 = jnp.min(jnp.where(g3 == m1, mi, per), axis=1, keepdims=True)
    m2 = jnp.max(jnp.where(mi == i1, -jnp.inf, g3), axis=1, keepdims=True)
    cur = (m1 + m2).reshape(N_GROUPS, tm)
    gi = lax.broadcasted_iota(jnp.int32, (N_GROUPS, tm), 0)
    sel = jnp.zeros((N_GROUPS, tm), jnp.bool_)
    for _ in range(TOPK_GROUPS):
        mx = jnp.max(cur, axis=0, keepdims=True)
        pick = gi == jnp.min(jnp.where(cur == mx, gi, N_GROUPS), axis=0, keepdims=True)
        sel = sel | pick
        cur = jnp.where(pick, -jnp.inf, cur)
    ok = jnp.broadcast_to(sel.reshape(N_GROUPS, 1, tm), (N_GROUPS, per, tm)).reshape(e, tm)
    cur = jnp.where(ok, biased, NEG_BIG)
    ei = lax.broadcasted_iota(jnp.int32, (e, tm), 0)
    chosen = jnp.zeros((e, tm), jnp.bool_)
    picks = []
    for k in range(TOP_K):
        mx = jnp.max(cur, axis=0, keepdims=True)
        ek = jnp.min(jnp.where(cur == mx, ei, e), axis=0, keepdims=True)
        pick = ei == ek
        chosen = chosen | pick
        cur = jnp.where(pick, -jnp.inf, cur)
        picks.append((ek, pick, jnp.sum(jnp.where(pick, scores, 0.0), axis=0, keepdims=True)))
    denom = sum(w for _, _, w in picks)
    si = lax.broadcasted_iota(jnp.int32, (tm, tm), 0)
    ti = lax.broadcasted_iota(jnp.int32, (tm, tm), 1)
    before = jnp.dot(chosen.astype(BF16), (si < ti).astype(BF16), preferred_element_type=F32)
    before = before + carry_ref[...]
    for k, (ek, pick, w) in enumerate(picks):
        idx_ref[k:k + 1, :] = ek
        wts_ref[k:k + 1, :] = w / denom * ROUTED_SCALE
        rank_ref[k:k + 1, :] = jnp.sum(jnp.where(pick, before, 0.0), axis=0, keepdims=True).astype(jnp.int32)
    carry_ref[...] += jnp.sum(chosen.astype(F32), axis=1, keepdims=True)
    cnt_ref[...] = carry_ref[...].astype(jnp.int32)


def router(x, w_router, router_bias, tm_pref=512):
    m, d = x.shape
    e = w_router.shape[1]
    tm = _tile(m, tm_pref, 128)
    pick = pl.BlockSpec((TOP_K, tm), lambda i: (0, i))
    return pl.pallas_call(
        _router_kernel,
        grid=(m // tm,),
        in_specs=[pl.BlockSpec((tm, d), lambda i: (i, 0)),
                  pl.BlockSpec((e, d), lambda i: (0, 0)),
                  pl.BlockSpec((e, 1), lambda i: (0, 0))],
        out_specs=[pick, pick, pick, pl.BlockSpec((e, 1), lambda i: (0, 0))],
        out_shape=[jax.ShapeDtypeStruct((TOP_K, m), jnp.int32), jax.ShapeDtypeStruct((TOP_K, m), F32),
                   jax.ShapeDtypeStruct((TOP_K, m), jnp.int32), jax.ShapeDtypeStruct((e, 1), jnp.int32)],
        scratch_shapes=[pltpu.VMEM((e, 1), F32)],
        compiler_params=_params("arbitrary"),
        name="moe_router",
    )(x, jnp.transpose(w_router).astype(F32), router_bias.reshape(e, 1).astype(F32))


DISPATCH_UNROLL = 8


def _dispatch_kernel(dst_ref, x_ref, xs_hbm, sem, *, tm):
    for k in range(TOP_K):
        def body(t, carry, k=k):
            pltpu.make_async_copy(x_ref.at[pl.ds(t, 1)], xs_hbm.at[pl.ds(dst_ref[k, t], 1)], sem).start()
            return carry
        lax.fori_loop(0, tm, body, 0, unroll=DISPATCH_UNROLL)
    for k in range(TOP_K):
        pltpu.make_async_copy(x_ref, xs_hbm.at[pl.ds(0, tm)], sem).wait()


def moe_dispatch(xw, dst, tm):
    m, h = xw.shape
    return pl.pallas_call(
        functools.partial(_dispatch_kernel, tm=tm),
        grid=(m // tm,),
        in_specs=[pl.BlockSpec((TOP_K, tm), lambda i: (0, i), memory_space=pltpu.SMEM),
                  pl.BlockSpec((tm, h), lambda i: (i, 0))],
        out_specs=pl.BlockSpec(memory_space=pl.ANY),
        out_shape=jax.ShapeDtypeStruct((TOP_K * m, h), jnp.uint32),
        scratch_shapes=[pltpu.SemaphoreType.DMA(())],
        compiler_params=_params("arbitrary"),
        name="moe_dispatch",
    )(dst, xw)


def _group_metadata(counts, n_rows, t):
    e = counts.shape[0]
    n_items = n_rows // t + e - 1
    lanes = -(-n_items // 128) * 128
    plan, offs = pl.pallas_call(
        functools.partial(_plan_kernel, t=t),
        out_shape=[jax.ShapeDtypeStruct((8, lanes), jnp.int32), jax.ShapeDtypeStruct((e, 1), jnp.int32)],
        name="moe_plan",
    )(counts.reshape(e, 1))
    rows = [plan[r, :n_items] for r in range(6)]
    return (*rows, plan[6, :1]), offs.reshape(e)


def _plan_kernel(cnt_ref, plan_ref, offs_ref, *, t):
    e = cnt_ref.shape[0]
    lanes = plan_ref.shape[1]
    cnt = cnt_ref[...]
    er = lax.broadcasted_iota(jnp.int32, (e, e), 0)
    ec = lax.broadcasted_iota(jnp.int32, (e, e), 1)
    incl = (ec <= er).astype(F32)

    def cumsum(col):
        return jnp.dot(incl, col.astype(F32), precision=HIGHEST, preferred_element_type=F32).astype(jnp.int32)

    ends = cumsum(cnt)
    offs = ends - cnt
    first_tile = offs // t
    tiles_of = jnp.where(cnt > 0, (ends - 1) // t - first_tile + 1, 0)
    item_end = cumsum(tiles_of)
    used = jnp.max(item_end, axis=0, keepdims=True)
    ids = jnp.minimum(lax.broadcasted_iota(jnp.int32, (1, lanes), 1), used - 1)
    exp = jnp.minimum(jnp.sum((item_end <= ids).astype(jnp.int32), axis=0, keepdims=True), e - 1)
    onehot = lax.broadcasted_iota(jnp.int32, (e, lanes), 0) == exp

    def pick(col):
        return jnp.sum(jnp.where(onehot, col, 0), axis=0, keepdims=True)

    tile = pick(first_tile) + ids - (pick(item_end) - pick(tiles_of))
    lo = jnp.clip(pick(offs) - tile * t, 0, t)
    hi = jnp.clip(pick(ends) - tile * t, 0, t)
    lane = lax.broadcasted_iota(jnp.int32, (1, lanes), 1)
    prev_tile = jnp.where(lane == 0, -1, pltpu.roll(tile, 1, axis=1))
    prev_exp = jnp.where(lane == 0, -1, pltpu.roll(exp, 1, axis=1))
    plan_ref[0:1, :] = tile
    plan_ref[1:2, :] = exp
    plan_ref[2:3, :] = lo
    plan_ref[3:4, :] = hi
    plan_ref[4:5, :] = (tile != prev_tile).astype(jnp.int32)
    plan_ref[5:6, :] = (exp != prev_exp).astype(jnp.int32)
    plan_ref[6:7, :] = jnp.broadcast_to(used, (1, lanes))
    plan_ref[7:8, :] = jnp.zeros((1, lanes), jnp.int32)
    offs_ref[...] = offs


def _group_ffn_kernel(tile_ref, exp_ref, lo_ref, hi_ref, first_ref, newexp_ref, used_ref,
                      x_ref, wg_ref, wu_ref, wd_ref, o_ref, wgb, wub, wdb):
    i = pl.program_id(0)
    t, h = x_ref.shape

    @pl.when(i < used_ref[0])
    def _():
        @pl.when(newexp_ref[i] == 1)
        def _():
            wgb[...] = wg_ref[0, 0].astype(BF16)
            wub[...] = wu_ref[0, 0].astype(BF16)
            wdb[...] = wd_ref[0, 0].astype(BF16)

        def ffn(mine):
            left, right = _unpack_halves(x_ref[...])
            left = left.astype(BF16)
            right = right.astype(BF16)
            g = _dot_nt(left, wgb[:, :h]) + _dot_nt(right, wgb[:, h:])
            u = _dot_nt(left, wub[:, :h]) + _dot_nt(right, wub[:, h:])
            a = g * _sigmoid(g) * u
            if mine is not None:
                a = jnp.where(mine, a, 0.0)
            return _pack_halves(jnp.dot(a.astype(BF16), wdb[...], preferred_element_type=F32))

        whole = (lo_ref[i] == 0) & (hi_ref[i] == t)

        @pl.when(whole)
        def _():
            o_ref[...] = ffn(None)

        @pl.when(jnp.logical_not(whole))
        def _():
            row = lax.broadcasted_iota(jnp.int32, (t, 1), 0)
            mine = (row >= lo_ref[i]) & (row < hi_ref[i])
            y = ffn(mine)

            @pl.when(first_ref[i] == 1)
            def _():
                o_ref[...] = y

            @pl.when(first_ref[i] == 0)
            def _():
                o_ref[...] = jnp.where(mine, y, o_ref[...])


def moe_group_ffn(xs, meta, wg, wu, wd, layer, t):
    r, h = xs.shape
    _, e, f, d = wg.shape
    n_items = r // t + e - 1
    grid_spec = pltpu.PrefetchScalarGridSpec(
        num_scalar_prefetch=7,
        grid=(n_items,),
        in_specs=[pl.BlockSpec((t, h), lambda i, tile, *_: (tile[i], 0)),
                  pl.BlockSpec((1, 1, f, d), lambda i, tile, exp, *_: (layer, exp[i], 0, 0)),
                  pl.BlockSpec((1, 1, f, d), lambda i, tile, exp, *_: (layer, exp[i], 0, 0)),
                  pl.BlockSpec((1, 1, f, d), lambda i, tile, exp, *_: (layer, exp[i], 0, 0))],
        out_specs=pl.BlockSpec((t, h), lambda i, tile, *_: (tile[i], 0)),
        scratch_shapes=[pltpu.VMEM((f, d), BF16), pltpu.VMEM((f, d), BF16), pltpu.VMEM((f, d), BF16)],
    )
    return pl.pallas_call(
        _group_ffn_kernel,
        grid_spec=grid_spec,
        out_shape=jax.ShapeDtypeStruct((r, h), jnp.uint32),
        compiler_params=_params("arbitrary"),
        name="moe_group_ffn",
    )(*meta, xs, wg, wu, wd)


def _combine_kernel(dst_ref, nxt_ref, wts_ref, x_ref, xb_ref, sg_ref, su_ref, sd_ref, g_ref, b_ref, ys_hbm,
                    o_ref, ob_ref, buf, sem, *, alpha, tm, steps):
    i = pl.program_id(0)
    slot = i % 2

    def gather(idx_ref, s):
        for k in range(TOP_K):
            def body(t, carry, k=k):
                pltpu.make_async_copy(ys_hbm.at[pl.ds(idx_ref[k, t], 1)], buf.at[s, k, pl.ds(t, 1)],
                                      sem.at[s]).start()
                return carry
            lax.fori_loop(0, tm, body, 0, unroll=DISPATCH_UNROLL)

    @pl.when(i == 0)
    def _():
        gather(dst_ref, 0)

    @pl.when(i + 1 < steps)
    def _():
        gather(nxt_ref, 1 - slot)

    for k in range(TOP_K):
        pltpu.make_async_copy(ys_hbm.at[pl.ds(0, tm)], buf.at[slot, k], sem.at[slot]).wait()

    h = buf.shape[-1]
    acc_l = jnp.zeros((tm, h), F32)
    acc_r = jnp.zeros((tm, h), F32)
    wts = jnp.transpose(wts_ref[...])
    for k in range(TOP_K):
        left, right = _unpack_halves(buf[slot, k])
        c = wts[:, k:k + 1]
        acc_l += c * left
        acc_r += c * right
    xb = xb_ref[...]
    hs = jnp.dot(xb, sg_ref[...], preferred_element_type=F32)
    hs = hs * _sigmoid(hs) * jnp.dot(xb, su_ref[...], preferred_element_type=F32)
    y = jnp.concatenate([acc_l, acc_r], axis=1) + jnp.dot(hs.astype(BF16), sd_ref[...], preferred_element_type=F32)
    out = _layer_norm(alpha * x_ref[...] + y, g_ref[...], b_ref[...])
    o_ref[...] = out
    ob_ref[...] = out.astype(BF16)


def moe_combine_ln(ys, dst_tiles, wts, x, xb, sg, su, sd, g, b, alpha, tm):
    m, d = x.shape
    h = ys.shape[1]
    fs = sg.shape[1]
    steps = m // tm
    row = pl.BlockSpec((tm, d), lambda i: (i, 0))
    vec = pl.BlockSpec((1, d), lambda i: (0, 0))
    return pl.pallas_call(
        functools.partial(_combine_kernel, alpha=alpha, tm=tm, steps=steps),
        grid=(steps,),
        in_specs=[pl.BlockSpec((TOP_K, tm), lambda i: (0, i), memory_space=pltpu.SMEM),
                  pl.BlockSpec((TOP_K, tm), lambda i: (0, jnp.minimum(i + 1, steps - 1)), memory_space=pltpu.SMEM),
                  pl.BlockSpec((TOP_K, tm), lambda i: (0, i)),
                  row, row,
                  pl.BlockSpec((d, fs), lambda i: (0, 0)),
                  pl.BlockSpec((d, fs), lambda i: (0, 0)),
                  pl.BlockSpec((fs, d), lambda i: (0, 0)),
                  vec, vec,
                  pl.BlockSpec(memory_space=pl.ANY)],
        out_specs=[row, row],
        out_shape=[jax.ShapeDtypeStruct((m, d), F32), jax.ShapeDtypeStruct((m, d), BF16)],
        scratch_shapes=[pltpu.VMEM((2, TOP_K, tm, h), jnp.uint32), pltpu.SemaphoreType.DMA((2,))],
        compiler_params=_params("arbitrary"),
        name="moe_combine_ln",
    )(dst_tiles, dst_tiles, wts, x, xb, sg, su, sd, g.reshape(1, d), b.reshape(1, d), ys)


def moe_block(xf, xb, xw, w_router, router_bias, wg, wu, wd, layer, sg, su, sd, g, b, alpha,
              t_rows=256, tm_dispatch=512, tm_combine=128):
    m = xf.shape[0]
    e = w_router.shape[1]
    t_rows = _tile(TOP_K * m, t_rows, 8)
    tm_dispatch = _tile(m, tm_dispatch, 128)
    tm_combine = _tile(m, tm_combine, 128)
    idx, wts, rank, counts = router(xf, w_router, router_bias)
    meta, offs = _group_metadata(counts.reshape(-1), TOP_K * m, t_rows)
    start = jnp.sum(jnp.where(idx[:, None, :] == jnp.arange(e, dtype=jnp.int32)[None, :, None],
                              offs[None, :, None], 0), axis=1)
    dst = (start + rank).astype(jnp.int32)
    xs = moe_dispatch(xw, dst, tm_dispatch)
    ys = moe_group_ffn(xs, meta, jnp.swapaxes(wg, 2, 3), jnp.swapaxes(wu, 2, 3), wd, layer, t_rows)
    return moe_combine_ln(ys, dst, wts, xf, xb, sg.astype(BF16), su.astype(BF16), sd.astype(BF16),
                          g, b, alpha, tm_combine)


def kernel(x, w_in_ab, b_in_ab, w_in_cd, b_in_cd, hgrn_lb_logits, hgrn_norm, sinks, rel_bias, gla_w2, gla_b,
           gla_norm, w_out, ln_g, ln_b, w_router, router_bias, w_exp_gate, w_exp_up, w_exp_down, w_sh_gate,
           w_sh_up, w_sh_down):
    bsz, s, d = x.shape
    m = bsz * s
    depth = w_out.shape[0]
    alpha = (2.0 * depth) ** 0.25
    half = d // 2
    a_heads = half // HEAD_DIM
    b_heads = half // HEAD_DIM
    b_kv_heads = b_heads // 4
    c_dk = (d // 4) // C_HEADS
    c_dv = half // C_HEADS
    d_heads = half // HEAD_DIM
    rank = gla_w2.shape[1]
    c_cols = 2 * C_HEADS * c_dk + 2 * C_HEADS * c_dv
    d_cols = 3 * d_heads * HEAD_DIM

    lbs = lower_bounds(hgrn_lb_logits)
    bias = band_bias(rel_bias)

    xf = x.reshape(m, d).astype(F32)
    xb = xf.astype(BF16)
    for l in range(depth):
        j = l // 2
        if l % 2 == 0:
            proj = matmul_bias_layer(xb, w_in_ab, j, w_in_ab.shape[2], b_in_ab[j], F32).reshape(bsz, s, -1)
            o1 = hgrn2_mixer(proj, lbs[l], hgrn_norm[j], a_heads)
            o2 = swa_mixer(proj, 4 * a_heads * HEAD_DIM, b_heads, b_kv_heads, sinks[j], bias)
        else:
            w_in = w_in_cd[j]
            b_in = b_in_cd[j]
            d0 = c_cols + rank
            f0 = d0 + d_cols
            proj_c = matmul_bias(xb, w_in[:, :c_cols].astype(BF16), b_in[:c_cols], F32).reshape(bsz, s, -1)
            qfold = jnp.where(jnp.arange(d_cols) < d_heads * HEAD_DIM, LOG2E * HEAD_DIM ** -0.5, 1.0).astype(F32)
            proj_d = matmul_bias(xb, (w_in[:, d0:f0] * qfold).astype(BF16), b_in[d0:f0] * qfold,
                                 BF16).reshape(bsz, s, -1)
            w_small = jnp.concatenate([w_in[:, c_cols:d0], w_in[:, f0:]], axis=1).astype(BF16)
            b_small = jnp.concatenate([b_in[c_cols:d0], b_in[f0:]])
            small = matmul_bias(xb, w_small, b_small, F32).reshape(bsz, s, -1)
            o1 = gla_mixer(proj_c, small[..., :rank], gla_w2[j].astype(F32), gla_b[j].astype(F32), gla_norm[j],
                           C_HEADS, c_dk, c_dv)
            o2 = fox_mixer(proj_d, fox_log_decay(small[..., rank:]), d_heads)
        mix = matmul_pair(o1.reshape(m, half), o2.reshape(m, half), w_out, l, F32)
        xf, xb, xw = add_layer_norm(xf, mix, ln_g[l, 0], ln_b[l, 0], alpha)
        xf, xb = moe_block(xf, xb, xw, w_router[l], router_bias[l], w_exp_gate, w_exp_up, w_exp_down, l,
                           w_sh_gate[l], w_sh_up[l], w_sh_down[l], ln_g[l, 1], ln_b[l, 1], alpha)
    return xf.reshape(bsz, s, d).astype(x.dtype)
```

```python
import functools
import math

import numpy as np
import jax
import jax.numpy as jnp
from jax import lax
from jax.experimental import pallas as pl
from jax.experimental.pallas import tpu as pltpu

F32 = jnp.float32
BF16 = jnp.bfloat16
HIGHEST = lax.Precision.HIGHEST

HEAD_DIM = 128
WINDOW = 128
C_HEADS = 4
GLA_TAU = 16.0
CHUNK = 64
SUB = 16
N_BUCKETS = 32
MAX_DISTANCE = 128
TOP_K = 8
N_GROUPS = 8
TOPK_GROUPS = 4
ROUTED_SCALE = 2.5
LN_EPS = 1e-5
RMS_EPS = 1e-6
NEG_BIG = -1e30

VMEM_LIMIT_BYTES = 52 * 1024 * 1024


def _params(*sem):
    return pltpu.CompilerParams(dimension_semantics=sem, vmem_limit_bytes=VMEM_LIMIT_BYTES)


def _tile(n, pref, align):
    t = min(pref, n)
    t -= t % align
    while t >= align:
        if n % t == 0:
            return t
        t -= align
    return n


def _sigmoid(x):
    return 1.0 / (1.0 + jnp.exp(-x))


def _log_sigmoid(x):
    return jnp.minimum(x, 0.0) - jnp.log(1.0 + jnp.exp(-jnp.abs(x)))


def _dot_nt(a, b, **kw):
    return lax.dot_general(a, b, (((1,), (1,)), ((), ())), preferred_element_type=F32, **kw)


def _dot_tn(a, b, **kw):
    return lax.dot_general(a, b, (((0,), (0,)), ((), ())), preferred_element_type=F32, **kw)


def _mm_kernel(x_ref, w_ref, b_ref, o_ref):
    acc = jnp.dot(x_ref[...], w_ref[...], preferred_element_type=F32)
    o_ref[...] = (acc + b_ref[...]).astype(o_ref.dtype)


def matmul_bias(x, w, b, out_dtype, tm_pref=512, tn_pref=1024):
    m, k = x.shape
    n = w.shape[1]
    tm = _tile(m, tm_pref, 8)
    tn = _tile(n, tn_pref, 128)
    return pl.pallas_call(
        _mm_kernel,
        grid=(n // tn, m // tm),
        in_specs=[pl.BlockSpec((tm, k), lambda j, i: (i, 0)),
                  pl.BlockSpec((k, tn), lambda j, i: (0, j)),
                  pl.BlockSpec((1, tn), lambda j, i: (0, j))],
        out_specs=pl.BlockSpec((tm, tn), lambda j, i: (i, j)),
        out_shape=jax.ShapeDtypeStruct((m, n), out_dtype),
        compiler_params=_params("parallel", "parallel"),
        name="matmul_bias",
    )(x, w, b.reshape(1, n).astype(F32))


def _mm_layer_kernel(x_ref, w_ref, b_ref, o_ref, wb_ref):
    @pl.when(pl.program_id(1) == 0)
    def _():
        wb_ref[...] = w_ref[0].astype(BF16)

    acc = jnp.dot(x_ref[...], wb_ref[...], preferred_element_type=F32)
    o_ref[...] = (acc + b_ref[...]).astype(o_ref.dtype)


def matmul_bias_layer(x, w_stack, layer, n_cols, b, out_dtype, tm_pref=512, tn_pref=1024):
    m, k = x.shape
    tm = _tile(m, tm_pref, 8)
    tn = _tile(n_cols, tn_pref, 128)
    return pl.pallas_call(
        _mm_layer_kernel,
        grid=(n_cols // tn, m // tm),
        in_specs=[pl.BlockSpec((tm, k), lambda j, i: (i, 0)),
                  pl.BlockSpec((1, k, tn), lambda j, i: (layer, 0, j), pipeline_mode=pl.Buffered(1)),
                  pl.BlockSpec((1, tn), lambda j, i: (0, j))],
        out_specs=pl.BlockSpec((tm, tn), lambda j, i: (i, j)),
        out_shape=jax.ShapeDtypeStruct((m, n_cols), out_dtype),
        scratch_shapes=[pltpu.VMEM((k, tn), BF16)],
        compiler_params=_params("parallel", "arbitrary"),
        name="matmul_bias_layer",
    )(x, w_stack, b.reshape(1, n_cols).astype(F32))


def _mm2_kernel(a1_ref, a2_ref, w1_ref, w2_ref, o_ref, w1b_ref, w2b_ref):
    @pl.when(pl.program_id(1) == 0)
    def _():
        w1b_ref[...] = w1_ref[0].astype(BF16)
        w2b_ref[...] = w2_ref[0].astype(BF16)

    acc = jnp.dot(a1_ref[...], w1b_ref[...], preferred_element_type=F32)
    acc += jnp.dot(a2_ref[...], w2b_ref[...], preferred_element_type=F32)
    o_ref[...] = acc.astype(o_ref.dtype)


def matmul_pair(a1, a2, w_stack, layer, out_dtype, tm_pref=512, tn_pref=1024):
    m, k1 = a1.shape
    assert a2.shape[1] == k1 and w_stack.shape[1] == 2 * k1
    n = w_stack.shape[2]
    tm = _tile(m, tm_pref, 8)
    tn = _tile(n, tn_pref, 128)
    return pl.pallas_call(
        _mm2_kernel,
        grid=(n // tn, m // tm),
        in_specs=[pl.BlockSpec((tm, k1), lambda j, i: (i, 0)),
                  pl.BlockSpec((tm, k1), lambda j, i: (i, 0)),
                  pl.BlockSpec((1, k1, tn), lambda j, i: (layer, 0, j), pipeline_mode=pl.Buffered(1)),
                  pl.BlockSpec((1, k1, tn), lambda j, i: (layer, 1, j), pipeline_mode=pl.Buffered(1))],
        out_specs=pl.BlockSpec((tm, tn), lambda j, i: (i, j)),
        out_shape=jax.ShapeDtypeStruct((m, n), out_dtype),
        scratch_shapes=[pltpu.VMEM((k1, tn), BF16), pltpu.VMEM((k1, tn), BF16)],
        compiler_params=_params("parallel", "arbitrary"),
        name="matmul_pair",
    )(a1, a2, w_stack, w_stack)


def _pack_halves(v):
    h = v.shape[1] // 2
    r = v.astype(BF16).astype(F32)
    lo = lax.bitcast_convert_type(r[:, :h], jnp.uint32) >> 16
    hi = lax.bitcast_convert_type(r[:, h:], jnp.uint32) & jnp.uint32(0xFFFF0000)
    return lo | hi


def _unpack_halves(w):
    left = lax.bitcast_convert_type(w << 16, F32)
    right = lax.bitcast_convert_type(w & jnp.uint32(0xFFFF0000), F32)
    return left, right


def _layer_norm(z, g, b):
    mu = jnp.mean(z, axis=-1, keepdims=True)
    zc = z - mu
    var = jnp.mean(zc * zc, axis=-1, keepdims=True)
    return zc * lax.rsqrt(var + LN_EPS) * g + b


def _add_ln_kernel(x_ref, y_ref, g_ref, b_ref, o_ref, ob_ref, ow_ref, *, alpha):
    out = _layer_norm(alpha * x_ref[...] + y_ref[...], g_ref[...], b_ref[...])
    o_ref[...] = out
    ob_ref[...] = out.astype(BF16)
    ow_ref[...] = _pack_halves(out)


def add_layer_norm(x, y, g, b, alpha, tm_pref=256):
    m, d = x.shape
    tm = _tile(m, tm_pref, 16)
    row = pl.BlockSpec((tm, d), lambda i: (i, 0))
    half = pl.BlockSpec((tm, d // 2), lambda i: (i, 0))
    vec = pl.BlockSpec((1, d), lambda i: (0, 0))
    return pl.pallas_call(
        functools.partial(_add_ln_kernel, alpha=alpha),
        grid=(m // tm,),
        in_specs=[row, row, vec, vec],
        out_specs=[row, row, half],
        out_shape=[jax.ShapeDtypeStruct((m, d), F32), jax.ShapeDtypeStruct((m, d), BF16),
                   jax.ShapeDtypeStruct((m, d // 2), jnp.uint32)],
        compiler_params=_params("parallel"),
        name="add_layer_norm",
    )(x, y, g.reshape(1, d), b.reshape(1, d))


def _lower_bounds_kernel(z_ref, o_ref):
    z = z_ref[...]
    depth = z.shape[0]
    e = jnp.exp(z - jnp.max(z, axis=0, keepdims=True))
    p = e / jnp.sum(e, axis=0, keepdims=True)
    run = jnp.zeros_like(p[0:1])
    for l in range(depth):
        run = run + p[l:l + 1]
        o_ref[l:l + 1, :] = run - p[0:1]


def lower_bounds(logits):
    return pl.pallas_call(
        _lower_bounds_kernel,
        out_shape=jax.ShapeDtypeStruct(logits.shape, F32),
        name="hgrn_lower_bounds",
    )(logits.astype(F32))


def _gla_chunk(q, k, v, g, st_ref):
    dk = q.shape[1]
    nsub = CHUNK // SUB
    row = lax.broadcasted_iota(jnp.int32, (CHUNK, CHUNK), 0)
    col = lax.broadcasted_iota(jnp.int32, (CHUNK, CHUNK), 1)
    tri = (col <= row).astype(F32)
    b = jnp.dot(tri, g, precision=HIGHEST, preferred_element_type=F32)
    b_last = b[CHUNK - 1:CHUNK, :]
    st = st_ref[...]

    qe = q * jnp.exp(b)
    o = _dot_nt(qe.astype(BF16), st.astype(BF16))

    levels = [jnp.zeros((1, dk), F32)] + [b[SUB * i - 1:SUB * i, :] for i in range(1, nsub)]
    lvl_rows = jnp.concatenate([jnp.broadcast_to(r, (SUB, dk)) for r in levels], axis=0)
    qt = q * jnp.exp(b - lvl_rows)
    row_blk = lax.broadcasted_iota(jnp.int32, (CHUNK, dk), 0) // SUB
    qhat = jnp.concatenate([jnp.where(row_blk == i, qt, 0.0) for i in range(1, nsub)], axis=1)
    khat = jnp.concatenate([k * jnp.exp(jnp.minimum(levels[i] - b, 0.0)) for i in range(1, nsub)], axis=1)
    p = _dot_nt(qhat.astype(BF16), khat.astype(BF16))
    p = jnp.where(col < (row // SUB) * SUB, p, 0.0)
    o += jnp.dot(p.astype(BF16), v.astype(BF16), preferred_element_type=F32)

    b4 = b.reshape(nsub, SUB, dk)
    q4 = q.reshape(nsub, SUB, dk)
    k4 = k.reshape(nsub, SUB, dk)
    ti = lax.broadcasted_iota(jnp.int32, (1, SUB, SUB, 1), 1)
    si = lax.broadcasted_iota(jnp.int32, (1, SUB, SUB, 1), 2)
    diff = b4[:, :, None, :] - b4[:, None, :, :]
    dec = jnp.exp(jnp.where(si <= ti, diff, NEG_BIG))
    dg = jnp.sum(q4[:, :, None, :] * k4[:, None, :, :] * dec, axis=-1)
    v4 = v.reshape(nsub, SUB, v.shape[1])
    od = jnp.einsum("its,isv->itv", dg.astype(BF16), v4.astype(BF16), preferred_element_type=F32)
    o += od.reshape(CHUNK, v.shape[1])

    kd = k * jnp.exp(b_last - b)
    st_ref[...] = st * jnp.exp(b_last) + _dot_tn(v.astype(BF16), kd.astype(BF16))
    return o


HGRN_HEADS_PER_STEP = 4
GLA_HEADS_PER_STEP = 2
CHUNK_UNROLL = 2


def _hgrn_kernel(q_ref, f_ref, i_ref, gate_ref, lb_ref, nw_ref, o_ref, st_ref, *, chunks, hp):
    @pl.when(pl.program_id(2) == 0)
    def _():
        st_ref[...] = jnp.zeros_like(st_ref)

    d = HEAD_DIM
    nw = nw_ref[...]

    def body(c, carry):
        rows = pl.ds(pl.multiple_of(c * CHUNK, CHUNK), CHUNK)
        for h in range(hp):
            cols = slice(h * d, (h + 1) * d)
            lb = lb_ref[:, cols]
            qa = q_ref[0, rows, cols]
            f = lb + (1.0 - lb) * _sigmoid(f_ref[0, rows, cols])
            o = _gla_chunk(qa * _sigmoid(qa), 1.0 - f, i_ref[0, rows, cols], jnp.log(f), st_ref.at[h])
            o = o * lax.rsqrt(jnp.mean(o * o, axis=-1, keepdims=True) + RMS_EPS) * nw
            o_ref[0, rows, cols] = (o * _sigmoid(gate_ref[0, rows, cols])).astype(o_ref.dtype)
        return carry

    lax.fori_loop(0, chunks, body, 0, unroll=CHUNK_UNROLL)


def hgrn2_mixer(proj, lb, norm_w, heads, t_pref=512):
    bsz, s, _ = proj.shape
    d = HEAD_DIM
    hp = HGRN_HEADS_PER_STEP
    assert heads % hp == 0
    groups = heads // hp
    t = _tile(s, t_pref, CHUNK)

    def col(group):
        return pl.BlockSpec((1, t, hp * d), lambda b, h, i, group=group: (b, i, group * groups + h))

    return pl.pallas_call(
        functools.partial(_hgrn_kernel, chunks=t // CHUNK, hp=hp),
        grid=(bsz, groups, s // t),
        in_specs=[col(0), col(1), col(2), col(3),
                  pl.BlockSpec((1, hp * d), lambda b, h, i: (0, h)),
                  pl.BlockSpec((1, d), lambda b, h, i: (0, 0))],
        out_specs=pl.BlockSpec((1, t, hp * d), lambda b, h, i: (b, i, h)),
        out_shape=jax.ShapeDtypeStruct((bsz, s, heads * d), BF16),
        scratch_shapes=[pltpu.VMEM((hp, d, d), F32)],
        compiler_params=_params("parallel", "parallel", "arbitrary"),
        name="hgrn2_mixer",
    )(proj, proj, proj, proj, lb.reshape(1, heads * d), norm_w.reshape(1, d))


def _gla_kernel(q_ref, k_ref, v_ref, gate_ref, a_ref, w2_ref, gb_ref, nw_ref, o_ref, st_ref,
                *, chunks, scale, hp, dk, dv):
    @pl.when(pl.program_id(2) == 0)
    def _():
        st_ref[...] = jnp.zeros_like(st_ref)

    nw = nw_ref[...]

    def body(c, carry):
        rows = pl.ds(pl.multiple_of(c * CHUNK, CHUNK), CHUNK)
        a = a_ref[0, rows, :]
        for h in range(hp):
            kc = slice(h * dk, (h + 1) * dk)
            vc = slice(h * dv, (h + 1) * dv)
            logit = jnp.dot(a, w2_ref[:, kc], precision=HIGHEST, preferred_element_type=F32) + gb_ref[:, kc]
            g = _log_sigmoid(logit) * (1.0 / GLA_TAU)
            o = _gla_chunk(q_ref[0, rows, kc] * scale, k_ref[0, rows, kc], v_ref[0, rows, vc], g, st_ref.at[h])
            o = o * lax.rsqrt(jnp.mean(o * o, axis=-1, keepdims=True) + RMS_EPS) * nw
            gate = gate_ref[0, rows, vc]
            o_ref[0, rows, vc] = (o * gate * _sigmoid(gate)).astype(o_ref.dtype)
        return carry

    lax.fori_loop(0, chunks, body, 0, unroll=CHUNK_UNROLL)


def gla_mixer(proj, a, w2, gb, norm_w, heads, dk, dv, t_pref=512):
    bsz, s, _ = proj.shape
    rank = a.shape[-1]
    t = _tile(s, t_pref, CHUNK)
    hp = GLA_HEADS_PER_STEP
    assert heads % hp == 0 and (2 * heads * dk) % (hp * dv) == 0
    groups = heads // hp
    v0 = 2 * heads * dk // (hp * dv)
    return pl.pallas_call(
        functools.partial(_gla_kernel, chunks=t // CHUNK, scale=dk ** -0.5, hp=hp, dk=dk, dv=dv),
        grid=(bsz, groups, s // t),
        in_specs=[pl.BlockSpec((1, t, hp * dk), lambda b, h, i: (b, i, h)),
                  pl.BlockSpec((1, t, hp * dk), lambda b, h, i: (b, i, groups + h)),
                  pl.BlockSpec((1, t, hp * dv), lambda b, h, i: (b, i, v0 + h)),
                  pl.BlockSpec((1, t, hp * dv), lambda b, h, i: (b, i, v0 + groups + h)),
                  pl.BlockSpec((1, t, rank), lambda b, h, i: (b, i, 0)),
                  pl.BlockSpec((rank, hp * dk), lambda b, h, i: (0, h)),
                  pl.BlockSpec((1, hp * dk), lambda b, h, i: (0, h)),
                  pl.BlockSpec((1, dv), lambda b, h, i: (0, 0))],
        out_specs=pl.BlockSpec((1, t, hp * dv), lambda b, h, i: (b, i, h)),
        out_shape=jax.ShapeDtypeStruct((bsz, s, heads * dv), BF16),
        scratch_shapes=[pltpu.VMEM((hp, dv, dk), F32)],
        compiler_params=_params("parallel", "parallel", "arbitrary"),
        name="gla_mixer",
    )(proj, proj, proj, proj, a, w2, gb.reshape(1, heads * dk), norm_w.reshape(1, dv))


def _t5_bucket_table():
    dist = WINDOW + np.arange(WINDOW)[:, None] - np.arange(2 * WINDOW)[None, :]
    max_exact = N_BUCKETS // 2
    d = np.maximum(dist, 0)
    large = max_exact + (np.log(np.maximum(d, 1).astype(np.float32) / max_exact)
                         / math.log(MAX_DISTANCE / max_exact) * (N_BUCKETS - max_exact)).astype(np.int32)
    large = np.minimum(large, N_BUCKETS - 1)
    return np.where(d < max_exact, d, large).astype(np.int32)


def _band_bias_kernel(rel_ref, bucket_ref, o_ref):
    h = pl.program_id(0)
    bucket = bucket_ref[...]
    acc = jnp.zeros(bucket.shape, F32)
    for n in range(N_BUCKETS):
        acc = jnp.where(bucket == n, rel_ref[n, h], acc)
    o_ref[0] = acc


def band_bias(rel_bias):
    heads = rel_bias.shape[1]
    bucket = jnp.asarray(_t5_bucket_table())
    return pl.pallas_call(
        _band_bias_kernel,
        grid=(heads,),
        in_specs=[pl.BlockSpec(memory_space=pltpu.SMEM),
                  pl.BlockSpec((WINDOW, 2 * WINDOW), lambda h: (0, 0))],
        out_specs=pl.BlockSpec((1, WINDOW, 2 * WINDOW), lambda h: (h, 0, 0)),
        out_shape=jax.ShapeDtypeStruct((heads, WINDOW, 2 * WINDOW), F32),
        name="t5_band_bias",
    )(rel_bias.astype(F32), bucket)


def _swa_kernel(sink_ref, q_ref, kp_ref, kc_ref, vp_ref, vc_ref, bias_ref, o_ref, *, group):
    n = pl.program_id(1)
    hkv = pl.program_id(2)
    w = WINDOW
    d = HEAD_DIM
    scale = d ** -0.5
    kb = jnp.concatenate([kp_ref[0], kc_ref[0]], axis=0).astype(BF16)
    vb = jnp.concatenate([vp_ref[0], vc_ref[0]], axis=0).astype(BF16)
    vb1 = jnp.concatenate([vb, jnp.ones((2 * w, d), BF16)], axis=1)
    qi = lax.broadcasted_iota(jnp.int32, (w, 2 * w), 0)
    ki = lax.broadcasted_iota(jnp.int32, (w, 2 * w), 1)
    dist = w + qi - ki
    mask = (dist >= 0) & (dist < w) & ((n * w - w + ki) >= 0)
    for gi in range(group):
        q = q_ref[0, :, gi * d:(gi + 1) * d].astype(BF16)
        s = _dot_nt(q, kb) * scale + bias_ref[gi]
        s = jnp.where(mask, s, NEG_BIG)
        sink = sink_ref[hkv * group + gi]
        m = jnp.broadcast_to(jnp.maximum(jnp.max(s, axis=-1, keepdims=True), sink), (w, d))
        p = jnp.concatenate([jnp.exp(s[:, :d] - m), jnp.exp(s[:, d:] - m)], axis=1)
        o = jnp.dot(p.astype(BF16), vb1, preferred_element_type=F32)
        o_ref[0, :, gi * d:(gi + 1) * d] = (o[:, :d] / (o[:, d:] + jnp.exp(sink - m))).astype(o_ref.dtype)


def swa_mixer(proj, col0, heads, kv_heads, sinks, bias):
    bsz, s, _ = proj.shape
    w = WINDOW
    d = HEAD_DIM
    group = heads // kv_heads
    nb = s // w
    q0 = col0 // (group * d)
    k0 = (col0 + heads * d) // d
    v0 = k0 + kv_heads
    assert col0 % (group * d) == 0

    def prev(c0):
        return pl.BlockSpec((1, w, d), lambda b, n, h: (b, jnp.maximum(n - 1, 0), c0 + h))

    def cur(c0):
        return pl.BlockSpec((1, w, d), lambda b, n, h: (b, n, c0 + h))

    return pl.pallas_call(
        functools.partial(_swa_kernel, group=group),
        grid=(bsz, nb, kv_heads),
        in_specs=[pl.BlockSpec(memory_space=pltpu.SMEM),
                  pl.BlockSpec((1, w, group * d), lambda b, n, h: (b, n, q0 + h)),
                  prev(k0), cur(k0), prev(v0), cur(v0),
                  pl.BlockSpec((group, w, 2 * w), lambda b, n, h: (h, 0, 0))],
        out_specs=pl.BlockSpec((1, w, group * d), lambda b, n, h: (b, n, h)),
        out_shape=jax.ShapeDtypeStruct((bsz, s, heads * d), BF16),
        compiler_params=_params("parallel", "parallel", "parallel"),
        name="swa_mixer",
    )(sinks.astype(F32), proj, proj, proj, proj, proj, bias)


def _fox_cumsum_kernel(f_ref, o_ref, *, per_head):
    ls = _log_sigmoid(f_ref[0])
    r = ls.shape[0]
    li = lax.broadcasted_iota(jnp.int32, (128, 128), 0)
    lj = lax.broadcasted_iota(jnp.int32, (128, 128), 1)
    within = jnp.dot(ls, (li <= lj).astype(F32), precision=HIGHEST, preferred_element_type=F32)
    total = jnp.dot(ls, jnp.ones((128, 128), F32), precision=HIGHEST, preferred_element_type=F32)
    ri = lax.broadcasted_iota(jnp.int32, (r, r), 0)
    rj = lax.broadcasted_iota(jnp.int32, (r, r), 1)
    before = ((rj < ri) & (rj // per_head == ri // per_head)).astype(F32)
    o_ref[0] = within + jnp.dot(before, total, precision=HIGHEST, preferred_element_type=F32)


def fox_log_decay(f_logit):
    bsz, s, h = f_logit.shape
    per_head = s // 128
    f = jnp.transpose(f_logit, (0, 2, 1)).reshape(bsz, h * per_head, 128)
    c = pl.pallas_call(
        functools.partial(_fox_cumsum_kernel, per_head=per_head),
        grid=(bsz,),
        in_specs=[pl.BlockSpec((1, h * per_head, 128), lambda b: (b, 0, 0))],
        out_specs=pl.BlockSpec((1, h * per_head, 128), lambda b: (b, 0, 0)),
        out_shape=jax.ShapeDtypeStruct((bsz, h * per_head, 128), F32),
        compiler_params=_params("parallel"),
        name="fox_log_decay",
    )(f)
    return c.reshape(bsz, h, s)


FOX_HEADS_PER_STEP = 8
LOG2E = math.log2(math.e)


def _fox_kernel(q_ref, k_ref, v_ref, ck_ref, o_ref, m_ref, acc_ref, *, tq, tk, hp):
    i = pl.program_id(2)
    j = pl.program_id(3)
    d = HEAD_DIM

    @pl.when(j == 0)
    def _():
        m_ref[...] = jnp.full_like(m_ref, NEG_BIG)
        acc_ref[...] = jnp.zeros_like(acc_ref)

    def step(masked):
        if masked:
            qpos = i * tq + lax.broadcasted_iota(jnp.int32, (tq, tk), 0)
            kpos = j * tk + lax.broadcasted_iota(jnp.int32, (tq, tk), 1)
            keep = kpos <= qpos
        ones = jnp.ones((tk, d), BF16)
        for h in range(hp):
            cols = slice(h * d, (h + 1) * d)
            s = _dot_nt(q_ref[0, :, cols], k_ref[0, :, cols]) - ck_ref[0, h] * LOG2E
            if masked:
                s = jnp.where(keep, s, NEG_BIG)
            m_prev = m_ref[h]
            m_new = jnp.maximum(m_prev, jnp.max(s, axis=-1, keepdims=True))
            a = jnp.exp2(m_prev - m_new)
            p = jnp.concatenate([jnp.exp2(s[:, c * d:(c + 1) * d] - m_new) for c in range(tk // d)], axis=1)
            v1 = jnp.concatenate([v_ref[0, :, cols], ones], axis=1)
            acc_ref[h] = jnp.concatenate([a, a], axis=1) * acc_ref[h] + jnp.dot(
                p.astype(BF16), v1, preferred_element_type=F32)
            m_ref[h] = m_new

    @pl.when(j < i)
    def _():
        step(False)

    @pl.when(j == i)
    def _():
        step(True)
        for h in range(hp):
            acc = acc_ref[h]
            o_ref[0, :, h * d:(h + 1) * d] = (acc[:, :d] / acc[:, d:]).astype(o_ref.dtype)


def fox_mixer(proj, c, heads, t_pref=512):
    bsz, s, _ = proj.shape
    d = HEAD_DIM
    hp = FOX_HEADS_PER_STEP
    assert heads % hp == 0
    t = _tile(s, t_pref, 128)
    nb = s // t
    groups = heads // hp
    c_row = c.reshape(bsz, heads, 1, s)
    return pl.pallas_call(
        functools.partial(_fox_kernel, tq=t, tk=t, hp=hp),
        grid=(bsz, groups, nb, nb),
        in_specs=[pl.BlockSpec((1, t, hp * d), lambda b, h, i, j: (b, i, h)),
                  pl.BlockSpec((1, t, hp * d), lambda b, h, i, j: (b, jnp.minimum(j, i), groups + h)),
                  pl.BlockSpec((1, t, hp * d), lambda b, h, i, j: (b, jnp.minimum(j, i), 2 * groups + h)),
                  pl.BlockSpec((1, hp, 1, t), lambda b, h, i, j: (b, h, 0, jnp.minimum(j, i)))],
        out_specs=pl.BlockSpec((1, t, hp * d), lambda b, h, i, j: (b, i, h)),
        out_shape=jax.ShapeDtypeStruct((bsz, s, heads * d), BF16),
        scratch_shapes=[pltpu.VMEM((hp, t, d), F32), pltpu.VMEM((hp, t, 2 * d), F32)],
        compiler_params=_params("parallel", "parallel", "parallel", "arbitrary"),
        name="fox_mixer",
    )(proj, proj, proj, c_row)


def _router_kernel(x_ref, wt_ref, rb_ref, idx_ref, wts_ref, rank_ref, cnt_ref, carry_ref):
    e = wt_ref.shape[0]
    tm = x_ref.shape[0]
    per = e // N_GROUPS

    @pl.when(pl.program_id(0) == 0)
    def _():
        carry_ref[...] = jnp.zeros_like(carry_ref)

    logits = _dot_nt(wt_ref[...], x_ref[...], precision=HIGHEST)
    scores = _sigmoid(logits)
    biased = scores + rb_ref[...]
    g3 = biased.reshape(N_GROUPS, per, tm)
    mi = lax.broadcasted_iota(jnp.int32, (N_GROUPS, per, tm), 1)
    m1 = jnp.max(g3, axis=1, keepdims=True)
    i1 = jnp.min(jnp.where(g3 == m1, mi, per), axis=1, keepdims=True)
    m2 = jnp.max(jnp.where(mi == i1, -jnp.inf, g3), axis=1, keepdims=True)
    cur = (m1 + m2).reshape(N_GROUPS, tm)
    gi = lax.broadcasted_iota(jnp.int32, (N_GROUPS, tm), 0)
    sel = jnp.zeros((N_GROUPS, tm), jnp.bool_)
    for _ in range(TOPK_GROUPS):
        mx = jnp.max(cur, axis=0, keepdims=True)
        pick = gi == jnp.min(jnp.where(cur == mx, gi, N_GROUPS), axis=0, keepdims=True)
        sel = sel | pick
        cur = jnp.where(pick, -jnp.inf, cur)
    ok = jnp.broadcast_to(sel.reshape(N_GROUPS, 1, tm), (N_GROUPS, per, tm)).reshape(e, tm)
    cur = jnp.where(ok, biased, NEG_BIG)
    ei = lax.broadcasted_iota(jnp.int32, (e, tm), 0)
    chosen = jnp.zeros((e, tm), jnp.bool_)
    picks = []
    for k in range(TOP_K):
        mx = jnp.max(cur, axis=0, keepdims=True)
        ek = jnp.min(jnp.where(cur == mx, ei, e), axis=0, keepdims=True)
        pick = ei == ek
        chosen = chosen | pick
        cur = jnp.where(pick, -jnp.inf, cur)
        picks.append((ek, pick, jnp.sum(jnp.where(pick, scores, 0.0), axis=0, keepdims=True)))
    denom = sum(w for _, _, w in picks)
    si = lax.broadcasted_iota(jnp.int32, (tm, tm), 0)
    ti = lax.broadcasted_iota(jnp.int32, (tm, tm), 1)
    before = jnp.dot(chosen.astype(BF16), (si < ti).astype(BF16), preferred_element_type=F32)
    before = before + carry_ref[...]
    for k, (ek, pick, w) in enumerate(picks):
        idx_ref[k:k + 1, :] = ek
        wts_ref[k:k + 1, :] = w / denom * ROUTED_SCALE
        rank_ref[k:k + 1, :] = jnp.sum(jnp.where(pick, before, 0.0), axis=0, keepdims=True).astype(jnp.int32)
    carry_ref[...] += jnp.sum(chosen.astype(F32), axis=1, keepdims=True)
    cnt_ref[...] = carry_ref[...].astype(jnp.int32)


def router(x, w_router, router_bias, tm_pref=512):
    m, d = x.shape
    e = w_router.shape[1]
    tm = _tile(m, tm_pref, 128)
    pick = pl.BlockSpec((TOP_K, tm), lambda i: (0, i))
    return pl.pallas_call(
        _router_kernel,
        grid=(m // tm,),
        in_specs=[pl.BlockSpec((tm, d), lambda i: (i, 0)),
                  pl.BlockSpec((e, d), lambda i: (0, 0)),
                  pl.BlockSpec((e, 1), lambda i: (0, 0))],
        out_specs=[pick, pick, pick, pl.BlockSpec((e, 1), lambda i: (0, 0))],
        out_shape=[jax.ShapeDtypeStruct((TOP_K, m), jnp.int32), jax.ShapeDtypeStruct((TOP_K, m), F32),
                   jax.ShapeDtypeStruct((TOP_K, m), jnp.int32), jax.ShapeDtypeStruct((e, 1), jnp.int32)],
        scratch_shapes=[pltpu.VMEM((e, 1), F32)],
        compiler_params=_params("arbitrary"),
        name="moe_router",
    )(x, jnp.transpose(w_router).astype(F32), router_bias.reshape(e, 1).astype(F32))


def _dispatch_kernel(dst_ref, x_ref, xs_hbm, sem, *, tm):
    for k in range(TOP_K):
        for t in range(tm):
            pltpu.make_async_copy(x_ref.at[pl.ds(t, 1)], xs_hbm.at[pl.ds(dst_ref[k, t], 1)], sem).start(
                priority=t % 2)
    for k in range(TOP_K):
        pltpu.make_async_copy(x_ref, xs_hbm.at[pl.ds(0, tm)], sem).wait()


def moe_dispatch(xw, dst, tm):
    m, h = xw.shape
    return pl.pallas_call(
        functools.partial(_dispatch_kernel, tm=tm),
        grid=(m // tm,),
        in_specs=[pl.BlockSpec((TOP_K, tm), lambda i: (0, i), memory_space=pltpu.SMEM),
                  pl.BlockSpec((tm, h), lambda i: (i, 0))],
        out_specs=pl.BlockSpec(memory_space=pl.ANY),
        out_shape=jax.ShapeDtypeStruct((TOP_K * m, h), jnp.uint32),
        scratch_shapes=[pltpu.SemaphoreType.DMA(())],
        compiler_params=_params("arbitrary"),
        name="moe_dispatch",
    )(dst, xw)


def _group_metadata(counts, n_rows, t):
    e = counts.shape[0]
    n_items = n_rows // t + e - 1
    lanes = -(-n_items // 128) * 128
    plan, offs = pl.pallas_call(
        functools.partial(_plan_kernel, t=t),
        out_shape=[jax.ShapeDtypeStruct((8, lanes), jnp.int32), jax.ShapeDtypeStruct((e, 1), jnp.int32)],
        name="moe_plan",
    )(counts.reshape(e, 1))
    rows = [plan[r, :n_items] for r in range(6)]
    return (*rows, plan[6, :1]), offs.reshape(e)


def _plan_kernel(cnt_ref, plan_ref, offs_ref, *, t):
    e = cnt_ref.shape[0]
    lanes = plan_ref.shape[1]
    cnt = cnt_ref[...]
    er = lax.broadcasted_iota(jnp.int32, (e, e), 0)
    ec = lax.broadcasted_iota(jnp.int32, (e, e), 1)
    incl = (ec <= er).astype(F32)

    def cumsum(col):
        return jnp.dot(incl, col.astype(F32), precision=HIGHEST, preferred_element_type=F32).astype(jnp.int32)

    ends = cumsum(cnt)
    offs = ends - cnt
    first_tile = offs // t
    tiles_of = jnp.where(cnt > 0, (ends - 1) // t - first_tile + 1, 0)
    item_end = cumsum(tiles_of)
    used = jnp.max(item_end, axis=0, keepdims=True)
    ids = jnp.minimum(lax.broadcasted_iota(jnp.int32, (1, lanes), 1), used - 1)
    exp = jnp.minimum(jnp.sum((item_end <= ids).astype(jnp.int32), axis=0, keepdims=True), e - 1)
    onehot = lax.broadcasted_iota(jnp.int32, (e, lanes), 0) == exp

    def pick(col):
        return jnp.sum(jnp.where(onehot, col, 0), axis=0, keepdims=True)

    tile = pick(first_tile) + ids - (pick(item_end) - pick(tiles_of))
    lo = jnp.clip(pick(offs) - tile * t, 0, t)
    hi = jnp.clip(pick(ends) - tile * t, 0, t)
    lane = lax.broadcasted_iota(jnp.int32, (1, lanes), 1)
    prev_tile = jnp.where(lane == 0, -1, pltpu.roll(tile, 1, axis=1))
    prev_exp = jnp.where(lane == 0, -1, pltpu.roll(exp, 1, axis=1))
    plan_ref[0:1, :] = tile
    plan_ref[1:2, :] = exp
    plan_ref[2:3, :] = lo
    plan_ref[3:4, :] = hi
    plan_ref[4:5, :] = (tile != prev_tile).astype(jnp.int32)
    plan_ref[5:6, :] = (exp != prev_exp).astype(jnp.int32)
    plan_ref[6:7, :] = jnp.broadcast_to(used, (1, lanes))
    plan_ref[7:8, :] = jnp.zeros((1, lanes), jnp.int32)
    offs_ref[...] = offs


def _group_ffn_kernel(tile_ref, exp_ref, lo_ref, hi_ref, first_ref, newexp_ref, used_ref,
                      x_ref, wg_ref, wu_ref, wd_ref, o_ref, wgb, wub, wdb):
    i = pl.program_id(0)
    t, h = x_ref.shape

    @pl.when(i < used_ref[0])
    def _():
        @pl.when(newexp_ref[i] == 1)
        def _():
            wgb[...] = wg_ref[0, 0].astype(BF16)
            wub[...] = wu_ref[0, 0].astype(BF16)
            wdb[...] = wd_ref[0, 0].astype(BF16)

        def ffn(mine):
            left, right = _unpack_halves(x_ref[...])
            left = left.astype(BF16)
            right = right.astype(BF16)
            g = _dot_nt(left, wgb[:, :h]) + _dot_nt(right, wgb[:, h:])
            u = _dot_nt(left, wub[:, :h]) + _dot_nt(right, wub[:, h:])
            a = g * _sigmoid(g) * u
            if mine is not None:
                a = jnp.where(mine, a, 0.0)
            return _pack_halves(jnp.dot(a.astype(BF16), wdb[...], preferred_element_type=F32))

        whole = (lo_ref[i] == 0) & (hi_ref[i] == t)

        @pl.when(whole)
        def _():
            o_ref[...] = ffn(None)

        @pl.when(jnp.logical_not(whole))
        def _():
            row = lax.broadcasted_iota(jnp.int32, (t, 1), 0)
            mine = (row >= lo_ref[i]) & (row < hi_ref[i])
            y = ffn(mine)

            @pl.when(first_ref[i] == 1)
            def _():
                o_ref[...] = y

            @pl.when(first_ref[i] == 0)
            def _():
                o_ref[...] = jnp.where(mine, y, o_ref[...])


def moe_group_ffn(xs, meta, wg, wu, wd, layer, t):
    r, h = xs.shape
    _, e, f, d = wg.shape
    n_items = r // t + e - 1
    grid_spec = pltpu.PrefetchScalarGridSpec(
        num_scalar_prefetch=7,
        grid=(n_items,),
        in_specs=[pl.BlockSpec((t, h), lambda i, tile, *_: (tile[i], 0)),
                  pl.BlockSpec((1, 1, f, d), lambda i, tile, exp, *_: (layer, exp[i], 0, 0)),
                  pl.BlockSpec((1, 1, f, d), lambda i, tile, exp, *_: (layer, exp[i], 0, 0)),
                  pl.BlockSpec((1, 1, f, d), lambda i, tile, exp, *_: (layer, exp[i], 0, 0))],
        out_specs=pl.BlockSpec((t, h), lambda i, tile, *_: (tile[i], 0)),
        scratch_shapes=[pltpu.VMEM((f, d), BF16), pltpu.VMEM((f, d), BF16), pltpu.VMEM((f, d), BF16)],
    )
    return pl.pallas_call(
        _group_ffn_kernel,
        grid_spec=grid_spec,
        out_shape=jax.ShapeDtypeStruct((r, h), jnp.uint32),
        compiler_params=_params("arbitrary"),
        name="moe_group_ffn",
    )(*meta, xs, wg, wu, wd)


def _combine_kernel(dst_ref, nxt_ref, wts_ref, x_ref, xb_ref, sg_ref, su_ref, sd_ref, g_ref, b_ref, ys_hbm,
                    o_ref, ob_ref, buf, sem, *, alpha, tm, steps):
    i = pl.program_id(0)
    h = buf.shape[-1]

    def gather_pick(idx_ref, s, k):
        for t in range(tm):
            pltpu.make_async_copy(ys_hbm.at[pl.ds(idx_ref[k, t], 1)], buf.at[s, k, pl.ds(t, 1)], sem.at[s]).start()

    def wait_all(s):
        for k in range(TOP_K):
            pltpu.make_async_copy(ys_hbm.at[pl.ds(0, tm)], buf.at[s, k], sem.at[s]).wait()

    @pl.when(i == 0)
    def _():
        for k in range(TOP_K):
            gather_pick(dst_ref, 0, k)

    def step(s):
        wait_all(s)
        acc_l = jnp.zeros((tm, h), F32)
        acc_r = jnp.zeros((tm, h), F32)
        wts = jnp.transpose(wts_ref[...])
        for k in range(TOP_K):
            gather_pick(nxt_ref, 1 - s, k)
            left, right = _unpack_halves(buf[s, k])
            c = wts[:, k:k + 1]
            acc_l += c * left
            acc_r += c * right
        xb = xb_ref[...]
        hs = jnp.dot(xb, sg_ref[...], preferred_element_type=F32)
        hs = hs * _sigmoid(hs) * jnp.dot(xb, su_ref[...], preferred_element_type=F32)
        y = jnp.concatenate([acc_l, acc_r], axis=1) + jnp.dot(hs.astype(BF16), sd_ref[...],
                                                              preferred_element_type=F32)
        out = _layer_norm(alpha * x_ref[...] + y, g_ref[...], b_ref[...])
        o_ref[...] = out
        ob_ref[...] = out.astype(BF16)

        @pl.when(i == steps - 1)
        def _():
            wait_all(1 - s)

    for s in range(2):
        @pl.when(i % 2 == s)
        def _(s=s):
            step(s)


def moe_combine_ln(ys, dst_tiles, wts, x, xb, sg, su, sd, g, b, alpha, tm):
    m, d = x.shape
    h = ys.shape[1]
    fs = sg.shape[1]
    steps = m // tm
    row = pl.BlockSpec((tm, d), lambda i: (i, 0))
    vec = pl.BlockSpec((1, d), lambda i: (0, 0))
    return pl.pallas_call(
        functools.partial(_combine_kernel, alpha=alpha, tm=tm, steps=steps),
        grid=(steps,),
        in_specs=[pl.BlockSpec((TOP_K, tm), lambda i: (0, i), memory_space=pltpu.SMEM),
                  pl.BlockSpec((TOP_K, tm), lambda i: (0, jnp.minimum(i + 1, steps - 1)), memory_space=pltpu.SMEM),
                  pl.BlockSpec((TOP_K, tm), lambda i: (0, i)),
                  row, row,
                  pl.BlockSpec((d, fs), lambda i: (0, 0)),
                  pl.BlockSpec((d, fs), lambda i: (0, 0)),
                  pl.BlockSpec((fs, d), lambda i: (0, 0)),
                  vec, vec,
                  pl.BlockSpec(memory_space=pl.ANY)],
        out_specs=[row, row],
        out_shape=[jax.ShapeDtypeStruct((m, d), F32), jax.ShapeDtypeStruct((m, d), BF16)],
        scratch_shapes=[pltpu.VMEM((2, TOP_K, tm, h), jnp.uint32), pltpu.SemaphoreType.DMA((2,))],
        compiler_params=_params("arbitrary"),
        name="moe_combine_ln",
    )(dst_tiles, dst_tiles, wts, x, xb, sg, su, sd, g.reshape(1, d), b.reshape(1, d), ys)


def moe_block(xf, xb, xw, w_router, router_bias, wg, wu, wd, layer, sg, su, sd, g, b, alpha,
              t_rows=256, tm_dispatch=128, tm_combine=128):
    m = xf.shape[0]
    e = w_router.shape[1]
    t_rows = _tile(TOP_K * m, t_rows, 8)
    tm_dispatch = _tile(m, tm_dispatch, 128)
    tm_combine = _tile(m, tm_combine, 128)
    idx, wts, rank, counts = router(xf, w_router, router_bias)
    meta, offs = _group_metadata(counts.reshape(-1), TOP_K * m, t_rows)
    start = jnp.sum(jnp.where(idx[:, None, :] == jnp.arange(e, dtype=jnp.int32)[None, :, None],
                              offs[None, :, None], 0), axis=1)
    dst = (start + rank).astype(jnp.int32)
    xs = moe_dispatch(xw, dst, tm_dispatch)
    ys = moe_group_ffn(xs, meta, jnp.swapaxes(wg, 2, 3), jnp.swapaxes(wu, 2, 3), wd, layer, t_rows)
    return moe_combine_ln(ys, dst, wts, xf, xb, sg.astype(BF16), su.astype(BF16), sd.astype(BF16),
                          g, b, alpha, tm_combine)


def kernel(x, w_in_ab, b_in_ab, w_in_cd, b_in_cd, hgrn_lb_logits, hgrn_norm, sinks, rel_bias, gla_w2, gla_b,
           gla_norm, w_out, ln_g, ln_b, w_router, router_bias, w_exp_gate, w_exp_up, w_exp_down, w_sh_gate,
           w_sh_up, w_sh_down):
    bsz, s, d = x.shape
    m = bsz * s
    depth = w_out.shape[0]
    alpha = (2.0 * depth) ** 0.25
    half = d // 2
    a_heads = half // HEAD_DIM
    b_heads = half // HEAD_DIM
    b_kv_heads = b_heads // 4
    c_dk = (d // 4) // C_HEADS
    c_dv = half // C_HEADS
    d_heads = half // HEAD_DIM
    rank = gla_w2.shape[1]
    c_cols = 2 * C_HEADS * c_dk + 2 * C_HEADS * c_dv
    d_cols = 3 * d_heads * HEAD_DIM

    lbs = lower_bounds(hgrn_lb_logits)
    bias = band_bias(rel_bias)

    xf = x.reshape(m, d).astype(F32)
    xb = xf.astype(BF16)
    for l in range(depth):
        j = l // 2
        if l % 2 == 0:
            proj = matmul_bias_layer(xb, w_in_ab, j, w_in_ab.shape[2], b_in_ab[j], F32).reshape(bsz, s, -1)
            o1 = hgrn2_mixer(proj, lbs[l], hgrn_norm[j], a_heads)
            o2 = swa_mixer(proj, 4 * a_heads * HEAD_DIM, b_heads, b_kv_heads, sinks[j], bias)
        else:
            w_in = w_in_cd[j]
            b_in = b_in_cd[j]
            d0 = c_cols + rank
            f0 = d0 + d_cols
            proj_c = matmul_bias(xb, w_in[:, :c_cols].astype(BF16), b_in[:c_cols], F32).reshape(bsz, s, -1)
            qfold = jnp.where(jnp.arange(d_cols) < d_heads * HEAD_DIM, LOG2E * HEAD_DIM ** -0.5, 1.0).astype(F32)
            proj_d = matmul_bias(xb, (w_in[:, d0:f0] * qfold).astype(BF16), b_in[d0:f0] * qfold,
                                 BF16).reshape(bsz, s, -1)
            w_small = jnp.concatenate([w_in[:, c_cols:d0], w_in[:, f0:]], axis=1).astype(BF16)
            b_small = jnp.concatenate([b_in[c_cols:d0], b_in[f0:]])
            small = matmul_bias(xb, w_small, b_small, F32).reshape(bsz, s, -1)
            o1 = gla_mixer(proj_c, small[..., :rank], gla_w2[j].astype(F32), gla_b[j].astype(F32), gla_norm[j],
                           C_HEADS, c_dk, c_dv)
            o2 = fox_mixer(proj_d, fox_log_decay(small[..., rank:]), d_heads)
        mix = matmul_pair(o1.reshape(m, half), o2.reshape(m, half), w_out, l, F32)
        xf, xb, xw = add_layer_norm(xf, mix, ln_g[l, 0], ln_b[l, 0], alpha)
        xf, xb = moe_block(xf, xb, xw, w_router[l], router_bias[l], w_exp_gate, w_exp_up, w_exp_down, l,
                           w_sh_gate[l], w_sh_up[l], w_sh_down[l], ln_g[l, 1], ln_b[l, 1], alpha)
    return xf.reshape(bsz, s, d).astype(x.dtype)
```

```python
import functools
import math

import numpy as np
import jax
import jax.numpy as jnp
from jax import lax
from jax.experimental import pallas as pl
from jax.experimental.pallas import tpu as pltpu

F32 = jnp.float32
BF16 = jnp.bfloat16
HIGHEST = lax.Precision.HIGHEST

HEAD_DIM = 128
WINDOW = 128
C_HEADS = 4
GLA_TAU = 16.0
CHUNK = 64
SUB = 16
N_BUCKETS = 32
MAX_DISTANCE = 128
TOP_K = 8
N_GROUPS = 8
TOPK_GROUPS = 4
ROUTED_SCALE = 2.5
LN_EPS = 1e-5
RMS_EPS = 1e-6
NEG_BIG = -1e30

VMEM_LIMIT_BYTES = 52 * 1024 * 1024


def _params(*sem):
    return pltpu.CompilerParams(dimension_semantics=sem, vmem_limit_bytes=VMEM_LIMIT_BYTES)


def _tile(n, pref, align):
    t = min(pref, n)
    t -= t % align
    while t >= align:
        if n % t == 0:
            return t
        t -= align
    return n


def _sigmoid(x):
    return 1.0 / (1.0 + jnp.exp(-x))


def _log_sigmoid(x):
    return jnp.minimum(x, 0.0) - jnp.log(1.0 + jnp.exp(-jnp.abs(x)))


def _dot_nt(a, b, **kw):
    return lax.dot_general(a, b, (((1,), (1,)), ((), ())), preferred_element_type=F32, **kw)


def _dot_tn(a, b, **kw):
    return lax.dot_general(a, b, (((0,), (0,)), ((), ())), preferred_element_type=F32, **kw)


def _mm_kernel(x_ref, w_ref, b_ref, o_ref):
    acc = jnp.dot(x_ref[...], w_ref[...], preferred_element_type=F32)
    o_ref[...] = (acc + b_ref[...]).astype(o_ref.dtype)


def matmul_bias(x, w, b, out_dtype, tm_pref=512, tn_pref=1024):
    m, k = x.shape
    n = w.shape[1]
    tm = _tile(m, tm_pref, 8)
    tn = _tile(n, tn_pref, 128)
    return pl.pallas_call(
        _mm_kernel,
        grid=(n // tn, m // tm),
        in_specs=[pl.BlockSpec((tm, k), lambda j, i: (i, 0)),
                  pl.BlockSpec((k, tn), lambda j, i: (0, j)),
                  pl.BlockSpec((1, tn), lambda j, i: (0, j))],
        out_specs=pl.BlockSpec((tm, tn), lambda j, i: (i, j)),
        out_shape=jax.ShapeDtypeStruct((m, n), out_dtype),
        compiler_params=_params("parallel", "parallel"),
        name="matmul_bias",
    )(x, w, b.reshape(1, n).astype(F32))


def _mm_layer_kernel(x_ref, w_ref, b_ref, o_ref, wb_ref):
    @pl.when(pl.program_id(1) == 0)
    def _():
        wb_ref[...] = w_ref[0].astype(BF16)

    acc = jnp.dot(x_ref[...], wb_ref[...], preferred_element_type=F32)
    o_ref[...] = (acc + b_ref[...]).astype(o_ref.dtype)


def matmul_bias_layer(x, w_stack, layer, n_cols, b, out_dtype, tm_pref=512, tn_pref=1024):
    m, k = x.shape
    tm = _tile(m, tm_pref, 8)
    tn = _tile(n_cols, tn_pref, 128)
    return pl.pallas_call(
        _mm_layer_kernel,
        grid=(n_cols // tn, m // tm),
        in_specs=[pl.BlockSpec((tm, k), lambda j, i: (i, 0)),
                  pl.BlockSpec((1, k, tn), lambda j, i: (layer, 0, j), pipeline_mode=pl.Buffered(1)),
                  pl.BlockSpec((1, tn), lambda j, i: (0, j))],
        out_specs=pl.BlockSpec((tm, tn), lambda j, i: (i, j)),
        out_shape=jax.ShapeDtypeStruct((m, n_cols), out_dtype),
        scratch_shapes=[pltpu.VMEM((k, tn), BF16)],
        compiler_params=_params("parallel", "arbitrary"),
        name="matmul_bias_layer",
    )(x, w_stack, b.reshape(1, n_cols).astype(F32))


def _mm2_kernel(a1_ref, a2_ref, w1_ref, w2_ref, o_ref, w1b_ref, w2b_ref):
    @pl.when(pl.program_id(1) == 0)
    def _():
        w1b_ref[...] = w1_ref[0].astype(BF16)
        w2b_ref[...] = w2_ref[0].astype(BF16)

    acc = jnp.dot(a1_ref[...], w1b_ref[...], preferred_element_type=F32)
    acc += jnp.dot(a2_ref[...], w2b_ref[...], preferred_element_type=F32)
    o_ref[...] = acc.astype(o_ref.dtype)


def matmul_pair(a1, a2, w_stack, layer, out_dtype, tm_pref=512, tn_pref=1024):
    m, k1 = a1.shape
    assert a2.shape[1] == k1 and w_stack.shape[1] == 2 * k1
    n = w_stack.shape[2]
    tm = _tile(m, tm_pref, 8)
    tn = _tile(n, tn_pref, 128)
    return pl.pallas_call(
        _mm2_kernel,
        grid=(n // tn, m // tm),
        in_specs=[pl.BlockSpec((tm, k1), lambda j, i: (i, 0)),
                  pl.BlockSpec((tm, k1), lambda j, i: (i, 0)),
                  pl.BlockSpec((1, k1, tn), lambda j, i: (layer, 0, j), pipeline_mode=pl.Buffered(1)),
                  pl.BlockSpec((1, k1, tn), lambda j, i: (layer, 1, j), pipeline_mode=pl.Buffered(1))],
        out_specs=pl.BlockSpec((tm, tn), lambda j, i: (i, j)),
        out_shape=jax.ShapeDtypeStruct((m, n), out_dtype),
        scratch_shapes=[pltpu.VMEM((k1, tn), BF16), pltpu.VMEM((k1, tn), BF16)],
        compiler_params=_params("parallel", "arbitrary"),
        name="matmul_pair",
    )(a1, a2, w_stack, w_stack)


def _pack_halves(v):
    h = v.shape[1] // 2
    r = v.astype(BF16).astype(F32)
    lo = lax.bitcast_convert_type(r[:, :h], jnp.uint32) >> 16
    hi = lax.bitcast_convert_type(r[:, h:], jnp.uint32) & jnp.uint32(0xFFFF0000)
    return lo | hi


def _unpack_halves(w):
    left = lax.bitcast_convert_type(w << 16, F32)
    right = lax.bitcast_convert_type(w & jnp.uint32(0xFFFF0000), F32)
    return left, right


def _layer_norm(z, g, b):
    mu = jnp.mean(z, axis=-1, keepdims=True)
    zc = z - mu
    var = jnp.mean(zc * zc, axis=-1, keepdims=True)
    return zc * lax.rsqrt(var + LN_EPS) * g + b


def _add_ln_kernel(x_ref, y_ref, g_ref, b_ref, o_ref, ob_ref, ow_ref, *, alpha):
    out = _layer_norm(alpha * x_ref[...] + y_ref[...], g_ref[...], b_ref[...])
    o_ref[...] = out
    ob_ref[...] = out.astype(BF16)
    ow_ref[...] = _pack_halves(out)


def add_layer_norm(x, y, g, b, alpha, tm_pref=256):
    m, d = x.shape
    tm = _tile(m, tm_pref, 16)
    row = pl.BlockSpec((tm, d), lambda i: (i, 0))
    half = pl.BlockSpec((tm, d // 2), lambda i: (i, 0))
    vec = pl.BlockSpec((1, d), lambda i: (0, 0))
    return pl.pallas_call(
        functools.partial(_add_ln_kernel, alpha=alpha),
        grid=(m // tm,),
        in_specs=[row, row, vec, vec],
        out_specs=[row, row, half],
        out_shape=[jax.ShapeDtypeStruct((m, d), F32), jax.ShapeDtypeStruct((m, d), BF16),
                   jax.ShapeDtypeStruct((m, d // 2), jnp.uint32)],
        compiler_params=_params("parallel"),
        name="add_layer_norm",
    )(x, y, g.reshape(1, d), b.reshape(1, d))


def _lower_bounds_kernel(z_ref, o_ref):
    z = z_ref[...]
    depth = z.shape[0]
    e = jnp.exp(z - jnp.max(z, axis=0, keepdims=True))
    p = e / jnp.sum(e, axis=0, keepdims=True)
    run = jnp.zeros_like(p[0:1])
    for l in range(depth):
        run = run + p[l:l + 1]
        o_ref[l:l + 1, :] = run - p[0:1]


def lower_bounds(logits):
    return pl.pallas_call(
        _lower_bounds_kernel,
        out_shape=jax.ShapeDtypeStruct(logits.shape, F32),
        name="hgrn_lower_bounds",
    )(logits.astype(F32))


def _gla_chunk(q, k, v, g, st_ref):
    dk = q.shape[1]
    nsub = CHUNK // SUB
    row = lax.broadcasted_iota(jnp.int32, (CHUNK, CHUNK), 0)
    col = lax.broadcasted_iota(jnp.int32, (CHUNK, CHUNK), 1)
    tri = (col <= row).astype(F32)
    b = jnp.dot(tri, g, precision=HIGHEST, preferred_element_type=F32)
    b_last = b[CHUNK - 1:CHUNK, :]
    st = st_ref[...]

    qe = q * jnp.exp(b)
    o = _dot_nt(qe.astype(BF16), st.astype(BF16))

    levels = [jnp.zeros((1, dk), F32)] + [b[SUB * i - 1:SUB * i, :] for i in range(1, nsub)]
    lvl_rows = jnp.concatenate([jnp.broadcast_to(r, (SUB, dk)) for r in levels], axis=0)
    qt = q * jnp.exp(b - lvl_rows)
    row_blk = lax.broadcasted_iota(jnp.int32, (CHUNK, dk), 0) // SUB
    qhat = jnp.concatenate([jnp.where(row_blk == i, qt, 0.0) for i in range(1, nsub)], axis=1)
    khat = jnp.concatenate([k * jnp.exp(jnp.minimum(levels[i] - b, 0.0)) for i in range(1, nsub)], axis=1)
    p = _dot_nt(qhat.astype(BF16), khat.astype(BF16))
    p = jnp.where(col < (row // SUB) * SUB, p, 0.0)
    o += jnp.dot(p.astype(BF16), v.astype(BF16), preferred_element_type=F32)

    b4 = b.reshape(nsub, SUB, dk)
    q4 = q.reshape(nsub, SUB, dk)
    k4 = k.reshape(nsub, SUB, dk)
    ti = lax.broadcasted_iota(jnp.int32, (1, SUB, SUB, 1), 1)
    si = lax.broadcasted_iota(jnp.int32, (1, SUB, SUB, 1), 2)
    diff = b4[:, :, None, :] - b4[:, None, :, :]
    dec = jnp.exp(jnp.where(si <= ti, diff, NEG_BIG))
    dg = jnp.sum(q4[:, :, None, :] * k4[:, None, :, :] * dec, axis=-1)
    v4 = v.reshape(nsub, SUB, v.shape[1])
    od = jnp.einsum("its,isv->itv", dg.astype(BF16), v4.astype(BF16), preferred_element_type=F32)
    o += od.reshape(CHUNK, v.shape[1])

    kd = k * jnp.exp(b_last - b)
    st_ref[...] = st * jnp.exp(b_last) + _dot_tn(v.astype(BF16), kd.astype(BF16))
    return o


HGRN_HEADS_PER_STEP = 4
GLA_HEADS_PER_STEP = 2
CHUNK_UNROLL = 2


def _hgrn_kernel(q_ref, f_ref, i_ref, gate_ref, lb_ref, nw_ref, o_ref, st_ref, *, chunks, hp):
    @pl.when(pl.program_id(2) == 0)
    def _():
        st_ref[...] = jnp.zeros_like(st_ref)

    d = HEAD_DIM
    nw = nw_ref[...]

    def body(c, carry):
        rows = pl.ds(pl.multiple_of(c * CHUNK, CHUNK), CHUNK)
        for h in range(hp):
            cols = slice(h * d, (h + 1) * d)
            lb = lb_ref[:, cols]
            qa = q_ref[0, rows, cols]
            f = lb + (1.0 - lb) * _sigmoid(f_ref[0, rows, cols])
            o = _gla_chunk(qa * _sigmoid(qa), 1.0 - f, i_ref[0, rows, cols], jnp.log(f), st_ref.at[h])
            o = o * lax.rsqrt(jnp.mean(o * o, axis=-1, keepdims=True) + RMS_EPS) * nw
            o_ref[0, rows, cols] = (o * _sigmoid(gate_ref[0, rows, cols])).astype(o_ref.dtype)
        return carry

    lax.fori_loop(0, chunks, body, 0, unroll=CHUNK_UNROLL)


def hgrn2_mixer(proj, lb, norm_w, heads, t_pref=512):
    bsz, s, _ = proj.shape
    d = HEAD_DIM
    hp = HGRN_HEADS_PER_STEP
    assert heads % hp == 0
    groups = heads // hp
    t = _tile(s, t_pref, CHUNK)

    def col(group):
        return pl.BlockSpec((1, t, hp * d), lambda b, h, i, group=group: (b, i, group * groups + h))

    return pl.pallas_call(
        functools.partial(_hgrn_kernel, chunks=t // CHUNK, hp=hp),
        grid=(bsz, groups, s // t),
        in_specs=[col(0), col(1), col(2), col(3),
                  pl.BlockSpec((1, hp * d), lambda b, h, i: (0, h)),
                  pl.BlockSpec((1, d), lambda b, h, i: (0, 0))],
        out_specs=pl.BlockSpec((1, t, hp * d), lambda b, h, i: (b, i, h)),
        out_shape=jax.ShapeDtypeStruct((bsz, s, heads * d), BF16),
        scratch_shapes=[pltpu.VMEM((hp, d, d), F32)],
        compiler_params=_params("parallel", "parallel", "arbitrary"),
        name="hgrn2_mixer",
    )(proj, proj, proj, proj, lb.reshape(1, heads * d), norm_w.reshape(1, d))


def _gla_kernel(q_ref, k_ref, v_ref, gate_ref, a_ref, w2_ref, gb_ref, nw_ref, o_ref, st_ref,
                *, chunks, scale, hp, dk, dv):
    @pl.when(pl.program_id(2) == 0)
    def _():
        st_ref[...] = jnp.zeros_like(st_ref)

    nw = nw_ref[...]

    def body(c, carry):
        rows = pl.ds(pl.multiple_of(c * CHUNK, CHUNK), CHUNK)
        a = a_ref[0, rows, :]
        for h in range(hp):
            kc = slice(h * dk, (h + 1) * dk)
            vc = slice(h * dv, (h + 1) * dv)
            logit = jnp.dot(a, w2_ref[:, kc], precision=HIGHEST, preferred_element_type=F32) + gb_ref[:, kc]
            g = _log_sigmoid(logit) * (1.0 / GLA_TAU)
            o = _gla_chunk(q_ref[0, rows, kc] * scale, k_ref[0, rows, kc], v_ref[0, rows, vc], g, st_ref.at[h])
            o = o * lax.rsqrt(jnp.mean(o * o, axis=-1, keepdims=True) + RMS_EPS) * nw
            gate = gate_ref[0, rows, vc]
            o_ref[0, rows, vc] = (o * gate * _sigmoid(gate)).astype(o_ref.dtype)
        return carry

    lax.fori_loop(0, chunks, body, 0, unroll=CHUNK_UNROLL)


def gla_mixer(proj, a, w2, gb, norm_w, heads, dk, dv, t_pref=512):
    bsz, s, _ = proj.shape
    rank = a.shape[-1]
    t = _tile(s, t_pref, CHUNK)
    hp = GLA_HEADS_PER_STEP
    assert heads % hp == 0 and (2 * heads * dk) % (hp * dv) == 0
    groups = heads // hp
    v0 = 2 * heads * dk // (hp * dv)
    return pl.pallas_call(
        functools.partial(_gla_kernel, chunks=t // CHUNK, scale=dk ** -0.5, hp=hp, dk=dk, dv=dv),
        grid=(bsz, groups, s // t),
        in_specs=[pl.BlockSpec((1, t, hp * dk), lambda b, h, i: (b, i, h)),
                  pl.BlockSpec((1, t, hp * dk), lambda b, h, i: (b, i, groups + h)),
                  pl.BlockSpec((1, t, hp * dv), lambda b, h, i: (b, i, v0 + h)),
                  pl.BlockSpec((1, t, hp * dv), lambda b, h, i: (b, i, v0 + groups + h)),
                  pl.BlockSpec((1, t, rank), lambda b, h, i: (b, i, 0)),
                  pl.BlockSpec((rank, hp * dk), lambda b, h, i: (0, h)),
                  pl.BlockSpec((1, hp * dk), lambda b, h, i: (0, h)),
                  pl.BlockSpec((1, dv), lambda b, h, i: (0, 0))],
        out_specs=pl.BlockSpec((1, t, hp * dv), lambda b, h, i: (b, i, h)),
        out_shape=jax.ShapeDtypeStruct((bsz, s, heads * dv), BF16),
        scratch_shapes=[pltpu.VMEM((hp, dv, dk), F32)],
        compiler_params=_params("parallel", "parallel", "arbitrary"),
        name="gla_mixer",
    )(proj, proj, proj, proj, a, w2, gb.reshape(1, heads * dk), norm_w.reshape(1, dv))


def _t5_bucket_table():
    dist = WINDOW + np.arange(WINDOW)[:, None] - np.arange(2 * WINDOW)[None, :]
    max_exact = N_BUCKETS // 2
    d = np.maximum(dist, 0)
    large = max_exact + (np.log(np.maximum(d, 1).astype(np.float32) / max_exact)
                         / math.log(MAX_DISTANCE / max_exact) * (N_BUCKETS - max_exact)).astype(np.int32)
    large = np.minimum(large, N_BUCKETS - 1)
    return np.where(d < max_exact, d, large).astype(np.int32)


def _band_bias_kernel(rel_ref, bucket_ref, o_ref):
    h = pl.program_id(0)
    bucket = bucket_ref[...]
    acc = jnp.zeros(bucket.shape, F32)
    for n in range(N_BUCKETS):
        acc = jnp.where(bucket == n, rel_ref[n, h], acc)
    o_ref[0] = acc


def band_bias(rel_bias):
    heads = rel_bias.shape[1]
    bucket = jnp.asarray(_t5_bucket_table())
    return pl.pallas_call(
        _band_bias_kernel,
        grid=(heads,),
        in_specs=[pl.BlockSpec(memory_space=pltpu.SMEM),
                  pl.BlockSpec((WINDOW, 2 * WINDOW), lambda h: (0, 0))],
        out_specs=pl.BlockSpec((1, WINDOW, 2 * WINDOW), lambda h: (h, 0, 0)),
        out_shape=jax.ShapeDtypeStruct((heads, WINDOW, 2 * WINDOW), F32),
        name="t5_band_bias",
    )(rel_bias.astype(F32), bucket)


def _swa_kernel(sink_ref, q_ref, kp_ref, kc_ref, vp_ref, vc_ref, bias_ref, o_ref, *, group):
    n = pl.program_id(1)
    hkv = pl.program_id(2)
    w = WINDOW
    d = HEAD_DIM
    scale = d ** -0.5
    kb = jnp.concatenate([kp_ref[0], kc_ref[0]], axis=0).astype(BF16)
    vb = jnp.concatenate([vp_ref[0], vc_ref[0]], axis=0).astype(BF16)
    vb1 = jnp.concatenate([vb, jnp.ones((2 * w, d), BF16)], axis=1)
    qi = lax.broadcasted_iota(jnp.int32, (w, 2 * w), 0)
    ki = lax.broadcasted_iota(jnp.int32, (w, 2 * w), 1)
    dist = w + qi - ki
    mask = (dist >= 0) & (dist < w) & ((n * w - w + ki) >= 0)
    for gi in range(group):
        q = q_ref[0, :, gi * d:(gi + 1) * d].astype(BF16)
        s = _dot_nt(q, kb) * scale + bias_ref[gi]
        s = jnp.where(mask, s, NEG_BIG)
        sink = sink_ref[hkv * group + gi]
        m = jnp.broadcast_to(jnp.maximum(jnp.max(s, axis=-1, keepdims=True), sink), (w, d))
        p = jnp.concatenate([jnp.exp(s[:, :d] - m), jnp.exp(s[:, d:] - m)], axis=1)
        o = jnp.dot(p.astype(BF16), vb1, preferred_element_type=F32)
        o_ref[0, :, gi * d:(gi + 1) * d] = (o[:, :d] / (o[:, d:] + jnp.exp(sink - m))).astype(o_ref.dtype)


def swa_mixer(proj, col0, heads, kv_heads, sinks, bias):
    bsz, s, _ = proj.shape
    w = WINDOW
    d = HEAD_DIM
    group = heads // kv_heads
    nb = s // w
    q0 = col0 // (group * d)
    k0 = (col0 + heads * d) // d
    v0 = k0 + kv_heads
    assert col0 % (group * d) == 0

    def prev(c0):
        return pl.BlockSpec((1, w, d), lambda b, n, h: (b, jnp.maximum(n - 1, 0), c0 + h))

    def cur(c0):
        return pl.BlockSpec((1, w, d), lambda b, n, h: (b, n, c0 + h))

    return pl.pallas_call(
        functools.partial(_swa_kernel, group=group),
        grid=(bsz, nb, kv_heads),
        in_specs=[pl.BlockSpec(memory_space=pltpu.SMEM),
                  pl.BlockSpec((1, w, group * d), lambda b, n, h: (b, n, q0 + h)),
                  prev(k0), cur(k0), prev(v0), cur(v0),
                  pl.BlockSpec((group, w, 2 * w), lambda b, n, h: (h, 0, 0))],
        out_specs=pl.BlockSpec((1, w, group * d), lambda b, n, h: (b, n, h)),
        out_shape=jax.ShapeDtypeStruct((bsz, s, heads * d), BF16),
        compiler_params=_params("parallel", "parallel", "parallel"),
        name="swa_mixer",
    )(sinks.astype(F32), proj, proj, proj, proj, proj, bias)


def _fox_cumsum_kernel(f_ref, o_ref, *, per_head):
    ls = _log_sigmoid(f_ref[0])
    r = ls.shape[0]
    li = lax.broadcasted_iota(jnp.int32, (128, 128), 0)
    lj = lax.broadcasted_iota(jnp.int32, (128, 128), 1)
    within = jnp.dot(ls, (li <= lj).astype(F32), precision=HIGHEST, preferred_element_type=F32)
    total = jnp.dot(ls, jnp.ones((128, 128), F32), precision=HIGHEST, preferred_element_type=F32)
    ri = lax.broadcasted_iota(jnp.int32, (r, r), 0)
    rj = lax.broadcasted_iota(jnp.int32, (r, r), 1)
    before = ((rj < ri) & (rj // per_head == ri // per_head)).astype(F32)
    o_ref[0] = within + jnp.dot(before, total, precision=HIGHEST, preferred_element_type=F32)


def fox_log_decay(f_logit):
    bsz, s, h = f_logit.shape
    per_head = s // 128
    f = jnp.transpose(f_logit, (0, 2, 1)).reshape(bsz, h * per_head, 128)
    c = pl.pallas_call(
        functools.partial(_fox_cumsum_kernel, per_head=per_head),
        grid=(bsz,),
        in_specs=[pl.BlockSpec((1, h * per_head, 128), lambda b: (b, 0, 0))],
        out_specs=pl.BlockSpec((1, h * per_head, 128), lambda b: (b, 0, 0)),
        out_shape=jax.ShapeDtypeStruct((bsz, h * per_head, 128), F32),
        compiler_params=_params("parallel"),
        name="fox_log_decay",
    )(f)
    return c.reshape(bsz, h, s)


FOX_HEADS_PER_STEP = 8
LOG2E = math.log2(math.e)


def _fox_kernel(qi_ref, kj_ref, q_ref, k_ref, v_ref, ck_ref, o_ref, m_ref, acc_ref, *, tq, tk, hp):
    pair = pl.program_id(2)
    i = qi_ref[pair]
    j = kj_ref[pair]
    d = HEAD_DIM

    @pl.when(j == 0)
    def _():
        m_ref[...] = jnp.full_like(m_ref, NEG_BIG)
        acc_ref[...] = jnp.zeros_like(acc_ref)

    def step(masked):
        if masked:
            qpos = i * tq + lax.broadcasted_iota(jnp.int32, (tq, tk), 0)
            kpos = j * tk + lax.broadcasted_iota(jnp.int32, (tq, tk), 1)
            keep = kpos <= qpos
        ones = jnp.ones((tk, d), BF16)
        for h in range(hp):
            cols = slice(h * d, (h + 1) * d)
            s = _dot_nt(q_ref[0, :, cols], k_ref[0, :, cols]) - ck_ref[0, h] * LOG2E
            if masked:
                s = jnp.where(keep, s, NEG_BIG)
            m_prev = m_ref[h]
            m_new = jnp.maximum(m_prev, jnp.max(s, axis=-1, keepdims=True))
            a = jnp.exp2(m_prev - m_new)
            p = jnp.concatenate([jnp.exp2(s[:, c * d:(c + 1) * d] - m_new) for c in range(tk // d)], axis=1)
            v1 = jnp.concatenate([v_ref[0, :, cols], ones], axis=1)
            acc_ref[h] = jnp.concatenate([a, a], axis=1) * acc_ref[h] + jnp.dot(
                p.astype(BF16), v1, preferred_element_type=F32)
            m_ref[h] = m_new

    @pl.when(j < i)
    def _():
        step(False)

    @pl.when(j == i)
    def _():
        step(True)
        for h in range(hp):
            acc = acc_ref[h]
            o_ref[0, :, h * d:(h + 1) * d] = (acc[:, :d] / acc[:, d:]).astype(o_ref.dtype)


def fox_mixer(proj, c, heads, t_pref=512):
    bsz, s, _ = proj.shape
    d = HEAD_DIM
    hp = FOX_HEADS_PER_STEP
    assert heads % hp == 0
    t = _tile(s, t_pref, 128)
    nb = s // t
    groups = heads // hp
    c_row = c.reshape(bsz, heads, 1, s)
    pairs = [(i, j) for i in range(nb) for j in range(i + 1)]
    qi = jnp.asarray([i for i, _ in pairs], jnp.int32)
    kj = jnp.asarray([j for _, j in pairs], jnp.int32)
    grid_spec = pltpu.PrefetchScalarGridSpec(
        num_scalar_prefetch=2,
        grid=(bsz, groups, len(pairs)),
        in_specs=[pl.BlockSpec((1, t, hp * d), lambda b, h, p, qi, kj: (b, qi[p], h)),
                  pl.BlockSpec((1, t, hp * d), lambda b, h, p, qi, kj: (b, kj[p], groups + h)),
                  pl.BlockSpec((1, t, hp * d), lambda b, h, p, qi, kj: (b, kj[p], 2 * groups + h)),
                  pl.BlockSpec((1, hp, 1, t), lambda b, h, p, qi, kj: (b, h, 0, kj[p]))],
        out_specs=pl.BlockSpec((1, t, hp * d), lambda b, h, p, qi, kj: (b, qi[p], h)),
        scratch_shapes=[pltpu.VMEM((hp, t, d), F32), pltpu.VMEM((hp, t, 2 * d), F32)],
    )
    return pl.pallas_call(
        functools.partial(_fox_kernel, tq=t, tk=t, hp=hp),
        grid_spec=grid_spec,
        out_shape=jax.ShapeDtypeStruct((bsz, s, heads * d), BF16),
        compiler_params=_params("parallel", "parallel", "arbitrary"),
        name="fox_mixer",
    )(qi, kj, proj, proj, proj, c_row)


def _router_kernel(x_ref, wt_ref, rb_ref, idx_ref, wts_ref, rank_ref, cnt_ref, carry_ref):
    e = wt_ref.shape[0]
    tm = x_ref.shape[0]
    per = e // N_GROUPS

    @pl.when(pl.program_id(0) == 0)
    def _():
        carry_ref[...] = jnp.zeros_like(carry_ref)

    logits = _dot_nt(wt_ref[...], x_ref[...], precision=HIGHEST)
    scores = _sigmoid(logits)
    biased = scores + rb_ref[...]
    g3 = biased.reshape(N_GROUPS, per, tm)
    mi = lax.broadcasted_iota(jnp.int32, (N_GROUPS, per, tm), 1)
    m1 = jnp.max(g3, axis=1, keepdims=True)
    i1 = jnp.min(jnp.where(g3 == m1, mi, per), axis=1, keepdims=True)
    m2 = jnp.max(jnp.where(mi == i1, -jnp.inf, g3), axis=1, keepdims=True)
    cur = (m1 + m2).reshape(N_GROUPS, tm)
    gi = lax.broadcasted_iota(jnp.int32, (N_GROUPS, tm), 0)
    sel = jnp.zeros((N_GROUPS, tm), jnp.bool_)
    for _ in range(TOPK_GROUPS):
        mx = jnp.max(cur, axis=0, keepdims=True)
        pick = gi == jnp.min(jnp.where(cur == mx, gi, N_GROUPS), axis=0, keepdims=True)
        sel = sel | pick
        cur = jnp.where(pick, -jnp.inf, cur)
    ok = jnp.broadcast_to(sel.reshape(N_GROUPS, 1, tm), (N_GROUPS, per, tm)).reshape(e, tm)
    cur = jnp.where(ok, biased, NEG_BIG)
    ei = lax.broadcasted_iota(jnp.int32, (e, tm), 0)
    chosen = jnp.zeros((e, tm), jnp.bool_)
    picks = []
    for k in range(TOP_K):
        mx = jnp.max(cur, axis=0, keepdims=True)
        ek = jnp.min(jnp.where(cur == mx, ei, e), axis=0, keepdims=True)
        pick = ei == ek
        chosen = chosen | pick
        cur = jnp.where(pick, -jnp.inf, cur)
        picks.append((ek, pick, jnp.sum(jnp.where(pick, scores, 0.0), axis=0, keepdims=True)))
    denom = sum(w for _, _, w in picks)
    si = lax.broadcasted_iota(jnp.int32, (tm, tm), 0)
    ti = lax.broadcasted_iota(jnp.int32, (tm, tm), 1)
    before = jnp.dot(chosen.astype(BF16), (si < ti).astype(BF16), preferred_element_type=F32)
    before = before + carry_ref[...]
    for k, (ek, pick, w) in enumerate(picks):
        idx_ref[k:k + 1, :] = ek
        wts_ref[k:k + 1, :] = w / denom * ROUTED_SCALE
        rank_ref[k:k + 1, :] = jnp.sum(jnp.where(pick, before, 0.0), axis=0, keepdims=True).astype(jnp.int32)
    carry_ref[...] += jnp.sum(chosen.astype(F32), axis=1, keepdims=True)
    cnt_ref[...] = carry_ref[...].astype(jnp.int32)


def router(x, w_router, router_bias, tm_pref=512):
    m, d = x.shape
    e = w_router.shape[1]
    tm = _tile(m, tm_pref, 128)
    pick = pl.BlockSpec((TOP_K, tm), lambda i: (0, i))
    return pl.pallas_call(
        _router_kernel,
        grid=(m // tm,),
        in_specs=[pl.BlockSpec((tm, d), lambda i: (i, 0)),
                  pl.BlockSpec((e, d), lambda i: (0, 0)),
                  pl.BlockSpec((e, 1), lambda i: (0, 0))],
        out_specs=[pick, pick, pick, pl.BlockSpec((e, 1), lambda i: (0, 0))],
        out_shape=[jax.ShapeDtypeStruct((TOP_K, m), jnp.int32), jax.ShapeDtypeStruct((TOP_K, m), F32),
                   jax.ShapeDtypeStruct((TOP_K, m), jnp.int32), jax.ShapeDtypeStruct((e, 1), jnp.int32)],
        scratch_shapes=[pltpu.VMEM((e, 1), F32)],
        compiler_params=_params("arbitrary"),
        name="moe_router",
    )(x, jnp.transpose(w_router).astype(F32), router_bias.reshape(e, 1).astype(F32))


def _dispatch_kernel(dst_ref, x_ref, xs_hbm, sem, *, tm):
    for k in range(TOP_K):
        for t in range(tm):
            pltpu.make_async_copy(x_ref.at[pl.ds(t, 1)], xs_hbm.at[pl.ds(dst_ref[k, t], 1)], sem).start(
                priority=t % 2)
    for k in range(TOP_K):
        pltpu.make_async_copy(x_ref, xs_hbm.at[pl.ds(0, tm)], sem).wait()


def moe_dispatch(xw, dst, tm):
    m, h = xw.shape
    return pl.pallas_call(
        functools.partial(_dispatch_kernel, tm=tm),
        grid=(m // tm,),
        in_specs=[pl.BlockSpec((TOP_K, tm), lambda i: (0, i), memory_space=pltpu.SMEM),
                  pl.BlockSpec((tm, h), lambda i: (i, 0))],
        out_specs=pl.BlockSpec(memory_space=pl.ANY),
        out_shape=jax.ShapeDtypeStruct((TOP_K * m, h), jnp.uint32),
        scratch_shapes=[pltpu.SemaphoreType.DMA(())],
        compiler_params=_params("arbitrary"),
        name="moe_dispatch",
    )(dst, xw)


def _group_metadata(counts, n_rows, t):
    e = counts.shape[0]
    n_items = n_rows // t + e - 1
    lanes = -(-n_items // 128) * 128
    plan, offs = pl.pallas_call(
        functools.partial(_plan_kernel, t=t),
        out_shape=[jax.ShapeDtypeStruct((8, lanes), jnp.int32), jax.ShapeDtypeStruct((e, 1), jnp.int32)],
        name="moe_plan",
    )(counts.reshape(e, 1))
    rows = [plan[r, :n_items] for r in range(6)]
    return (*rows, plan[6, :1]), offs.reshape(e)


def _plan_kernel(cnt_ref, plan_ref, offs_ref, *, t):
    e = cnt_ref.shape[0]
    lanes = plan_ref.shape[1]
    cnt = cnt_ref[...]
    er = lax.broadcasted_iota(jnp.int32, (e, e), 0)
    ec = lax.broadcasted_iota(jnp.int32, (e, e), 1)
    incl = (ec <= er).astype(F32)

    def cumsum(col):
        return jnp.dot(incl, col.astype(F32), precision=HIGHEST, preferred_element_type=F32).astype(jnp.int32)

    ends = cumsum(cnt)
    offs = ends - cnt
    first_tile = offs // t
    tiles_of = jnp.where(cnt > 0, (ends - 1) // t - first_tile + 1, 0)
    item_end = cumsum(tiles_of)
    used = jnp.max(item_end, axis=0, keepdims=True)
    ids = jnp.minimum(lax.broadcasted_iota(jnp.int32, (1, lanes), 1), used - 1)
    exp = jnp.minimum(jnp.sum((item_end <= ids).astype(jnp.int32), axis=0, keepdims=True), e - 1)
    onehot = lax.broadcasted_iota(jnp.int32, (e, lanes), 0) == exp

    def pick(col):
        return jnp.sum(jnp.where(onehot, col, 0), axis=0, keepdims=True)

    tile = pick(first_tile) + ids - (pick(item_end) - pick(tiles_of))
    lo = jnp.clip(pick(offs) - tile * t, 0, t)
    hi = jnp.clip(pick(ends) - tile * t, 0, t)
    lane = lax.broadcasted_iota(jnp.int32, (1, lanes), 1)
    prev_tile = jnp.where(lane == 0, -1, pltpu.roll(tile, 1, axis=1))
    prev_exp = jnp.where(lane == 0, -1, pltpu.roll(exp, 1, axis=1))
    plan_ref[0:1, :] = tile
    plan_ref[1:2, :] = exp
    plan_ref[2:3, :] = lo
    plan_ref[3:4, :] = hi
    plan_ref[4:5, :] = (tile != prev_tile).astype(jnp.int32)
    plan_ref[5:6, :] = (exp != prev_exp).astype(jnp.int32)
    plan_ref[6:7, :] = jnp.broadcast_to(used, (1, lanes))
    plan_ref[7:8, :] = jnp.zeros((1, lanes), jnp.int32)
    offs_ref[...] = offs


def _group_ffn_kernel(tile_ref, exp_ref, lo_ref, hi_ref, first_ref, newexp_ref, used_ref,
                      x_ref, wg_ref, wu_ref, wd_ref, o_ref, wgb, wub, wdb):
    i = pl.program_id(0)
    t, h = x_ref.shape

    @pl.when(i < used_ref[0])
    def _():
        @pl.when(newexp_ref[i] == 1)
        def _():
            wgb[...] = wg_ref[0, 0].astype(BF16)
            wub[...] = wu_ref[0, 0].astype(BF16)
            wdb[...] = wd_ref[0, 0].astype(BF16)

        def ffn(mine):
            left, right = _unpack_halves(x_ref[...])
            left = left.astype(BF16)
            right = right.astype(BF16)
            g = _dot_nt(left, wgb[:, :h]) + _dot_nt(right, wgb[:, h:])
            u = _dot_nt(left, wub[:, :h]) + _dot_nt(right, wub[:, h:])
            a = g * _sigmoid(g) * u
            if mine is not None:
                a = jnp.where(mine, a, 0.0)
            return _pack_halves(jnp.dot(a.astype(BF16), wdb[...], preferred_element_type=F32))

        whole = (lo_ref[i] == 0) & (hi_ref[i] == t)

        @pl.when(whole)
        def _():
            o_ref[...] = ffn(None)

        @pl.when(jnp.logical_not(whole))
        def _():
            row = lax.broadcasted_iota(jnp.int32, (t, 1), 0)
            mine = (row >= lo_ref[i]) & (row < hi_ref[i])
            y = ffn(mine)

            @pl.when(first_ref[i] == 1)
            def _():
                o_ref[...] = y

            @pl.when(first_ref[i] == 0)
            def _():
                o_ref[...] = jnp.where(mine, y, o_ref[...])


def moe_group_ffn(xs, meta, wg, wu, wd, layer, t):
    r, h = xs.shape
    _, e, f, d = wg.shape
    n_items = r // t + e - 1
    grid_spec = pltpu.PrefetchScalarGridSpec(
        num_scalar_prefetch=7,
        grid=(n_items,),
        in_specs=[pl.BlockSpec((t, h), lambda i, tile, *_: (tile[i], 0)),
                  pl.BlockSpec((1, 1, f, d), lambda i, tile, exp, *_: (layer, exp[i], 0, 0)),
                  pl.BlockSpec((1, 1, f, d), lambda i, tile, exp, *_: (layer, exp[i], 0, 0)),
                  pl.BlockSpec((1, 1, f, d), lambda i, tile, exp, *_: (layer, exp[i], 0, 0))],
        out_specs=pl.BlockSpec((t, h), lambda i, tile, *_: (tile[i], 0)),
        scratch_shapes=[pltpu.VMEM((f, d), BF16), pltpu.VMEM((f, d), BF16), pltpu.VMEM((f, d), BF16)],
    )
    return pl.pallas_call(
        _group_ffn_kernel,
        grid_spec=grid_spec,
        out_shape=jax.ShapeDtypeStruct((r, h), jnp.uint32),
        compiler_params=_params("arbitrary"),
        name="moe_group_ffn",
    )(*meta, xs, wg, wu, wd)


def _combine_kernel(dst_ref, nxt_ref, wts_ref, x_ref, xb_ref, sg_ref, su_ref, sd_ref, g_ref, b_ref, ys_hbm,
                    o_ref, ob_ref, buf, sem, *, alpha, tm, steps):
    i = pl.program_id(0)
    h = buf.shape[-1]

    def gather_pick(idx_ref, s, k):
        for t in range(tm):
            pltpu.make_async_copy(ys_hbm.at[pl.ds(idx_ref[k, t], 1)], buf.at[s, k, pl.ds(t, 1)], sem.at[s]).start()

    def wait_all(s):
        for k in range(TOP_K):
            pltpu.make_async_copy(ys_hbm.at[pl.ds(0, tm)], buf.at[s, k], sem.at[s]).wait()

    @pl.when(i == 0)
    def _():
        for k in range(TOP_K):
            gather_pick(dst_ref, 0, k)

    def step(s):
        wait_all(s)
        acc_l = jnp.zeros((tm, h), F32)
        acc_r = jnp.zeros((tm, h), F32)
        wts = jnp.transpose(wts_ref[...])
        for k in range(TOP_K):
            gather_pick(nxt_ref, 1 - s, k)
            left, right = _unpack_halves(buf[s, k])
            c = wts[:, k:k + 1]
            acc_l += c * left
            acc_r += c * right
        xb = xb_ref[...]
        hs = jnp.dot(xb, sg_ref[...], preferred_element_type=F32)
        hs = hs * _sigmoid(hs) * jnp.dot(xb, su_ref[...], preferred_element_type=F32)
        y = jnp.concatenate([acc_l, acc_r], axis=1) + jnp.dot(hs.astype(BF16), sd_ref[...],
                                                              preferred_element_type=F32)
        out = _layer_norm(alpha * x_ref[...] + y, g_ref[...], b_ref[...])
        o_ref[...] = out
        ob_ref[...] = out.astype(BF16)

        @pl.when(i == steps - 1)
        def _():
            wait_all(1 - s)

    for s in range(2):
        @pl.when(i % 2 == s)
        def _(s=s):
            step(s)


def moe_combine_ln(ys, dst_tiles, wts, x, xb, sg, su, sd, g, b, alpha, tm):
    m, d = x.shape
    h = ys.shape[1]
    fs = sg.shape[1]
    steps = m // tm
    row = pl.BlockSpec((tm, d), lambda i: (i, 0))
    vec = pl.BlockSpec((1, d), lambda i: (0, 0))
    return pl.pallas_call(
        functools.partial(_combine_kernel, alpha=alpha, tm=tm, steps=steps),
        grid=(steps,),
        in_specs=[pl.BlockSpec((TOP_K, tm), lambda i: (0, i), memory_space=pltpu.SMEM),
                  pl.BlockSpec((TOP_K, tm), lambda i: (0, jnp.minimum(i + 1, steps - 1)), memory_space=pltpu.SMEM),
                  pl.BlockSpec((TOP_K, tm), lambda i: (0, i)),
                  row, row,
                  pl.BlockSpec((d, fs), lambda i: (0, 0)),
                  pl.BlockSpec((d, fs), lambda i: (0, 0)),
                  pl.BlockSpec((fs, d), lambda i: (0, 0)),
                  vec, vec,
                  pl.BlockSpec(memory_space=pl.ANY)],
        out_specs=[row, row],
        out_shape=[jax.ShapeDtypeStruct((m, d), F32), jax.ShapeDtypeStruct((m, d), BF16)],
        scratch_shapes=[pltpu.VMEM((2, TOP_K, tm, h), jnp.uint32), pltpu.SemaphoreType.DMA((2,))],
        compiler_params=_params("arbitrary"),
        name="moe_combine_ln",
    )(dst_tiles, dst_tiles, wts, x, xb, sg, su, sd, g.reshape(1, d), b.reshape(1, d), ys)


def moe_block(xf, xb, xw, w_router, router_bias, wg, wu, wd, layer, sg, su, sd, g, b, alpha,
              t_rows=512, tm_dispatch=128, tm_combine=128):
    m = xf.shape[0]
    e = w_router.shape[1]
    t_rows = _tile(TOP_K * m, t_rows, 8)
    tm_dispatch = _tile(m, tm_dispatch, 128)
    tm_combine = _tile(m, tm_combine, 128)
    idx, wts, rank, counts = router(xf, w_router, router_bias)
    meta, offs = _group_metadata(counts.reshape(-1), TOP_K * m, t_rows)
    start = jnp.sum(jnp.where(idx[:, None, :] == jnp.arange(e, dtype=jnp.int32)[None, :, None],
                              offs[None, :, None], 0), axis=1)
    dst = (start + rank).astype(jnp.int32)
    xs = moe_dispatch(xw, dst, tm_dispatch)
    ys = moe_group_ffn(xs, meta, jnp.swapaxes(wg, 2, 3), jnp.swapaxes(wu, 2, 3), wd, layer, t_rows)
    return moe_combine_ln(ys, dst, wts, xf, xb, sg.astype(BF16), su.astype(BF16), sd.astype(BF16),
                          g, b, alpha, tm_combine)


def kernel(x, w_in_ab, b_in_ab, w_in_cd, b_in_cd, hgrn_lb_logits, hgrn_norm, sinks, rel_bias, gla_w2, gla_b,
           gla_norm, w_out, ln_g, ln_b, w_router, router_bias, w_exp_gate, w_exp_up, w_exp_down, w_sh_gate,
           w_sh_up, w_sh_down):
    bsz, s, d = x.shape
    m = bsz * s
    depth = w_out.shape[0]
    alpha = (2.0 * depth) ** 0.25
    half = d // 2
    a_heads = half // HEAD_DIM
    b_heads = half // HEAD_DIM
    b_kv_heads = b_heads // 4
    c_dk = (d // 4) // C_HEADS
    c_dv = half // C_HEADS
    d_heads = half // HEAD_DIM
    rank = gla_w2.shape[1]
    c_cols = 2 * C_HEADS * c_dk + 2 * C_HEADS * c_dv
    d_cols = 3 * d_heads * HEAD_DIM

    lbs = lower_bounds(hgrn_lb_logits)
    bias = band_bias(rel_bias)

    xf = x.reshape(m, d).astype(F32)
    xb = xf.astype(BF16)
    for l in range(depth):
        j = l // 2
        if l % 2 == 0:
            proj = matmul_bias_layer(xb, w_in_ab, j, w_in_ab.shape[2], b_in_ab[j], F32).reshape(bsz, s, -1)
            o1 = hgrn2_mixer(proj, lbs[l], hgrn_norm[j], a_heads)
            o2 = swa_mixer(proj, 4 * a_heads * HEAD_DIM, b_heads, b_kv_heads, sinks[j], bias)
        else:
            w_in = w_in_cd[j]
            b_in = b_in_cd[j]
            d0 = c_cols + rank
            f0 = d0 + d_cols
            proj_c = matmul_bias(xb, w_in[:, :c_cols].astype(BF16), b_in[:c_cols], F32).reshape(bsz, s, -1)
            qfold = jnp.where(jnp.arange(d_cols) < d_heads * HEAD_DIM, LOG2E * HEAD_DIM ** -0.5, 1.0).astype(F32)
            proj_d = matmul_bias(xb, (w_in[:, d0:f0] * qfold).astype(BF16), b_in[d0:f0] * qfold,
                                 BF16).reshape(bsz, s, -1)
            w_small = jnp.concatenate([w_in[:, c_cols:d0], w_in[:, f0:]], axis=1).astype(BF16)
            b_small = jnp.concatenate([b_in[c_cols:d0], b_in[f0:]])
            small = matmul_bias(xb, w_small, b_small, F32).reshape(bsz, s, -1)
            o1 = gla_mixer(proj_c, small[..., :rank], gla_w2[j].astype(F32), gla_b[j].astype(F32), gla_norm[j],
                           C_HEADS, c_dk, c_dv)
            o2 = fox_mixer(proj_d, fox_log_decay(small[..., rank:]), d_heads)
        mix = matmul_pair(o1.reshape(m, half), o2.reshape(m, half), w_out, l, F32)
        xf, xb, xw = add_layer_norm(xf, mix, ln_g[l, 0], ln_b[l, 0], alpha)
        xf, xb = moe_block(xf, xb, xw, w_router[l], router_bias[l], w_exp_gate, w_exp_up, w_exp_down, l,
                           w_sh_gate[l], w_sh_up[l], w_sh_down[l], ln_g[l, 1], ln_b[l, 1], alpha)
    return xf.reshape(bsz, s, d).astype(x.dtype)
```

```python
import functools
import math

import numpy as np
import jax
import jax.numpy as jnp
from jax import lax
from jax.experimental import pallas as pl
from jax.experimental.pallas import tpu as pltpu

F32 = jnp.float32
BF16 = jnp.bfloat16
HIGHEST = lax.Precision.HIGHEST

HEAD_DIM = 128
WINDOW = 128
C_HEADS = 4
GLA_TAU = 16.0
CHUNK = 64
SUB = 16
N_BUCKETS = 32
MAX_DISTANCE = 128
TOP_K = 8
N_GROUPS = 8
TOPK_GROUPS = 4
ROUTED_SCALE = 2.5
LN_EPS = 1e-5
RMS_EPS = 1e-6
NEG_BIG = -1e30

VMEM_LIMIT_BYTES = 52 * 1024 * 1024


def _params(*sem):
    return pltpu.CompilerParams(dimension_semantics=sem, vmem_limit_bytes=VMEM_LIMIT_BYTES)


def _tile(n, pref, align):
    t = min(pref, n)
    t -= t % align
    while t >= align:
        if n % t == 0:
            return t
        t -= align
    return n


def _sigmoid(x):
    return 1.0 / (1.0 + jnp.exp(-x))


def _log_sigmoid(x):
    return jnp.minimum(x, 0.0) - jnp.log(1.0 + jnp.exp(-jnp.abs(x)))


def _dot_nt(a, b, **kw):
    return lax.dot_general(a, b, (((1,), (1,)), ((), ())), preferred_element_type=F32, **kw)


def _dot_tn(a, b, **kw):
    return lax.dot_general(a, b, (((0,), (0,)), ((), ())), preferred_element_type=F32, **kw)


def _mm_kernel(x_ref, w_ref, b_ref, o_ref):
    acc = jnp.dot(x_ref[...], w_ref[...], preferred_element_type=F32)
    o_ref[...] = (acc + b_ref[...]).astype(o_ref.dtype)


def matmul_bias(x, w, b, out_dtype, tm_pref=512, tn_pref=1024):
    m, k = x.shape
    n = w.shape[1]
    tm = _tile(m, tm_pref, 8)
    tn = _tile(n, tn_pref, 128)
    return pl.pallas_call(
        _mm_kernel,
        grid=(n // tn, m // tm),
        in_specs=[pl.BlockSpec((tm, k), lambda j, i: (i, 0)),
                  pl.BlockSpec((k, tn), lambda j, i: (0, j)),
                  pl.BlockSpec((1, tn), lambda j, i: (0, j))],
        out_specs=pl.BlockSpec((tm, tn), lambda j, i: (i, j)),
        out_shape=jax.ShapeDtypeStruct((m, n), out_dtype),
        compiler_params=_params("parallel", "parallel"),
        name="matmul_bias",
    )(x, w, b.reshape(1, n).astype(F32))


def _mm_layer_kernel(x_ref, w_ref, b_ref, o_ref, wb_ref):
    @pl.when(pl.program_id(1) == 0)
    def _():
        wb_ref[...] = w_ref[0].astype(BF16)

    acc = jnp.dot(x_ref[...], wb_ref[...], preferred_element_type=F32)
    o_ref[...] = (acc + b_ref[...]).astype(o_ref.dtype)


def matmul_bias_layer(x, w_stack, layer, n_cols, b, out_dtype, tm_pref=512, tn_pref=1024):
    m, k = x.shape
    tm = _tile(m, tm_pref, 8)
    tn = _tile(n_cols, tn_pref, 128)
    return pl.pallas_call(
        _mm_layer_kernel,
        grid=(n_cols // tn, m // tm),
        in_specs=[pl.BlockSpec((tm, k), lambda j, i: (i, 0)),
                  pl.BlockSpec((1, k, tn), lambda j, i: (layer, 0, j), pipeline_mode=pl.Buffered(1)),
                  pl.BlockSpec((1, tn), lambda j, i: (0, j))],
        out_specs=pl.BlockSpec((tm, tn), lambda j, i: (i, j)),
        out_shape=jax.ShapeDtypeStruct((m, n_cols), out_dtype),
        scratch_shapes=[pltpu.VMEM((k, tn), BF16)],
        compiler_params=_params("parallel", "arbitrary"),
        name="matmul_bias_layer",
    )(x, w_stack, b.reshape(1, n_cols).astype(F32))


def _mm2_kernel(a1_ref, a2_ref, w1_ref, w2_ref, o_ref, w1b_ref, w2b_ref):
    @pl.when(pl.program_id(1) == 0)
    def _():
        w1b_ref[...] = w1_ref[0].astype(BF16)
        w2b_ref[...] = w2_ref[0].astype(BF16)

    acc = jnp.dot(a1_ref[...], w1b_ref[...], preferred_element_type=F32)
    acc += jnp.dot(a2_ref[...], w2b_ref[...], preferred_element_type=F32)
    o_ref[...] = acc.astype(o_ref.dtype)


def matmul_pair(a1, a2, w_stack, layer, out_dtype, tm_pref=512, tn_pref=1024):
    m, k1 = a1.shape
    assert a2.shape[1] == k1 and w_stack.shape[1] == 2 * k1
    n = w_stack.shape[2]
    tm = _tile(m, tm_pref, 8)
    tn = _tile(n, tn_pref, 128)
    return pl.pallas_call(
        _mm2_kernel,
        grid=(n // tn, m // tm),
        in_specs=[pl.BlockSpec((tm, k1), lambda j, i: (i, 0)),
                  pl.BlockSpec((tm, k1), lambda j, i: (i, 0)),
                  pl.BlockSpec((1, k1, tn), lambda j, i: (layer, 0, j), pipeline_mode=pl.Buffered(1)),
                  pl.BlockSpec((1, k1, tn), lambda j, i: (layer, 1, j), pipeline_mode=pl.Buffered(1))],
        out_specs=pl.BlockSpec((tm, tn), lambda j, i: (i, j)),
        out_shape=jax.ShapeDtypeStruct((m, n), out_dtype),
        scratch_shapes=[pltpu.VMEM((k1, tn), BF16), pltpu.VMEM((k1, tn), BF16)],
        compiler_params=_params("parallel", "arbitrary"),
        name="matmul_pair",
    )(a1, a2, w_stack, w_stack)


def _pack_halves(v):
    h = v.shape[1] // 2
    r = v.astype(BF16).astype(F32)
    lo = lax.bitcast_convert_type(r[:, :h], jnp.uint32) >> 16
    hi = lax.bitcast_convert_type(r[:, h:], jnp.uint32) & jnp.uint32(0xFFFF0000)
    return lo | hi


def _unpack_halves(w):
    left = lax.bitcast_convert_type(w << 16, F32)
    right = lax.bitcast_convert_type(w & jnp.uint32(0xFFFF0000), F32)
    return left, right


def _layer_norm(z, g, b):
    mu = jnp.mean(z, axis=-1, keepdims=True)
    zc = z - mu
    var = jnp.mean(zc * zc, axis=-1, keepdims=True)
    return zc * lax.rsqrt(var + LN_EPS) * g + b


def _add_ln_kernel(x_ref, y_ref, g_ref, b_ref, o_ref, ob_ref, ow_ref, *, alpha):
    out = _layer_norm(alpha * x_ref[...] + y_ref[...], g_ref[...], b_ref[...])
    o_ref[...] = out
    ob_ref[...] = out.astype(BF16)
    ow_ref[...] = _pack_halves(out)


def add_layer_norm(x, y, g, b, alpha, tm_pref=256):
    m, d = x.shape
    tm = _tile(m, tm_pref, 16)
    row = pl.BlockSpec((tm, d), lambda i: (i, 0))
    half = pl.BlockSpec((tm, d // 2), lambda i: (i, 0))
    vec = pl.BlockSpec((1, d), lambda i: (0, 0))
    return pl.pallas_call(
        functools.partial(_add_ln_kernel, alpha=alpha),
        grid=(m // tm,),
        in_specs=[row, row, vec, vec],
        out_specs=[row, row, half],
        out_shape=[jax.ShapeDtypeStruct((m, d), F32), jax.ShapeDtypeStruct((m, d), BF16),
                   jax.ShapeDtypeStruct((m, d // 2), jnp.uint32)],
        compiler_params=_params("parallel"),
        name="add_layer_norm",
    )(x, y, g.reshape(1, d), b.reshape(1, d))


def _lower_bounds_kernel(z_ref, o_ref):
    z = z_ref[...]
    depth = z.shape[0]
    e = jnp.exp(z - jnp.max(z, axis=0, keepdims=True))
    p = e / jnp.sum(e, axis=0, keepdims=True)
    run = jnp.zeros_like(p[0:1])
    for l in range(depth):
        run = run + p[l:l + 1]
        o_ref[l:l + 1, :] = run - p[0:1]


def lower_bounds(logits):
    return pl.pallas_call(
        _lower_bounds_kernel,
        out_shape=jax.ShapeDtypeStruct(logits.shape, F32),
        name="hgrn_lower_bounds",
    )(logits.astype(F32))


def _gla_chunk(q, k, v, g, st_ref):
    dk = q.shape[1]
    nsub = CHUNK // SUB
    row = lax.broadcasted_iota(jnp.int32, (CHUNK, CHUNK), 0)
    col = lax.broadcasted_iota(jnp.int32, (CHUNK, CHUNK), 1)
    tri = (col <= row).astype(F32)
    b = jnp.dot(tri, g, precision=HIGHEST, preferred_element_type=F32)
    b_last = b[CHUNK - 1:CHUNK, :]
    st = st_ref[...]

    qe = q * jnp.exp(b)
    o = _dot_nt(qe.astype(BF16), st.astype(BF16))

    levels = [jnp.zeros((1, dk), F32)] + [b[SUB * i - 1:SUB * i, :] for i in range(1, nsub)]
    lvl_rows = jnp.concatenate([jnp.broadcast_to(r, (SUB, dk)) for r in levels], axis=0)
    qt = q * jnp.exp(b - lvl_rows)
    row_blk = lax.broadcasted_iota(jnp.int32, (CHUNK, dk), 0) // SUB
    qhat = jnp.concatenate([jnp.where(row_blk == i, qt, 0.0) for i in range(1, nsub)], axis=1)
    khat = jnp.concatenate([k * jnp.exp(jnp.minimum(levels[i] - b, 0.0)) for i in range(1, nsub)], axis=1)
    p = _dot_nt(qhat.astype(BF16), khat.astype(BF16))
    p = jnp.where(col < (row // SUB) * SUB, p, 0.0)
    o += jnp.dot(p.astype(BF16), v.astype(BF16), preferred_element_type=F32)

    b4 = b.reshape(nsub, SUB, dk)
    q4 = q.reshape(nsub, SUB, dk)
    k4 = k.reshape(nsub, SUB, dk)
    ti = lax.broadcasted_iota(jnp.int32, (1, SUB, SUB, 1), 1)
    si = lax.broadcasted_iota(jnp.int32, (1, SUB, SUB, 1), 2)
    diff = b4[:, :, None, :] - b4[:, None, :, :]
    dec = jnp.exp(jnp.where(si <= ti, diff, NEG_BIG))
    dg = jnp.sum(q4[:, :, None, :] * k4[:, None, :, :] * dec, axis=-1)
    v4 = v.reshape(nsub, SUB, v.shape[1])
    od = jnp.einsum("its,isv->itv", dg.astype(BF16), v4.astype(BF16), preferred_element_type=F32)
    o += od.reshape(CHUNK, v.shape[1])

    kd = k * jnp.exp(b_last - b)
    st_ref[...] = st * jnp.exp(b_last) + _dot_tn(v.astype(BF16), kd.astype(BF16))
    return o


HGRN_HEADS_PER_STEP = 4
GLA_HEADS_PER_STEP = 2
CHUNK_UNROLL = 2


def _hgrn_kernel(q_ref, f_ref, i_ref, gate_ref, lb_ref, nw_ref, o_ref, st_ref, *, chunks, hp):
    @pl.when(pl.program_id(2) == 0)
    def _():
        st_ref[...] = jnp.zeros_like(st_ref)

    d = HEAD_DIM
    nw = nw_ref[...]

    def body(c, carry):
        rows = pl.ds(pl.multiple_of(c * CHUNK, CHUNK), CHUNK)
        for h in range(hp):
            cols = slice(h * d, (h + 1) * d)
            lb = lb_ref[:, cols]
            qa = q_ref[0, rows, cols]
            f = lb + (1.0 - lb) * _sigmoid(f_ref[0, rows, cols])
            o = _gla_chunk(qa * _sigmoid(qa), 1.0 - f, i_ref[0, rows, cols], jnp.log(f), st_ref.at[h])
            o = o * lax.rsqrt(jnp.mean(o * o, axis=-1, keepdims=True) + RMS_EPS) * nw
            o_ref[0, rows, cols] = (o * _sigmoid(gate_ref[0, rows, cols])).astype(o_ref.dtype)
        return carry

    lax.fori_loop(0, chunks, body, 0, unroll=CHUNK_UNROLL)


def hgrn2_mixer(proj, lb, norm_w, heads, t_pref=512):
    bsz, s, _ = proj.shape
    d = HEAD_DIM
    hp = HGRN_HEADS_PER_STEP
    assert heads % hp == 0
    groups = heads // hp
    t = _tile(s, t_pref, CHUNK)

    def col(group):
        return pl.BlockSpec((1, t, hp * d), lambda b, h, i, group=group: (b, i, group * groups + h))

    return pl.pallas_call(
        functools.partial(_hgrn_kernel, chunks=t // CHUNK, hp=hp),
        grid=(bsz, groups, s // t),
        in_specs=[col(0), col(1), col(2), col(3),
                  pl.BlockSpec((1, hp * d), lambda b, h, i: (0, h)),
                  pl.BlockSpec((1, d), lambda b, h, i: (0, 0))],
        out_specs=pl.BlockSpec((1, t, hp * d), lambda b, h, i: (b, i, h)),
        out_shape=jax.ShapeDtypeStruct((bsz, s, heads * d), BF16),
        scratch_shapes=[pltpu.VMEM((hp, d, d), F32)],
        compiler_params=_params("parallel", "parallel", "arbitrary"),
        name="hgrn2_mixer",
    )(proj, proj, proj, proj, lb.reshape(1, heads * d), norm_w.reshape(1, d))


def _gla_kernel(q_ref, k_ref, v_ref, gate_ref, a_ref, w2_ref, gb_ref, nw_ref, o_ref, st_ref,
                *, chunks, scale, hp, dk, dv):
    @pl.when(pl.program_id(2) == 0)
    def _():
        st_ref[...] = jnp.zeros_like(st_ref)

    nw = nw_ref[...]

    def body(c, carry):
        rows = pl.ds(pl.multiple_of(c * CHUNK, CHUNK), CHUNK)
        a = a_ref[0, rows, :]
        for h in range(hp):
            kc = slice(h * dk, (h + 1) * dk)
            vc = slice(h * dv, (h + 1) * dv)
            logit = jnp.dot(a, w2_ref[:, kc], precision=HIGHEST, preferred_element_type=F32) + gb_ref[:, kc]
            g = _log_sigmoid(logit) * (1.0 / GLA_TAU)
            o = _gla_chunk(q_ref[0, rows, kc] * scale, k_ref[0, rows, kc], v_ref[0, rows, vc], g, st_ref.at[h])
            o = o * lax.rsqrt(jnp.mean(o * o, axis=-1, keepdims=True) + RMS_EPS) * nw
            gate = gate_ref[0, rows, vc]
            o_ref[0, rows, vc] = (o * gate * _sigmoid(gate)).astype(o_ref.dtype)
        return carry

    lax.fori_loop(0, chunks, body, 0, unroll=CHUNK_UNROLL)


def gla_mixer(proj, a, w2, gb, norm_w, heads, dk, dv, t_pref=512):
    bsz, s, _ = proj.shape
    rank = a.shape[-1]
    t = _tile(s, t_pref, CHUNK)
    hp = GLA_HEADS_PER_STEP
    assert heads % hp == 0 and (2 * heads * dk) % (hp * dv) == 0
    groups = heads // hp
    v0 = 2 * heads * dk // (hp * dv)
    return pl.pallas_call(
        functools.partial(_gla_kernel, chunks=t // CHUNK, scale=dk ** -0.5, hp=hp, dk=dk, dv=dv),
        grid=(bsz, groups, s // t),
        in_specs=[pl.BlockSpec((1, t, hp * dk), lambda b, h, i: (b, i, h)),
                  pl.BlockSpec((1, t, hp * dk), lambda b, h, i: (b, i, groups + h)),
                  pl.BlockSpec((1, t, hp * dv), lambda b, h, i: (b, i, v0 + h)),
                  pl.BlockSpec((1, t, hp * dv), lambda b, h, i: (b, i, v0 + groups + h)),
                  pl.BlockSpec((1, t, rank), lambda b, h, i: (b, i, 0)),
                  pl.BlockSpec((rank, hp * dk), lambda b, h, i: (0, h)),
                  pl.BlockSpec((1, hp * dk), lambda b, h, i: (0, h)),
                  pl.BlockSpec((1, dv), lambda b, h, i: (0, 0))],
        out_specs=pl.BlockSpec((1, t, hp * dv), lambda b, h, i: (b, i, h)),
        out_shape=jax.ShapeDtypeStruct((bsz, s, heads * dv), BF16),
        scratch_shapes=[pltpu.VMEM((hp, dv, dk), F32)],
        compiler_params=_params("parallel", "parallel", "arbitrary"),
        name="gla_mixer",
    )(proj, proj, proj, proj, a, w2, gb.reshape(1, heads * dk), norm_w.reshape(1, dv))


def _t5_bucket_table():
    dist = WINDOW + np.arange(WINDOW)[:, None] - np.arange(2 * WINDOW)[None, :]
    max_exact = N_BUCKETS // 2
    d = np.maximum(dist, 0)
    large = max_exact + (np.log(np.maximum(d, 1).astype(np.float32) / max_exact)
                         / math.log(MAX_DISTANCE / max_exact) * (N_BUCKETS - max_exact)).astype(np.int32)
    large = np.minimum(large, N_BUCKETS - 1)
    return np.where(d < max_exact, d, large).astype(np.int32)


def _band_bias_kernel(rel_ref, bucket_ref, o_ref):
    h = pl.program_id(0)
    bucket = bucket_ref[...]
    acc = jnp.zeros(bucket.shape, F32)
    for n in range(N_BUCKETS):
        acc = jnp.where(bucket == n, rel_ref[n, h], acc)
    o_ref[0] = acc


def band_bias(rel_bias):
    heads = rel_bias.shape[1]
    bucket = jnp.asarray(_t5_bucket_table())
    return pl.pallas_call(
        _band_bias_kernel,
        grid=(heads,),
        in_specs=[pl.BlockSpec(memory_space=pltpu.SMEM),
                  pl.BlockSpec((WINDOW, 2 * WINDOW), lambda h: (0, 0))],
        out_specs=pl.BlockSpec((1, WINDOW, 2 * WINDOW), lambda h: (h, 0, 0)),
        out_shape=jax.ShapeDtypeStruct((heads, WINDOW, 2 * WINDOW), F32),
        name="t5_band_bias",
    )(rel_bias.astype(F32), bucket)


def _swa_kernel(sink_ref, q_ref, kp_ref, kc_ref, vp_ref, vc_ref, bias_ref, o_ref, *, group):
    n = pl.program_id(1)
    hkv = pl.program_id(2)
    w = WINDOW
    d = HEAD_DIM
    scale = d ** -0.5
    kb = jnp.concatenate([kp_ref[0], kc_ref[0]], axis=0).astype(BF16)
    vb = jnp.concatenate([vp_ref[0], vc_ref[0]], axis=0).astype(BF16)
    vb1 = jnp.concatenate([vb, jnp.ones((2 * w, d), BF16)], axis=1)
    qi = lax.broadcasted_iota(jnp.int32, (w, 2 * w), 0)
    ki = lax.broadcasted_iota(jnp.int32, (w, 2 * w), 1)
    dist = w + qi - ki
    mask = (dist >= 0) & (dist < w) & ((n * w - w + ki) >= 0)
    for gi in range(group):
        q = q_ref[0, :, gi * d:(gi + 1) * d].astype(BF16)
        s = _dot_nt(q, kb) * scale + bias_ref[gi]
        s = jnp.where(mask, s, NEG_BIG)
        sink = sink_ref[hkv * group + gi]
        m = jnp.broadcast_to(jnp.maximum(jnp.max(s, axis=-1, keepdims=True), sink), (w, d))
        p = jnp.concatenate([jnp.exp(s[:, :d] - m), jnp.exp(s[:, d:] - m)], axis=1)
        o = jnp.dot(p.astype(BF16), vb1, preferred_element_type=F32)
        o_ref[0, :, gi * d:(gi + 1) * d] = (o[:, :d] / (o[:, d:] + jnp.exp(sink - m))).astype(o_ref.dtype)


def swa_mixer(proj, col0, heads, kv_heads, sinks, bias):
    bsz, s, _ = proj.shape
    w = WINDOW
    d = HEAD_DIM
    group = heads // kv_heads
    nb = s // w
    q0 = col0 // (group * d)
    k0 = (col0 + heads * d) // d
    v0 = k0 + kv_heads
    assert col0 % (group * d) == 0

    def prev(c0):
        return pl.BlockSpec((1, w, d), lambda b, n, h: (b, jnp.maximum(n - 1, 0), c0 + h))

    def cur(c0):
        return pl.BlockSpec((1, w, d), lambda b, n, h: (b, n, c0 + h))

    return pl.pallas_call(
        functools.partial(_swa_kernel, group=group),
        grid=(bsz, nb, kv_heads),
        in_specs=[pl.BlockSpec(memory_space=pltpu.SMEM),
                  pl.BlockSpec((1, w, group * d), lambda b, n, h: (b, n, q0 + h)),
                  prev(k0), cur(k0), prev(v0), cur(v0),
                  pl.BlockSpec((group, w, 2 * w), lambda b, n, h: (h, 0, 0))],
        out_specs=pl.BlockSpec((1, w, group * d), lambda b, n, h: (b, n, h)),
        out_shape=jax.ShapeDtypeStruct((bsz, s, heads * d), BF16),
        compiler_params=_params("parallel", "parallel", "parallel"),
        name="swa_mixer",
    )(sinks.astype(F32), proj, proj, proj, proj, proj, bias)


def _fox_cumsum_kernel(f_ref, o_ref, *, per_head):
    ls = _log_sigmoid(f_ref[0])
    r = ls.shape[0]
    li = lax.broadcasted_iota(jnp.int32, (128, 128), 0)
    lj = lax.broadcasted_iota(jnp.int32, (128, 128), 1)
    within = jnp.dot(ls, (li <= lj).astype(F32), precision=HIGHEST, preferred_element_type=F32)
    total = jnp.dot(ls, jnp.ones((128, 128), F32), precision=HIGHEST, preferred_element_type=F32)
    ri = lax.broadcasted_iota(jnp.int32, (r, r), 0)
    rj = lax.broadcasted_iota(jnp.int32, (r, r), 1)
    before = ((rj < ri) & (rj // per_head == ri // per_head)).astype(F32)
    o_ref[0] = within + jnp.dot(before, total, precision=HIGHEST, preferred_element_type=F32)


def fox_log_decay(f_logit):
    bsz, s, h = f_logit.shape
    per_head = s // 128
    f = jnp.transpose(f_logit, (0, 2, 1)).reshape(bsz, h * per_head, 128)
    c = pl.pallas_call(
        functools.partial(_fox_cumsum_kernel, per_head=per_head),
        grid=(bsz,),
        in_specs=[pl.BlockSpec((1, h * per_head, 128), lambda b: (b, 0, 0))],
        out_specs=pl.BlockSpec((1, h * per_head, 128), lambda b: (b, 0, 0)),
        out_shape=jax.ShapeDtypeStruct((bsz, h * per_head, 128), F32),
        compiler_params=_params("parallel"),
        name="fox_log_decay",
    )(f)
    return c.reshape(bsz, h, s)


FOX_HEADS_PER_STEP = 16
LOG2E = math.log2(math.e)


def _fox_kernel(qi_ref, kj_ref, q_ref, k_ref, v_ref, ck_ref, o_ref, m_ref, acc_ref, *, tq, tk, hp):
    pair = pl.program_id(2)
    i = qi_ref[pair]
    j = kj_ref[pair]
    d = HEAD_DIM

    @pl.when(j == 0)
    def _():
        m_ref[...] = jnp.full_like(m_ref, NEG_BIG)
        acc_ref[...] = jnp.zeros_like(acc_ref)

    def step(masked):
        if masked:
            qpos = i * tq + lax.broadcasted_iota(jnp.int32, (tq, tk), 0)
            kpos = j * tk + lax.broadcasted_iota(jnp.int32, (tq, tk), 1)
            keep = kpos <= qpos
        ones = jnp.ones((tk, d), BF16)
        for h in range(hp):
            cols = slice(h * d, (h + 1) * d)
            s = _dot_nt(q_ref[0, :, cols], k_ref[0, :, cols]) - ck_ref[0, h] * LOG2E
            if masked:
                s = jnp.where(keep, s, NEG_BIG)
            m_prev = m_ref[h]
            m_new = jnp.maximum(m_prev, jnp.max(s, axis=-1, keepdims=True))
            a = jnp.exp2(m_prev - m_new)
            p = jnp.concatenate([jnp.exp2(s[:, c * d:(c + 1) * d] - m_new) for c in range(tk // d)], axis=1)
            v1 = jnp.concatenate([v_ref[0, :, cols], ones], axis=1)
            acc_ref[h] = jnp.concatenate([a, a], axis=1) * acc_ref[h] + jnp.dot(
                p.astype(BF16), v1, preferred_element_type=F32)
            m_ref[h] = m_new

    @pl.when(j < i)
    def _():
        step(False)

    @pl.when(j == i)
    def _():
        step(True)
        for h in range(hp):
            acc = acc_ref[h]
            o_ref[0, :, h * d:(h + 1) * d] = (acc[:, :d] / acc[:, d:]).astype(o_ref.dtype)


def fox_mixer(proj, c, heads, t_pref=512):
    bsz, s, _ = proj.shape
    d = HEAD_DIM
    hp = min(FOX_HEADS_PER_STEP, heads)
    assert heads % hp == 0
    t = _tile(s, t_pref, 128)
    nb = s // t
    groups = heads // hp
    c_row = c.reshape(bsz, heads, 1, s)
    pairs = [(i, j) for i in range(nb) for j in range(i + 1)]
    qi = jnp.asarray([i for i, _ in pairs], jnp.int32)
    kj = jnp.asarray([j for _, j in pairs], jnp.int32)
    grid_spec = pltpu.PrefetchScalarGridSpec(
        num_scalar_prefetch=2,
        grid=(bsz, groups, len(pairs)),
        in_specs=[pl.BlockSpec((1, t, hp * d), lambda b, h, p, qi, kj: (b, qi[p], h)),
                  pl.BlockSpec((1, t, hp * d), lambda b, h, p, qi, kj: (b, kj[p], groups + h)),
                  pl.BlockSpec((1, t, hp * d), lambda b, h, p, qi, kj: (b, kj[p], 2 * groups + h)),
                  pl.BlockSpec((1, hp, 1, t), lambda b, h, p, qi, kj: (b, h, 0, kj[p]))],
        out_specs=pl.BlockSpec((1, t, hp * d), lambda b, h, p, qi, kj: (b, qi[p], h)),
        scratch_shapes=[pltpu.VMEM((hp, t, d), F32), pltpu.VMEM((hp, t, 2 * d), F32)],
    )
    return pl.pallas_call(
        functools.partial(_fox_kernel, tq=t, tk=t, hp=hp),
        grid_spec=grid_spec,
        out_shape=jax.ShapeDtypeStruct((bsz, s, heads * d), BF16),
        compiler_params=_params("parallel", "parallel", "arbitrary"),
        name="fox_mixer",
    )(qi, kj, proj, proj, proj, c_row)


def _router_kernel(x_ref, wt_ref, rb_ref, idx_ref, wts_ref, rank_ref, cnt_ref, carry_ref):
    e = wt_ref.shape[0]
    tm = x_ref.shape[0]
    per = e // N_GROUPS

    @pl.when(pl.program_id(0) == 0)
    def _():
        carry_ref[...] = jnp.zeros_like(carry_ref)

    logits = _dot_nt(wt_ref[...], x_ref[...], precision=HIGHEST)
    scores = _sigmoid(logits)
    biased = scores + rb_ref[...]
    g3 = biased.reshape(N_GROUPS, per, tm)
    mi = lax.broadcasted_iota(jnp.int32, (N_GROUPS, per, tm), 1)
    m1 = jnp.max(g3, axis=1, keepdims=True)
    i1 = jnp.min(jnp.where(g3 == m1, mi, per), axis=1, keepdims=True)
    m2 = jnp.max(jnp.where(mi == i1, -jnp.inf, g3), axis=1, keepdims=True)
    cur = (m1 + m2).reshape(N_GROUPS, tm)
    gi = lax.broadcasted_iota(jnp.int32, (N_GROUPS, tm), 0)
    sel = jnp.zeros((N_GROUPS, tm), jnp.bool_)
    for _ in range(TOPK_GROUPS):
        mx = jnp.max(cur, axis=0, keepdims=True)
        pick = gi == jnp.min(jnp.where(cur == mx, gi, N_GROUPS), axis=0, keepdims=True)
        sel = sel | pick
        cur = jnp.where(pick, -jnp.inf, cur)
    ok = jnp.broadcast_to(sel.reshape(N_GROUPS, 1, tm), (N_GROUPS, per, tm)).reshape(e, tm)
    cur = jnp.where(ok, biased, NEG_BIG)
    ei = lax.broadcasted_iota(jnp.int32, (e, tm), 0)
    chosen = jnp.zeros((e, tm), jnp.bool_)
    picks = []
    for k in range(TOP_K):
        mx = jnp.max(cur, axis=0, keepdims=True)
        ek = jnp.min(jnp.where(cur == mx, ei, e), axis=0, keepdims=True)
        pick = ei == ek
        chosen = chosen | pick
        cur = jnp.where(pick, -jnp.inf, cur)
        picks.append((ek, pick, jnp.sum(jnp.where(pick, scores, 0.0), axis=0, keepdims=True)))
    denom = sum(w for _, _, w in picks)
    si = lax.broadcasted_iota(jnp.int32, (tm, tm), 0)
    ti = lax.broadcasted_iota(jnp.int32, (tm, tm), 1)
    before = jnp.dot(chosen.astype(BF16), (si < ti).astype(BF16), preferred_element_type=F32)
    before = before + carry_ref[...]
    for k, (ek, pick, w) in enumerate(picks):
        idx_ref[k:k + 1, :] = ek
        wts_ref[k:k + 1, :] = w / denom * ROUTED_SCALE
        rank_ref[k:k + 1, :] = jnp.sum(jnp.where(pick, before, 0.0), axis=0, keepdims=True).astype(jnp.int32)
    carry_ref[...] += jnp.sum(chosen.astype(F32), axis=1, keepdims=True)
    cnt_ref[...] = carry_ref[...].astype(jnp.int32)


def router(x, w_router, router_bias, tm_pref=512):
    m, d = x.shape
    e = w_router.shape[1]
    tm = _tile(m, tm_pref, 128)
    pick = pl.BlockSpec((TOP_K, tm), lambda i: (0, i))
    return pl.pallas_call(
        _router_kernel,
        grid=(m // tm,),
        in_specs=[pl.BlockSpec((tm, d), lambda i: (i, 0)),
                  pl.BlockSpec((e, d), lambda i: (0, 0)),
                  pl.BlockSpec((e, 1), lambda i: (0, 0))],
        out_specs=[pick, pick, pick, pl.BlockSpec((e, 1), lambda i: (0, 0))],
        out_shape=[jax.ShapeDtypeStruct((TOP_K, m), jnp.int32), jax.ShapeDtypeStruct((TOP_K, m), F32),
                   jax.ShapeDtypeStruct((TOP_K, m), jnp.int32), jax.ShapeDtypeStruct((e, 1), jnp.int32)],
        scratch_shapes=[pltpu.VMEM((e, 1), F32)],
        compiler_params=_params("arbitrary"),
        name="moe_router",
    )(x, jnp.transpose(w_router).astype(F32), router_bias.reshape(e, 1).astype(F32))


def _dispatch_kernel(dst_ref, x_ref, xs_hbm, sem, *, tm):
    for k in range(TOP_K):
        for t in range(tm):
            pltpu.make_async_copy(x_ref.at[pl.ds(t, 1)], xs_hbm.at[pl.ds(dst_ref[k, t], 1)], sem).start(
                priority=t % 2)
    for k in range(TOP_K):
        pltpu.make_async_copy(x_ref, xs_hbm.at[pl.ds(0, tm)], sem).wait()


def moe_dispatch(xw, dst, tm):
    m, h = xw.shape
    return pl.pallas_call(
        functools.partial(_dispatch_kernel, tm=tm),
        grid=(m // tm,),
        in_specs=[pl.BlockSpec((TOP_K, tm), lambda i: (0, i), memory_space=pltpu.SMEM),
                  pl.BlockSpec((tm, h), lambda i: (i, 0))],
        out_specs=pl.BlockSpec(memory_space=pl.ANY),
        out_shape=jax.ShapeDtypeStruct((TOP_K * m, h), jnp.uint32),
        scratch_shapes=[pltpu.SemaphoreType.DMA(())],
        compiler_params=_params("arbitrary"),
        name="moe_dispatch",
    )(dst, xw)


def _group_metadata(counts, n_rows, t):
    e = counts.shape[0]
    n_items = n_rows // t + e - 1
    lanes = -(-n_items // 128) * 128
    plan, offs = pl.pallas_call(
        functools.partial(_plan_kernel, t=t),
        out_shape=[jax.ShapeDtypeStruct((8, lanes), jnp.int32), jax.ShapeDtypeStruct((e, 1), jnp.int32)],
        name="moe_plan",
    )(counts.reshape(e, 1))
    rows = [plan[r, :n_items] for r in range(6)]
    return (*rows, plan[6, :1]), offs.reshape(e)


def _plan_kernel(cnt_ref, plan_ref, offs_ref, *, t):
    e = cnt_ref.shape[0]
    lanes = plan_ref.shape[1]
    cnt = cnt_ref[...]
    er = lax.broadcasted_iota(jnp.int32, (e, e), 0)
    ec = lax.broadcasted_iota(jnp.int32, (e, e), 1)
    incl = (ec <= er).astype(F32)

    def cumsum(col):
        return jnp.dot(incl, col.astype(F32), precision=HIGHEST, preferred_element_type=F32).astype(jnp.int32)

    ends = cumsum(cnt)
    offs = ends - cnt
    first_tile = offs // t
    tiles_of = jnp.where(cnt > 0, (ends - 1) // t - first_tile + 1, 0)
    item_end = cumsum(tiles_of)
    used = jnp.max(item_end, axis=0, keepdims=True)
    ids = jnp.minimum(lax.broadcasted_iota(jnp.int32, (1, lanes), 1), used - 1)
    exp = jnp.minimum(jnp.sum((item_end <= ids).astype(jnp.int32), axis=0, keepdims=True), e - 1)
    onehot = lax.broadcasted_iota(jnp.int32, (e, lanes), 0) == exp

    def pick(col):
        return jnp.sum(jnp.where(onehot, col, 0), axis=0, keepdims=True)

    tile = pick(first_tile) + ids - (pick(item_end) - pick(tiles_of))
    lo = jnp.clip(pick(offs) - tile * t, 0, t)
    hi = jnp.clip(pick(ends) - tile * t, 0, t)
    lane = lax.broadcasted_iota(jnp.int32, (1, lanes), 1)
    prev_tile = jnp.where(lane == 0, -1, pltpu.roll(tile, 1, axis=1))
    prev_exp = jnp.where(lane == 0, -1, pltpu.roll(exp, 1, axis=1))
    plan_ref[0:1, :] = tile
    plan_ref[1:2, :] = exp
    plan_ref[2:3, :] = lo
    plan_ref[3:4, :] = hi
    plan_ref[4:5, :] = (tile != prev_tile).astype(jnp.int32)
    plan_ref[5:6, :] = (exp != prev_exp).astype(jnp.int32)
    plan_ref[6:7, :] = jnp.broadcast_to(used, (1, lanes))
    plan_ref[7:8, :] = jnp.zeros((1, lanes), jnp.int32)
    offs_ref[...] = offs


def _group_ffn_kernel(tile_ref, exp_ref, lo_ref, hi_ref, first_ref, newexp_ref, used_ref,
                      x_ref, wg_ref, wu_ref, wd_ref, o_ref, wgb, wub, wdb):
    i = pl.program_id(0)
    t, h = x_ref.shape

    @pl.when(i < used_ref[0])
    def _():
        @pl.when(newexp_ref[i] == 1)
        def _():
            wgb[...] = wg_ref[0, 0].astype(BF16)
            wub[...] = wu_ref[0, 0].astype(BF16)
            wdb[...] = wd_ref[0, 0].astype(BF16)

        def ffn(mine):
            left, right = _unpack_halves(x_ref[...])
            left = left.astype(BF16)
            right = right.astype(BF16)
            g = _dot_nt(left, wgb[:, :h]) + _dot_nt(right, wgb[:, h:])
            u = _dot_nt(left, wub[:, :h]) + _dot_nt(right, wub[:, h:])
            a = g * _sigmoid(g) * u
            if mine is not None:
                a = jnp.where(mine, a, 0.0)
            return _pack_halves(jnp.dot(a.astype(BF16), wdb[...], preferred_element_type=F32))

        whole = (lo_ref[i] == 0) & (hi_ref[i] == t)

        @pl.when(whole)
        def _():
            o_ref[...] = ffn(None)

        @pl.when(jnp.logical_not(whole))
        def _():
            row = lax.broadcasted_iota(jnp.int32, (t, 1), 0)
            mine = (row >= lo_ref[i]) & (row < hi_ref[i])
            y = ffn(mine)

            @pl.when(first_ref[i] == 1)
            def _():
                o_ref[...] = y

            @pl.when(first_ref[i] == 0)
            def _():
                o_ref[...] = jnp.where(mine, y, o_ref[...])


def moe_group_ffn(xs, meta, wg, wu, wd, layer, t):
    r, h = xs.shape
    _, e, f, d = wg.shape
    n_items = r // t + e - 1
    grid_spec = pltpu.PrefetchScalarGridSpec(
        num_scalar_prefetch=7,
        grid=(n_items,),
        in_specs=[pl.BlockSpec((t, h), lambda i, tile, *_: (tile[i], 0)),
                  pl.BlockSpec((1, 1, f, d), lambda i, tile, exp, *_: (layer, exp[i], 0, 0)),
                  pl.BlockSpec((1, 1, f, d), lambda i, tile, exp, *_: (layer, exp[i], 0, 0)),
                  pl.BlockSpec((1, 1, f, d), lambda i, tile, exp, *_: (layer, exp[i], 0, 0))],
        out_specs=pl.BlockSpec((t, h), lambda i, tile, *_: (tile[i], 0)),
        scratch_shapes=[pltpu.VMEM((f, d), BF16), pltpu.VMEM((f, d), BF16), pltpu.VMEM((f, d), BF16)],
    )
    return pl.pallas_call(
        _group_ffn_kernel,
        grid_spec=grid_spec,
        out_shape=jax.ShapeDtypeStruct((r, h), jnp.uint32),
        compiler_params=_params("arbitrary"),
        name="moe_group_ffn",
    )(*meta, xs, wg, wu, wd)


def _combine_kernel(dst_ref, nxt_ref, wts_ref, x_ref, xb_ref, sg_ref, su_ref, sd_ref, g_ref, b_ref, ys_hbm,
                    o_ref, ob_ref, buf, sem, *, alpha, tm, steps):
    i = pl.program_id(0)
    h = buf.shape[-1]

    def gather_pick(idx_ref, s, k):
        for t in range(tm):
            pltpu.make_async_copy(ys_hbm.at[pl.ds(idx_ref[k, t], 1)], buf.at[s, k, pl.ds(t, 1)], sem.at[s]).start()

    def wait_all(s):
        for k in range(TOP_K):
            pltpu.make_async_copy(ys_hbm.at[pl.ds(0, tm)], buf.at[s, k], sem.at[s]).wait()

    @pl.when(i == 0)
    def _():
        for k in range(TOP_K):
            gather_pick(dst_ref, 0, k)

    def step(s):
        wait_all(s)
        acc_l = jnp.zeros((tm, h), F32)
        acc_r = jnp.zeros((tm, h), F32)
        wts = jnp.transpose(wts_ref[...])
        for k in range(TOP_K):
            gather_pick(nxt_ref, 1 - s, k)
            left, right = _unpack_halves(buf[s, k])
            c = wts[:, k:k + 1]
            acc_l += c * left
            acc_r += c * right
        xb = xb_ref[...]
        hs = jnp.dot(xb, sg_ref[...], preferred_element_type=F32)
        hs = hs * _sigmoid(hs) * jnp.dot(xb, su_ref[...], preferred_element_type=F32)
        y = jnp.concatenate([acc_l, acc_r], axis=1) + jnp.dot(hs.astype(BF16), sd_ref[...],
                                                              preferred_element_type=F32)
        out = _layer_norm(alpha * x_ref[...] + y, g_ref[...], b_ref[...])
        o_ref[...] = out
        ob_ref[...] = out.astype(BF16)

        @pl.when(i == steps - 1)
        def _():
            wait_all(1 - s)

    for s in range(2):
        @pl.when(i % 2 == s)
        def _(s=s):
            step(s)


def moe_combine_ln(ys, dst_tiles, wts, x, xb, sg, su, sd, g, b, alpha, tm):
    m, d = x.shape
    h = ys.shape[1]
    fs = sg.shape[1]
    steps = m // tm
    row = pl.BlockSpec((tm, d), lambda i: (i, 0))
    vec = pl.BlockSpec((1, d), lambda i: (0, 0))
    return pl.pallas_call(
        functools.partial(_combine_kernel, alpha=alpha, tm=tm, steps=steps),
        grid=(steps,),
        in_specs=[pl.BlockSpec((TOP_K, tm), lambda i: (0, i), memory_space=pltpu.SMEM),
                  pl.BlockSpec((TOP_K, tm), lambda i: (0, jnp.minimum(i + 1, steps - 1)), memory_space=pltpu.SMEM),
                  pl.BlockSpec((TOP_K, tm), lambda i: (0, i)),
                  row, row,
                  pl.BlockSpec((d, fs), lambda i: (0, 0)),
                  pl.BlockSpec((d, fs), lambda i: (0, 0)),
                  pl.BlockSpec((fs, d), lambda i: (0, 0)),
                  vec, vec,
                  pl.BlockSpec(memory_space=pl.ANY)],
        out_specs=[row, row],
        out_shape=[jax.ShapeDtypeStruct((m, d), F32), jax.ShapeDtypeStruct((m, d), BF16)],
        scratch_shapes=[pltpu.VMEM((2, TOP_K, tm, h), jnp.uint32), pltpu.SemaphoreType.DMA((2,))],
        compiler_params=_params("arbitrary"),
        name="moe_combine_ln",
    )(dst_tiles, dst_tiles, wts, x, xb, sg, su, sd, g.reshape(1, d), b.reshape(1, d), ys)


def moe_block(xf, xb, xw, w_router, router_bias, wg, wu, wd, layer, sg, su, sd, g, b, alpha,
              t_rows=512, tm_dispatch=128, tm_combine=128):
    m = xf.shape[0]
    e = w_router.shape[1]
    t_rows = _tile(TOP_K * m, t_rows, 8)
    tm_dispatch = _tile(m, tm_dispatch, 128)
    tm_combine = _tile(m, tm_combine, 128)
    idx, wts, rank, counts = router(xf, w_router, router_bias)
    meta, offs = _group_metadata(counts.reshape(-1), TOP_K * m, t_rows)
    start = jnp.sum(jnp.where(idx[:, None, :] == jnp.arange(e, dtype=jnp.int32)[None, :, None],
                              offs[None, :, None], 0), axis=1)
    dst = (start + rank).astype(jnp.int32)
    xs = moe_dispatch(xw, dst, tm_dispatch)
    ys = moe_group_ffn(xs, meta, jnp.swapaxes(wg, 2, 3), jnp.swapaxes(wu, 2, 3), wd, layer, t_rows)
    return moe_combine_ln(ys, dst, wts, xf, xb, sg.astype(BF16), su.astype(BF16), sd.astype(BF16),
                          g, b, alpha, tm_combine)


def kernel(x, w_in_ab, b_in_ab, w_in_cd, b_in_cd, hgrn_lb_logits, hgrn_norm, sinks, rel_bias, gla_w2, gla_b,
           gla_norm, w_out, ln_g, ln_b, w_router, router_bias, w_exp_gate, w_exp_up, w_exp_down, w_sh_gate,
           w_sh_up, w_sh_down):
    bsz, s, d = x.shape
    m = bsz * s
    depth = w_out.shape[0]
    alpha = (2.0 * depth) ** 0.25
    half = d // 2
    a_heads = half // HEAD_DIM
    b_heads = half // HEAD_DIM
    b_kv_heads = b_heads // 4
    c_dk = (d // 4) // C_HEADS
    c_dv = half // C_HEADS
    d_heads = half // HEAD_DIM
    rank = gla_w2.shape[1]
    c_cols = 2 * C_HEADS * c_dk + 2 * C_HEADS * c_dv
    d_cols = 3 * d_heads * HEAD_DIM

    lbs = lower_bounds(hgrn_lb_logits)
    bias = band_bias(rel_bias)

    xf = x.reshape(m, d).astype(F32)
    xb = xf.astype(BF16)
    for l in range(depth):
        j = l // 2
        if l % 2 == 0:
            proj = matmul_bias_layer(xb, w_in_ab, j, w_in_ab.shape[2], b_in_ab[j], F32).reshape(bsz, s, -1)
            o1 = hgrn2_mixer(proj, lbs[l], hgrn_norm[j], a_heads)
            o2 = swa_mixer(proj, 4 * a_heads * HEAD_DIM, b_heads, b_kv_heads, sinks[j], bias)
        else:
            w_in = w_in_cd[j]
            b_in = b_in_cd[j]
            d0 = c_cols + rank
            f0 = d0 + d_cols
            proj_c = matmul_bias(xb, w_in[:, :c_cols].astype(BF16), b_in[:c_cols], F32).reshape(bsz, s, -1)
            qfold = jnp.where(jnp.arange(d_cols) < d_heads * HEAD_DIM, LOG2E * HEAD_DIM ** -0.5, 1.0).astype(F32)
            proj_d = matmul_bias(xb, (w_in[:, d0:f0] * qfold).astype(BF16), b_in[d0:f0] * qfold,
                                 BF16).reshape(bsz, s, -1)
            w_small = jnp.concatenate([w_in[:, c_cols:d0], w_in[:, f0:]], axis=1).astype(BF16)
            b_small = jnp.concatenate([b_in[c_cols:d0], b_in[f0:]])
            small = matmul_bias(xb, w_small, b_small, F32).reshape(bsz, s, -1)
            o1 = gla_mixer(proj_c, small[..., :rank], gla_w2[j].astype(F32), gla_b[j].astype(F32), gla_norm[j],
                           C_HEADS, c_dk, c_dv)
            o2 = fox_mixer(proj_d, fox_log_decay(small[..., rank:]), d_heads)
        mix = matmul_pair(o1.reshape(m, half), o2.reshape(m, half), w_out, l, F32)
        xf, xb, xw = add_layer_norm(xf, mix, ln_g[l, 0], ln_b[l, 0], alpha)
        xf, xb = moe_block(xf, xb, xw, w_router[l], router_bias[l], w_exp_gate, w_exp_up, w_exp_down, l,
                           w_sh_gate[l], w_sh_up[l], w_sh_down[l], ln_g[l, 1], ln_b[l, 1], alpha)
    return xf.reshape(bsz, s, d).astype(x.dtype)
```
